```python
import math
import jax, jax.numpy as jnp
from jax import lax
import numpy as np

D_MODEL = 2048
BATCH = 2
SEQ = 4096
DEPTH = 4

D_MIX = D_MODEL
D_DIFF = D_MIX // 2
D_HGRN = D_MIX - D_DIFF
DIFF_HEAD_DIM = 128
DIFF_QK_DIM = DIFF_HEAD_DIM // 2
N_DIFF_HEADS = D_DIFF // DIFF_HEAD_DIM
HGRN_HEAD_DIM = 128
N_HGRN_HEADS = D_HGRN // HGRN_HEAD_DIM
HGRN_CHUNK = 64
Q_BLOCK = 128
NUM_BUCKETS = 32
MAX_DISTANCE = 128
N_EXPERTS = 16
N_GROUPS = 4
EXPERTS_PER_GROUP = N_EXPERTS // N_GROUPS
TOP_K = 2
D_EXPERT = D_MODEL // 4
N_MOD = 6
EPS = 1e-6
IN_COLS = 3 * D_DIFF + 4 * D_HGRN
SPLITS = [D_DIFF, 2 * D_DIFF, 3 * D_DIFF, 3 * D_DIFF + D_HGRN, 3 * D_DIFF + 2 * D_HGRN, 3 * D_DIFF + 3 * D_HGRN]

kernel_name = "hybrid_diffattn_hgrn2_groupedmoe_adaln"


def rms_norm(x, g):
    xf = x.astype(jnp.float32)
    y = xf * lax.rsqrt(jnp.mean(xf * xf, axis=-1, keepdims=True) + EPS)
    return (y * g.astype(jnp.float32)).astype(x.dtype)


def t5_bucket(n):
    max_exact = NUM_BUCKETS // 2
    nf = jnp.maximum(n, 1).astype(jnp.float32)
    large = max_exact + (jnp.log(nf / max_exact) / math.log(MAX_DISTANCE / max_exact)
                         * (NUM_BUCKETS - max_exact)).astype(jnp.int32)
    large = jnp.minimum(large, NUM_BUCKETS - 1)
    return jnp.where(n < max_exact, n, large)


def diff_attention(q, k, v, rel_table, lam, lambda_init, subln_g):
    B, S = q.shape[0], q.shape[1]
    nb = S // Q_BLOCK
    H2 = 2 * N_DIFF_HEADS
    qf = q.astype(jnp.float32) * (DIFF_QK_DIM ** -0.5)
    kf = k.astype(jnp.float32)
    vf = v.astype(jnp.float32)
    lam = lam.astype(jnp.float32)
    q_blocks = qf.reshape(B, nb, Q_BLOCK, H2, DIFF_QK_DIM).transpose(1, 0, 2, 3, 4)
    kpos = jnp.arange(S)

    def one_block(args):
        qb, blk = args
        qpos = blk * Q_BLOCK + jnp.arange(Q_BLOCK)
        rel = qpos[:, None] - kpos[None, :]
        causal = rel >= 0
        bias = rel_table.astype(jnp.float32)[t5_bucket(jnp.maximum(rel, 0))]
        s = jnp.einsum('bqhd,bkhd->bhqk', qb, kf) + bias.transpose(2, 0, 1)[None]
        s = jnp.where(causal[None, None], s, -1e30)
        p = jax.nn.softmax(s, axis=-1).reshape(B, N_DIFF_HEADS, 2, Q_BLOCK, S)
        a = p[:, :, 0] - lam * p[:, :, 1]
        return jnp.einsum('bhqk,bkhd->bqhd', a, vf)

    o = lax.map(one_block, (q_blocks, jnp.arange(nb)))
    o = o.transpose(1, 0, 2, 3, 4).reshape(B, S, N_DIFF_HEADS, DIFF_HEAD_DIM)
    o = rms_norm(o, subln_g) * (1.0 - lambda_init)
    return o.reshape(B, S, D_DIFF)


def hgrn2(q, f_raw, v, g, lb, norm_g):
    B, S, _ = q.shape
    H, dk, C = N_HGRN_HEADS, HGRN_HEAD_DIM, HGRN_CHUNK
    nc = S // C
    lbf = lb.astype(jnp.float32)
    fr = f_raw.astype(jnp.float32)
    f = lbf + (1.0 - lbf) * jax.nn.sigmoid(fr)
    log_f = jnp.log(jnp.maximum(f, jnp.finfo(jnp.float32).tiny))
    k = (1.0 - lbf) * jax.nn.sigmoid(-fr)

    def chunks(t):
        return t.astype(jnp.float32).reshape(B, nc, C, H, dk).transpose(1, 0, 3, 2, 4)

    xs = (chunks(q), chunks(k), chunks(v), chunks(log_f))
    causal = jnp.tril(jnp.ones((C, C), dtype=bool))[:, :, None]

    def step(state, inp):
        qc, kc, vc, lc = inp
        b = jnp.cumsum(lc, axis=2)
        b_end = b[:, :, -1:, :]
        o_inter = jnp.einsum('bhtd,bhdv->bhtv', qc * jnp.exp(b), state)
        rel = b[:, :, :, None, :] - b[:, :, None, :, :]
        decay = jnp.where(causal, jnp.exp(jnp.minimum(rel, 0.0)), 0.0)
        scores = jnp.einsum('bhtd,bhsd,bhtsd->bhts', qc, kc, decay)
        o_intra = jnp.einsum('bhts,bhsv->bhtv', scores, vc)
        new_state = (jnp.exp(b_end[:, :, 0, :])[..., None] * state
                     + jnp.einsum('bhsd,bhsv->bhdv', kc * jnp.exp(b_end - b), vc))
        return new_state, o_inter + o_intra

    state0 = jnp.zeros((B, H, dk, dk), jnp.float32)
    _, o = lax.scan(step, state0, xs)
    o = o.transpose(1, 0, 3, 2, 4).reshape(B, S, H, dk)
    o = rms_norm(o, norm_g).reshape(B, S, D_HGRN)
    return o * jax.nn.silu(g.astype(jnp.float32))


def grouped_moe(h, w_router, b_router, w_gate, w_up, w_down):
    B, S, D = h.shape
    t = h.reshape(B * S, D)
    logits = (t @ w_router).astype(jnp.float32) + b_router.astype(jnp.float32)
    probs = jax.nn.softmax(logits, axis=-1)
    grouped = probs.reshape(-1, N_GROUPS, EXPERTS_PER_GROUP)
    group_score = lax.top_k(grouped, TOP_K)[0].sum(-1)
    group_sel = jnp.argmax(group_score, axis=-1)
    expert_mask = jnp.repeat(jax.nn.one_hot(group_sel, N_GROUPS, dtype=jnp.bool_), EXPERTS_PER_GROUP, axis=-1)
    masked = jnp.where(expert_mask, probs, -1.0)
    top_vals, top_idx = lax.top_k(masked, TOP_K)
    weights = top_vals / jnp.sum(top_vals, axis=-1, keepdims=True)
    gates = jnp.sum(jax.nn.one_hot(top_idx, N_EXPERTS, dtype=jnp.float32) * weights[..., None], axis=-2)
    hg = jnp.einsum('td,edf->tef', t, w_gate)
    hu = jnp.einsum('td,edf->tef', t, w_up)
    act = jax.nn.silu(hg) * hu * gates[..., None].astype(hg.dtype)
    y = jnp.einsum('tef,efd->td', act, w_down)
    return y.reshape(B, S, D)


def setup_inputs(seed: int = 0) -> dict:
    key = jax.random.key(seed)
    ks = jax.random.split(key, 24)
    f32 = jnp.float32
    nrm = lambda k, shape, s: jax.random.normal(k, shape, f32) * s
    return {
        "x": nrm(ks[0], (BATCH, SEQ, D_MODEL), 1.0),
        "c": nrm(ks[1], (BATCH, D_MODEL), 1.0),
        "w_in": nrm(ks[2], (DEPTH, D_MODEL, IN_COLS), D_MODEL ** -0.5),
        "w_out": nrm(ks[3], (DEPTH, D_MIX, D_MODEL), D_MIX ** -0.5),
        "attn_norm": 1.0 + nrm(ks[4], (DEPTH, D_MODEL), 0.02),
        "ffn_norm": 1.0 + nrm(ks[5], (DEPTH, D_MODEL), 0.02),
        "w_ada": nrm(ks[6], (DEPTH, D_MODEL, N_MOD * D_MODEL), 0.2 * D_MODEL ** -0.5),
        "b_ada": nrm(ks[7], (DEPTH, N_MOD * D_MODEL), 0.02),
        "lambda_q1": nrm(ks[8], (DEPTH, DIFF_QK_DIM), 0.1),
        "lambda_k1": nrm(ks[9], (DEPTH, DIFF_QK_DIM), 0.1),
        "lambda_q2": nrm(ks[10], (DEPTH, DIFF_QK_DIM), 0.1),
        "lambda_k2": nrm(ks[11], (DEPTH, DIFF_QK_DIM), 0.1),
        "diff_subln": 1.0 + nrm(ks[12], (DEPTH, DIFF_HEAD_DIM), 0.02),
        "hgrn_lb": 1.0 + nrm(ks[13], (DEPTH, D_HGRN), 0.1),
        "hgrn_norm": 1.0 + nrm(ks[14], (DEPTH, HGRN_HEAD_DIM), 0.02),
        "rel_bias": nrm(ks[15], (NUM_BUCKETS, 2 * N_DIFF_HEADS), 0.5),
        "w_router": nrm(ks[16], (D_MODEL, N_EXPERTS), D_MODEL ** -0.5),
        "b_router": nrm(ks[17], (N_EXPERTS,), 0.01),
        "w_gate": nrm(ks[18], (DEPTH, N_EXPERTS, D_MODEL, D_EXPERT), D_MODEL ** -0.5),
        "w_up": nrm(ks[19], (DEPTH, N_EXPERTS, D_MODEL, D_EXPERT), D_MODEL ** -0.5),
        "w_down": nrm(ks[20], (DEPTH, N_EXPERTS, D_EXPERT, D_MODEL), D_EXPERT ** -0.5),
        "final_norm": 1.0 + nrm(ks[21], (D_MODEL,), 0.02),
    }


def reference(x, c, w_in, w_out, attn_norm, ffn_norm, w_ada, b_ada, lambda_q1, lambda_k1, lambda_q2, lambda_k2,
              diff_subln, hgrn_lb, hgrn_norm, rel_bias, w_router, b_router, w_gate, w_up, w_down, final_norm):
    B, S, _ = x.shape
    lb_soft = jax.nn.softmax(hgrn_lb.astype(jnp.float32), axis=0)
    lower_bounds = jnp.maximum(jnp.cumsum(lb_soft, axis=0) - lb_soft[0:1], 0.0)
    c_act = jax.nn.silu(c)
    for l in range(DEPTH):
        mod = c_act @ w_ada[l] + b_ada[l]
        shift1, scale1, gate1, shift2, scale2, gate2 = jnp.split(mod, N_MOD, axis=-1)
        h = rms_norm(x, attn_norm[l]) * (1.0 + scale1[:, None]) + shift1[:, None]
        proj = h @ w_in[l]
        qd, kd, vd, qh, fh, ih, gh = jnp.split(proj, SPLITS, axis=-1)
        lambda_init = 0.8 - 0.6 * math.exp(-0.3 * l)
        lam = (jnp.exp(jnp.sum(lambda_q1[l].astype(jnp.float32) * lambda_k1[l].astype(jnp.float32)))
               - jnp.exp(jnp.sum(lambda_q2[l].astype(jnp.float32) * lambda_k2[l].astype(jnp.float32)))
               + lambda_init)
        attn_out = diff_attention(qd.reshape(B, S, 2 * N_DIFF_HEADS, DIFF_QK_DIM),
                                  kd.reshape(B, S, 2 * N_DIFF_HEADS, DIFF_QK_DIM),
                                  vd.reshape(B, S, N_DIFF_HEADS, DIFF_HEAD_DIM),
                                  rel_bias, lam, lambda_init, diff_subln[l])
        hgrn_out = hgrn2(qh, fh, ih, gh, lower_bounds[l], hgrn_norm[l])
        mixed = jnp.concatenate([attn_out.astype(h.dtype), hgrn_out.astype(h.dtype)], axis=-1) @ w_out[l]
        x = x + gate1[:, None] * mixed
        h2 = rms_norm(x, ffn_norm[l]) * (1.0 + scale2[:, None]) + shift2[:, None]
        x = x + gate2[:, None] * grouped_moe(h2, w_router, b_router, w_gate[l], w_up[l], w_down[l])
    return rms_norm(x, final_norm)
```

```python
import functools
import math

import numpy as np
import jax
import jax.numpy as jnp
from jax import lax
from jax.experimental import pallas as pl
from jax.experimental.pallas import tpu as pltpu

F32 = jnp.float32
BF16 = jnp.bfloat16
EPS = 1e-6

LANES = 128
HEAD = 128
QK = 64
N_HEADS = 8
NUM_BUCKETS = 32
MAX_DISTANCE = 128
N_EXPERTS = 16
N_GROUPS = 4
GROUP_SIZE = N_EXPERTS // N_GROUPS
N_MOD = 6
NEG = -1e30

ATTN_T = 512
HGRN_L = 512
HGRN_C = 64
HGRN_SUB = 16
MOE_TM = 512
VMEM_LIMIT = 56 * 1024 * 1024


def _cparams(n_axes):
    return pltpu.CompilerParams(dimension_semantics=("arbitrary",) * n_axes, vmem_limit_bytes=VMEM_LIMIT)


def _ada_body(c_ref, w_ref, b_ref, o_ref):
    c = c_ref[...]
    ca = (c * jax.nn.sigmoid(c)).astype(BF16)
    o_ref[0] = jnp.dot(ca, w_ref[0].astype(BF16), preferred_element_type=F32) + b_ref[0]


def _ada(c8, w_ada, b_ada, tn=1024):
    depth, d, n = w_ada.shape
    return pl.pallas_call(
        _ada_body,
        grid=(depth, n // tn),
        in_specs=[pl.BlockSpec((8, d), lambda l, j: (0, 0)),
                  pl.BlockSpec((1, d, tn), lambda l, j: (l, 0, j)),
                  pl.BlockSpec((1, 1, tn), lambda l, j: (l, 0, j))],
        out_specs=pl.BlockSpec((1, 8, tn), lambda l, j: (l, 0, j)),
        out_shape=jax.ShapeDtypeStruct((depth, 8, n), F32),
        compiler_params=_cparams(2),
        name="ada_mod",
    )(c8, w_ada, b_ada.reshape(depth, 1, n))


def _modulated_norm(x, g, scale, shift):
    ms = jnp.mean(x * x, axis=-1, keepdims=True)
    return (x * lax.rsqrt(ms + EPS) * g) * (1.0 + scale) + shift


def _inproj_body(*refs, has_res):
    if has_res:
        x_ref, y_ref, gp_ref, mod_ref, g_ref, w_ref, xo_ref, p_ref, h_scr = refs
    else:
        x_ref, mod_ref, g_ref, w_ref, p_ref, h_scr = refs

    @pl.when(pl.program_id(2) == 0)
    def _():
        x = x_ref[0]
        if has_res:
            x = x + gp_ref[0, 5:6, :] * y_ref[...]
            xo_ref[0] = x
        h = _modulated_norm(x, g_ref[0], mod_ref[0, 1:2, :], mod_ref[0, 0:1, :])
        h_scr[...] = h.astype(BF16)

    p_ref[0] = jnp.dot(h_scr[...], w_ref[0], preferred_element_type=F32).astype(BF16)


def _inproj(x, y, mod_prev, mod, g_all, w_all, layer, tm=512, tn=1024):
    b, s, d = x.shape
    n = w_all.shape[-1]
    nt = s // tm
    has_res = y is not None
    xspec = pl.BlockSpec((1, tm, d), lambda bi, i, j: (bi, i, 0))
    modspec = pl.BlockSpec((1, N_MOD, d), lambda bi, i, j: (bi, 0, 0))
    in_specs = [xspec]
    args = [x]
    if has_res:
        in_specs += [pl.BlockSpec((tm, d), lambda bi, i, j: (bi * nt + i, 0)), modspec]
        args += [y, mod_prev]
    in_specs += [modspec,
                 pl.BlockSpec((1, 1, d), lambda bi, i, j: (layer, 0, 0)),
                 pl.BlockSpec((1, d, tn), lambda bi, i, j: (layer, 0, j))]
    args += [mod, g_all, w_all]
    pspec = pl.BlockSpec((1, tm, tn), lambda bi, i, j: (bi, i, j))
    pshape = jax.ShapeDtypeStruct((b, s, n), BF16)
    if has_res:
        out_specs, out_shape = [xspec, pspec], [jax.ShapeDtypeStruct(x.shape, F32), pshape]
    else:
        out_specs, out_shape = pspec, pshape
    out = pl.pallas_call(
        functools.partial(_inproj_body, has_res=has_res),
        grid=(b, nt, n // tn),
        in_specs=in_specs, out_specs=out_specs, out_shape=out_shape,
        scratch_shapes=[pltpu.VMEM((tm, d), BF16)],
        compiler_params=_cparams(3),
        name="inproj",
    )(*args)
    return (out[0], out[1]) if has_res else (x, out)


def _t5_bucket_np(n):
    max_exact = NUM_BUCKETS // 2
    nf = np.maximum(n, 1).astype(np.float32)
    large = max_exact + (np.log(nf / np.float32(max_exact)) / np.float32(math.log(MAX_DISTANCE / max_exact))
                         * np.float32(NUM_BUCKETS - max_exact)).astype(np.int32)
    large = np.minimum(large, NUM_BUCKETS - 1)
    return np.where(n < max_exact, n, large).astype(np.int32)


def _bias_body(tab_ref, bd_ref, bu_ref, o_ref):
    m = pl.program_id(0) * 2 + pl.program_id(1)
    bd = bd_ref[...]
    bu = bu_ref[...]
    far = tab_ref[NUM_BUCKETS - 1, m]
    accd = jnp.zeros(bd.shape, F32)
    accu = jnp.zeros(bu.shape, F32)
    for bkt in range(NUM_BUCKETS - 1):
        val = tab_ref[bkt, m] - far
        accd = jnp.where(bd == bkt, val, accd)
        accu = jnp.where(bu == bkt, val, accu)
    o_ref[0, 0] = jnp.where(bd < 0, NEG, accd)
    o_ref[0, 1] = accu


def _bias_tiles(rel_bias, t):
    r = np.arange(t)[:, None]
    c = np.arange(t)[None, :]
    bd = np.where(r >= c, _t5_bucket_np(np.maximum(r - c, 0)), -1).astype(np.int32)
    bu = _t5_bucket_np(t + r - c)
    assert MAX_DISTANCE <= t + 1, "tiles two or more to the left must lie in the far bucket"
    return pl.pallas_call(
        _bias_body,
        grid=(N_HEADS, 2),
        in_specs=[pl.BlockSpec(memory_space=pltpu.SMEM),
                  pl.BlockSpec((t, t), lambda h, m: (0, 0)),
                  pl.BlockSpec((t, t), lambda h, m: (0, 0))],
        out_specs=pl.BlockSpec((1, 2, t, t), lambda h, m: (h, 0, m, 0)),
        out_shape=jax.ShapeDtypeStruct((N_HEADS, 2, 2 * t, t), F32),
        compiler_params=_cparams(2),
        name="bias_tiles",
    )(rel_bias.astype(F32), jnp.asarray(bd), jnp.asarray(bu))


def _attn_body(sc_ref, q_ref, k_ref, v_ref, bias_ref, g_ref, o_ref, qq_scr, m_scr, l_scr, acc_scr, *, t):
    qi = pl.program_id(2)
    q = q_ref[0] * jnp.asarray(QK ** -0.5, BF16)
    lane = lax.broadcasted_iota(jnp.int32, q.shape, 1)
    zero = jnp.zeros_like(q)
    qq_scr[0:t, :] = jnp.where(lane < QK, q, zero)
    qq_scr[t:2 * t, :] = jnp.where(lane >= QK, q, zero)
    m_scr[...] = jnp.full(m_scr.shape, NEG, F32)
    l_scr[...] = jnp.zeros(l_scr.shape, F32)
    acc_scr[...] = jnp.zeros(acc_scr.shape, F32)

    def step(j, bias):
        start = pl.multiple_of(j * t, t)
        kb = k_ref[0, pl.ds(start, t), :]
        vb = v_ref[0, pl.ds(start, t), :]
        s = lax.dot_general(qq_scr[...], kb, (((1,), (1,)), ((), ())), preferred_element_type=F32)
        if bias is not None:
            s = s + bias
        m_prev = m_scr[...]
        m_new = jnp.maximum(m_prev, jnp.max(s, axis=-1, keepdims=True))
        alpha = jnp.exp(m_prev - m_new)
        p = jnp.exp(s - m_new)
        l_scr[...] = alpha * l_scr[...] + jnp.sum(p, axis=-1, keepdims=True)
        acc_scr[...] = alpha * acc_scr[...] + jnp.dot(p.astype(BF16), vb, preferred_element_type=F32)
        m_scr[...] = m_new

    def far_step(j, carry):
        step(j, None)
        return carry

    lax.fori_loop(0, jnp.maximum(qi - 1, 0), far_step, 0)

    @pl.when(qi >= 1)
    def _():
        step(qi - 1, bias_ref[0, 1])

    step(qi, bias_ref[0, 0])

    o_maps = acc_scr[...] / l_scr[...]
    o = o_maps[0:t] - sc_ref[0] * o_maps[t:2 * t]
    ms = jnp.mean(o * o, axis=-1, keepdims=True)
    o = (o * lax.rsqrt(ms + EPS) * g_ref[0]) * sc_ref[1]
    o_ref[0] = o.astype(BF16)


def _attention(proj, bias, scalars, subln_all, layer, t=ATTN_T):
    b, s, _ = proj.shape
    nq = s // t
    return pl.pallas_call(
        functools.partial(_attn_body, t=t),
        grid=(b, N_HEADS, nq),
        in_specs=[pl.BlockSpec(memory_space=pltpu.SMEM),
                  pl.BlockSpec((1, t, HEAD), lambda bi, h, i: (bi, i, h)),
                  pl.BlockSpec((1, s, HEAD), lambda bi, h, i: (bi, 0, N_HEADS + h)),
                  pl.BlockSpec((1, s, HEAD), lambda bi, h, i: (bi, 0, 2 * N_HEADS + h)),
                  pl.BlockSpec((1, 2, 2 * t, t), lambda bi, h, i: (h, 0, 0, 0)),
                  pl.BlockSpec((1, 1, HEAD), lambda bi, h, i: (layer, 0, 0))],
        out_specs=pl.BlockSpec((1, t, HEAD), lambda bi, h, i: (bi, i, h)),
        out_shape=jax.ShapeDtypeStruct((b, s, N_HEADS * HEAD), BF16),
        scratch_shapes=[pltpu.VMEM((2 * t, HEAD), BF16),
                        pltpu.VMEM((2 * t, 1), F32),
                        pltpu.VMEM((2 * t, 1), F32),
                        pltpu.VMEM((2 * t, HEAD), F32)],
        compiler_params=_cparams(3),
        name="diff_attention",
    )(scalars, proj, proj, proj, bias, subln_all)


def _hgrn_consts(l, c, sub):
    r = np.arange(l)[:, None]
    s = np.arange(l)[None, :]
    lcum = ((r // c == s // c) & (s <= r)).astype(np.float32)
    rc = np.arange(c)[:, None]
    sc = np.arange(c)[None, :]
    half = 2 * sub
    m16 = ((rc // half == sc // half) & (rc % half >= sub) & (sc % half < sub)).astype(np.float32)
    return jnp.asarray(lcum, BF16), jnp.asarray(m16, F32)


def _hgrn_body(q_ref, f_ref, i_ref, g_ref, lb_ref, ng_ref, lcum_ref, m16_ref, o_ref, st_scr, a_scr, oi_scr,
               *, l, c, sub):
    nchunk = l // c
    half = 2 * sub

    @pl.when(pl.program_id(2) == 0)
    def _():
        st_scr[...] = jnp.zeros(st_scr.shape, F32)

    q = q_ref[0].astype(F32)
    fr = f_ref[0].astype(F32)
    vb = i_ref[0]
    v = vb.astype(F32)
    lb = lb_ref[0]
    f = lb + (1.0 - lb) * jax.nn.sigmoid(fr)
    logf = jnp.log(jnp.maximum(f, jnp.finfo(F32).tiny))
    k = (1.0 - lb) * jax.nn.sigmoid(-fr)

    hi = logf.astype(BF16)
    lo = (logf - hi.astype(F32)).astype(BF16)
    lcum = lcum_ref[...]
    bcum = jnp.dot(lcum, hi, preferred_element_type=F32) + jnp.dot(lcum, lo, preferred_element_type=F32)

    def rows_of(arr, group, row):
        a3 = arr.reshape(l // group, group, HEAD)
        return jnp.broadcast_to(a3[:, row:row + 1, :], a3.shape).reshape(l, HEAD)

    b_end = rows_of(bcum, c, c - 1)
    ref_c = rows_of(bcum, c, c // 2 - 1)
    ref_h = rows_of(bcum, half, sub - 1)

    trow = lax.broadcasted_iota(jnp.int32, (l, 1), 0)
    up_c = (trow % c) >= (c // 2)
    q_c = jnp.where(up_c, q * jnp.exp(jnp.minimum(bcum - ref_c, 0.0)), 0.0).astype(BF16)
    k_c = jnp.where(up_c, 0.0, k * jnp.exp(jnp.minimum(ref_c - bcum, 0.0))).astype(BF16)
    q_h = (q * jnp.exp(jnp.minimum(bcum - ref_h, 0.0))).astype(BF16)
    k_h = (k * jnp.exp(jnp.minimum(ref_h - bcum, 0.0))).astype(BF16)
    q_in = (q * jnp.exp(bcum)).astype(BF16)
    k_out = (k * jnp.exp(b_end - bcum)).astype(BF16)
    dec = jnp.exp(b_end)
    m16 = m16_ref[...]

    nt = (((1,), (1,)), ((), ()))
    tn = (((0,), (0,)), ((), ()))
    for ci in range(nchunk):
        sl = slice(ci * c, (ci + 1) * c)
        a = lax.dot_general(q_c[sl], k_c[sl], nt, preferred_element_type=F32)
        a = a + m16 * lax.dot_general(q_h[sl], k_h[sl], nt, preferred_element_type=F32)
        a_scr[sl, :] = jnp.dot(a.astype(BF16), vb[sl], preferred_element_type=F32)
        st = st_scr[...]
        oi_scr[sl, :] = lax.dot_general(q_in[sl], st.astype(BF16), nt, preferred_element_type=F32)
        st_scr[...] = st * dec[ci * c:ci * c + 1, :] + lax.dot_general(vb[sl], k_out[sl], tn,
                                                                       preferred_element_type=F32)

    nb = l // sub
    b3 = bcum.reshape(nb, sub, HEAD)
    q3 = q.reshape(nb, sub, HEAD)
    k3 = k.reshape(nb, sub, HEAD)
    v3 = v.reshape(nb, sub, HEAD)
    tloc = lax.broadcasted_iota(jnp.int32, (nb, sub, 1), 1)
    od = jnp.zeros((nb, sub, HEAD), F32)
    for si in range(sub):
        e = jnp.exp(jnp.minimum(b3 - b3[:, si:si + 1, :], 0.0))
        pr = q3 * (k3[:, si:si + 1, :] * e)
        w = jnp.sum(pr, axis=-1, keepdims=True)
        w = jnp.where(tloc >= si, w, 0.0)
        od = od + w * v3[:, si:si + 1, :]

    o = a_scr[...] + oi_scr[...] + od.reshape(l, HEAD)
    ms = jnp.mean(o * o, axis=-1, keepdims=True)
    o = o * lax.rsqrt(ms + EPS) * ng_ref[0]
    gate = g_ref[0].astype(F32)
    o_ref[0] = (o * (gate * jax.nn.sigmoid(gate))).astype(BF16)


def _hgrn(proj, lower_bounds, norm_all, layer, l=HGRN_L, c=HGRN_C, sub=HGRN_SUB):
    b, s, _ = proj.shape
    lcum, m16 = _hgrn_consts(l, c, sub)
    base = 3 * N_HEADS

    def col(kind):
        return pl.BlockSpec((1, l, HEAD), lambda bi, h, i: (bi, i, base + kind * N_HEADS + h))

    return pl.pallas_call(
        functools.partial(_hgrn_body, l=l, c=c, sub=sub),
        grid=(b, N_HEADS, s // l),
        in_specs=[col(0), col(1), col(2), col(3),
                  pl.BlockSpec((1, 1, HEAD), lambda bi, h, i: (layer, 0, h)),
                  pl.BlockSpec((1, 1, HEAD), lambda bi, h, i: (layer, 0, 0)),
                  pl.BlockSpec((l, l), lambda bi, h, i: (0, 0)),
                  pl.BlockSpec((c, c), lambda bi, h, i: (0, 0))],
        out_specs=pl.BlockSpec((1, l, HEAD), lambda bi, h, i: (bi, i, h)),
        out_shape=jax.ShapeDtypeStruct((b, s, N_HEADS * HEAD), BF16),
        scratch_shapes=[pltpu.VMEM((HEAD, HEAD), F32),
                        pltpu.VMEM((l, HEAD), F32),
                        pltpu.VMEM((l, HEAD), F32)],
        compiler_params=_cparams(3),
        name="hgrn2",
    )(proj, proj, proj, proj, lower_bounds, norm_all, lcum, m16)


def _first_lane(cond, lane):
    return jnp.min(jnp.where(cond, lane, LANES), axis=-1, keepdims=True)


def _outproj_body(a_ref, hg_ref, w_ref, x_ref, mod_ref, g_ref, wrh_ref, wrl_ref, br_ref, xo_ref, h_ref, *, d):
    da = a_ref.shape[-1]
    mixed = jnp.dot(a_ref[0], w_ref[0, 0:da, :], preferred_element_type=F32)
    mixed = mixed + jnp.dot(hg_ref[0], w_ref[0, da:, :], preferred_element_type=F32)
    x = x_ref[0] + mod_ref[0, 2:3, :] * mixed
    xo_ref[0] = x
    h = _modulated_norm(x, g_ref[0], mod_ref[0, 4:5, :], mod_ref[0, 3:4, :])
    h_ref[0, :, 0:d] = h

    h_hi = h.astype(BF16)
    h_lo = (h - h_hi.astype(F32)).astype(BF16)
    logits = (jnp.dot(h_hi, wrh_ref[...], preferred_element_type=F32)
              + jnp.dot(h_lo, wrh_ref[...], preferred_element_type=F32)
              + jnp.dot(h_hi, wrl_ref[...], preferred_element_type=F32)) + br_ref[...]
    lane = lax.broadcasted_iota(jnp.int32, logits.shape, 1)
    valid = lane < N_EXPERTS
    logits = jnp.where(valid, logits, NEG)
    mx = jnp.max(logits, axis=-1, keepdims=True)
    ex = jnp.where(valid, jnp.exp(logits - mx), 0.0)
    probs = ex / jnp.sum(ex, axis=-1, keepdims=True)

    best = sel = v1 = v2 = i1 = i2 = None
    for gi in range(N_GROUPS):
        ing = (lane >= gi * GROUP_SIZE) & (lane < (gi + 1) * GROUP_SIZE)
        pg = jnp.where(ing, probs, -1.0)
        m1 = jnp.max(pg, axis=-1, keepdims=True)
        a1 = _first_lane(pg == m1, lane)
        pg2 = jnp.where(lane == a1, -1.0, pg)
        m2 = jnp.max(pg2, axis=-1, keepdims=True)
        a2 = _first_lane(pg2 == m2, lane)
        score = m1 + m2
        if gi == 0:
            best, sel, v1, v2, i1, i2 = score, jnp.zeros_like(a1), m1, m2, a1, a2
        else:
            better = score > best
            best = jnp.where(better, score, best)
            sel = jnp.where(better, gi, sel)
            v1 = jnp.where(better, m1, v1)
            v2 = jnp.where(better, m2, v2)
            i1 = jnp.where(better, a1, i1)
            i2 = jnp.where(better, a2, i2)
    tot = v1 + v2
    w1 = v1 / tot
    w2 = v2 / tot
    loc1 = i1 - sel * GROUP_SIZE
    loc2 = i2 - sel * GROUP_SIZE
    ext = jnp.where(lane == loc1, w1, 0.0) + jnp.where(lane == loc2, w2, 0.0)
    ext = jnp.where(lane == GROUP_SIZE, sel.astype(F32), ext)
    h_ref[0, :, d:] = ext


def _outproj_router(attn_o, hgrn_o, w_out_all, x, mod, ffn_norm_all, wr_hi, wr_lo, br, layer, tm=512):
    b, s, d = x.shape
    da = attn_o.shape[-1]
    dmix = w_out_all.shape[1]
    tok = lambda bi, i: (bi, i, 0)
    return pl.pallas_call(
        functools.partial(_outproj_body, d=d),
        grid=(b, s // tm),
        in_specs=[pl.BlockSpec((1, tm, da), tok),
                  pl.BlockSpec((1, tm, dmix - da), tok),
                  pl.BlockSpec((1, dmix, d), lambda bi, i: (layer, 0, 0)),
                  pl.BlockSpec((1, tm, d), tok),
                  pl.BlockSpec((1, N_MOD, d), lambda bi, i: (bi, 0, 0)),
                  pl.BlockSpec((1, 1, d), lambda bi, i: (layer, 0, 0)),
                  pl.BlockSpec((d, LANES), lambda bi, i: (0, 0)),
                  pl.BlockSpec((d, LANES), lambda bi, i: (0, 0)),
                  pl.BlockSpec((1, LANES), lambda bi, i: (0, 0))],
        out_specs=[pl.BlockSpec((1, tm, d), tok),
                   pl.BlockSpec((1, tm, d + LANES), tok)],
        out_shape=[jax.ShapeDtypeStruct((b, s, d), F32),
                   jax.ShapeDtypeStruct((b, s, d + LANES), F32)],
        compiler_params=_cparams(2),
        name="outproj_router",
    )(attn_o, hgrn_o, w_out_all, x, mod, ffn_norm_all, wr_hi, wr_lo, br)


def _moe_body(tg_ref, nv_ref, nu_ref, tok_ref, h_hbm, wg_ref, wu_ref, wd_ref, y_hbm,
              xbuf, xbf, acc, gsem, ssem, *, tm, d):
    i = pl.program_id(0)
    e = pl.program_id(1)
    base = i * tm

    def gather_copy(r, tok):
        return pltpu.make_async_copy(h_hbm.at[pl.ds(tok, 1)], xbuf.at[pl.ds(r, 1)], gsem)

    def scatter_copy(r, tok):
        return pltpu.make_async_copy(acc.at[pl.ds(r, 1)], y_hbm.at[pl.ds(tok, 1)], ssem)

    @pl.when(i < nu_ref[0])
    def _():
        @pl.when(e == 0)
        def _():
            def start(r, carry):
                gather_copy(r, tok_ref[base + r]).start()
                return carry

            def wait(r, carry):
                gather_copy(r, 0).wait()
                return carry

            lax.fori_loop(0, tm, start, 0)
            lax.fori_loop(0, tm, wait, 0)
            xbf[...] = xbuf[:, 0:d].astype(BF16)
            acc[...] = jnp.zeros(acc.shape, F32)

        ext = xbuf[:, d:d + LANES]
        gate = jnp.zeros((tm, 1), F32)
        for ei in range(GROUP_SIZE):
            gate = jnp.where(e == ei, ext[:, ei:ei + 1], gate)
        x = xbf[...]
        hg = jnp.dot(x, wg_ref[0, 0], preferred_element_type=F32)
        hu = jnp.dot(x, wu_ref[0, 0], preferred_element_type=F32)
        act = (hg * jax.nn.sigmoid(hg)) * hu * gate
        acc[...] += jnp.dot(act.astype(BF16), wd_ref[0, 0], preferred_element_type=F32)

        @pl.when(e == GROUP_SIZE - 1)
        def _():
            nvalid = nv_ref[i]

            def start(r, carry):
                scatter_copy(r, tok_ref[base + r]).start()
                return carry

            def wait(r, carry):
                scatter_copy(r, 0).wait()
                return carry

            lax.fori_loop(0, nvalid, start, 0)
            lax.fori_loop(0, nvalid, wait, 0)


def _route(hext, d, tm):
    b, s, _ = hext.shape
    t = b * s
    n_tiles = t // tm + N_GROUPS
    gid = hext[:, :, d + GROUP_SIZE].reshape(t).astype(jnp.int32)
    onehot = (gid[:, None] == jnp.arange(N_GROUPS, dtype=jnp.int32)[None, :]).astype(jnp.int32)
    cnt = jnp.sum(onehot, axis=0)
    rank = jnp.sum(onehot * (jnp.cumsum(onehot, axis=0) - onehot), axis=1)
    ntile_g = (cnt + tm - 1) // tm
    tile_end_g = jnp.cumsum(ntile_g)
    tile_start_g = tile_end_g - ntile_g
    pos = (tile_start_g * tm)[gid] + rank
    tok_of_row = jnp.zeros((n_tiles * tm,), jnp.int32).at[pos].set(jnp.arange(t, dtype=jnp.int32))
    tile = jnp.arange(n_tiles, dtype=jnp.int32)
    tile_gid = jnp.minimum(jnp.sum((tile[:, None] >= tile_end_g[None, :]).astype(jnp.int32), axis=1), N_GROUPS - 1)
    nvalid = jnp.clip(cnt[tile_gid] - (tile - tile_start_g[tile_gid]) * tm, 0, tm).astype(jnp.int32)
    n_used = tile_end_g[N_GROUPS - 1:].astype(jnp.int32)
    return tile_gid.astype(jnp.int32), nvalid, n_used, tok_of_row


def _moe(hext, wg_all, wu_all, wd_all, layer, tm=MOE_TM):
    b, s, dx = hext.shape
    d = dx - LANES
    t = b * s
    f = wg_all.shape[-1]
    tile_gid, nvalid, n_used, tok_of_row = _route(hext, d, tm)
    n_tiles = tile_gid.shape[0]

    def wmap(i, e, tg, nv, nu, tok):
        return (layer, tg[i] * GROUP_SIZE + e, 0, 0)

    return pl.pallas_call(
        functools.partial(_moe_body, tm=tm, d=d),
        grid_spec=pltpu.PrefetchScalarGridSpec(
            num_scalar_prefetch=4,
            grid=(n_tiles, GROUP_SIZE),
            in_specs=[pl.BlockSpec(memory_space=pl.ANY),
                      pl.BlockSpec((1, 1, d, f), wmap),
                      pl.BlockSpec((1, 1, d, f), wmap),
                      pl.BlockSpec((1, 1, f, d), wmap)],
            out_specs=pl.BlockSpec(memory_space=pl.ANY),
            scratch_shapes=[pltpu.VMEM((tm, dx), F32),
                            pltpu.VMEM((tm, d), BF16),
                            pltpu.VMEM((tm, d), F32),
                            pltpu.SemaphoreType.DMA,
                            pltpu.SemaphoreType.DMA],
        ),
        out_shape=jax.ShapeDtypeStruct((t, d), F32),
        compiler_params=_cparams(2),
        name="moe_experts",
    )(tile_gid, nvalid, n_used, tok_of_row, hext.reshape(t, dx), wg_all, wu_all, wd_all)


def _final_body(x_ref, y_ref, mod_ref, g_ref, o_ref):
    x = x_ref[0] + mod_ref[0, 5:6, :] * y_ref[...]
    ms = jnp.mean(x * x, axis=-1, keepdims=True)
    o_ref[0] = x * lax.rsqrt(ms + EPS) * g_ref[...]


def _final(x, y, mod, g, tm=512):
    b, s, d = x.shape
    nt = s // tm
    return pl.pallas_call(
        _final_body,
        grid=(b, nt),
        in_specs=[pl.BlockSpec((1, tm, d), lambda bi, i: (bi, i, 0)),
                  pl.BlockSpec((tm, d), lambda bi, i: (bi * nt + i, 0)),
                  pl.BlockSpec((1, N_MOD, d), lambda bi, i: (bi, 0, 0)),
                  pl.BlockSpec((1, d), lambda bi, i: (0, 0))],
        out_specs=pl.BlockSpec((1, tm, d), lambda bi, i: (bi, i, 0)),
        out_shape=jax.ShapeDtypeStruct(x.shape, F32),
        compiler_params=_cparams(2),
        name="final_norm",
    )(x, y, mod, g.reshape(1, d))


def kernel(x, c, w_in, w_out, attn_norm, ffn_norm, w_ada, b_ada, lambda_q1, lambda_k1, lambda_q2, lambda_k2,
           diff_subln, hgrn_lb, hgrn_norm, rel_bias, w_router, b_router, w_gate, w_up, w_down, final_norm):
    b, s, d = x.shape
    depth = w_in.shape[0]

    w_in_b = w_in.astype(BF16)
    w_out_b = w_out.astype(BF16)
    w_gate_b = w_gate.astype(BF16)
    w_up_b = w_up.astype(BF16)
    w_down_b = w_down.astype(BF16)
    wr = jnp.pad(w_router.astype(F32), ((0, 0), (0, LANES - N_EXPERTS)))
    wr_hi = wr.astype(BF16)
    wr_lo = (wr - wr_hi.astype(F32)).astype(BF16)
    br = jnp.pad(b_router.astype(F32), (0, LANES - N_EXPERTS)).reshape(1, LANES)
    lb_soft = jax.nn.softmax(hgrn_lb.astype(F32), axis=0)
    lower_bounds = jnp.maximum(jnp.cumsum(lb_soft, axis=0) - lb_soft[0:1], 0.0).reshape(depth, 1, -1)
    attn_norm3 = attn_norm.astype(F32).reshape(depth, 1, d)
    ffn_norm3 = ffn_norm.astype(F32).reshape(depth, 1, d)
    subln3 = diff_subln.astype(F32).reshape(depth, 1, HEAD)
    hnorm3 = hgrn_norm.astype(F32).reshape(depth, 1, HEAD)

    c8 = jnp.pad(c.astype(F32), ((0, 8 - b), (0, 0)))
    mods = _ada(c8, w_ada, b_ada)
    bias = _bias_tiles(rel_bias, ATTN_T)

    y = None
    mod_prev = None
    for layer in range(depth):
        mod = mods[layer, :b].reshape(b, N_MOD, d)
        lambda_init = 0.8 - 0.6 * math.exp(-0.3 * layer)
        lam = (jnp.exp(jnp.sum(lambda_q1[layer].astype(F32) * lambda_k1[layer].astype(F32)))
               - jnp.exp(jnp.sum(lambda_q2[layer].astype(F32) * lambda_k2[layer].astype(F32)))
               + lambda_init)
        scalars = jnp.stack([lam, jnp.asarray(1.0 - lambda_init, F32)]).astype(F32)

        x, proj = _inproj(x, y, mod_prev, mod, attn_norm3, w_in_b, layer)
        attn_o = _attention(proj, bias, scalars, subln3, layer)
        hgrn_o = _hgrn(proj, lower_bounds, hnorm3, layer)
        x, hext = _outproj_router(attn_o, hgrn_o, w_out_b, x, mod, ffn_norm3, wr_hi, wr_lo, br, layer)
        y = _moe(hext, w_gate_b, w_up_b, w_down_b, layer)
        mod_prev = mod
    return _final(x, y, mod_prev, final_norm.astype(F32))
```

```python
import functools
import math

import numpy as np
import jax
import jax.numpy as jnp
from jax import lax
from jax.experimental import pallas as pl
from jax.experimental.pallas import tpu as pltpu

F32 = jnp.float32
BF16 = jnp.bfloat16
EPS = 1e-6

LANES = 128
HEAD = 128
QK = 64
N_HEADS = 8
NUM_BUCKETS = 32
MAX_DISTANCE = 128
N_EXPERTS = 16
N_GROUPS = 4
GROUP_SIZE = N_EXPERTS // N_GROUPS
N_MOD = 6
NEG = -1e30

ATTN_T = 512
ATTN_CW = 1024
ONES_ROWS = 16
LOG2E = math.log2(math.e)
HGRN_L = 512
HGRN_C = 64
HGRN_SUB = 16
MOE_TM = 512
MOE_ISSUE_UNROLL = 32
VMEM_LIMIT = 56 * 1024 * 1024


def _cparams(n_axes):
    return pltpu.CompilerParams(dimension_semantics=("arbitrary",) * n_axes, vmem_limit_bytes=VMEM_LIMIT)


def _ada_body(c_ref, w_ref, b_ref, o_ref):
    c = c_ref[...]
    ca = (c * jax.nn.sigmoid(c)).astype(BF16)
    o_ref[0] = jnp.dot(ca, w_ref[0].astype(BF16), preferred_element_type=F32) + b_ref[0]


def _ada(c8, w_ada, b_ada, tn=1024):
    depth, d, n = w_ada.shape
    return pl.pallas_call(
        _ada_body,
        grid=(depth, n // tn),
        in_specs=[pl.BlockSpec((8, d), lambda l, j: (0, 0)),
                  pl.BlockSpec((1, d, tn), lambda l, j: (l, 0, j)),
                  pl.BlockSpec((1, 1, tn), lambda l, j: (l, 0, j))],
        out_specs=pl.BlockSpec((1, 8, tn), lambda l, j: (l, 0, j)),
        out_shape=jax.ShapeDtypeStruct((depth, 8, n), F32),
        compiler_params=_cparams(2),
        name="ada_mod",
    )(c8, w_ada, b_ada.reshape(depth, 1, n))


def _modulated_norm(x, g, scale, shift):
    ms = jnp.mean(x * x, axis=-1, keepdims=True)
    return (x * lax.rsqrt(ms + EPS) * g) * (1.0 + scale) + shift


def _inproj_body(*refs, has_res):
    if has_res:
        x_ref, y_ref, gp_ref, mod_ref, g_ref, w_ref, xo_ref, p_ref, h_scr = refs
    else:
        x_ref, mod_ref, g_ref, w_ref, p_ref, h_scr = refs

    @pl.when(pl.program_id(2) == 0)
    def _():
        x = x_ref[0]
        if has_res:
            x = x + gp_ref[0, 5:6, :] * y_ref[...]
            xo_ref[0] = x
        h = _modulated_norm(x, g_ref[0], mod_ref[0, 1:2, :], mod_ref[0, 0:1, :])
        h_scr[...] = h.astype(BF16)

    p_ref[0] = jnp.dot(h_scr[...], w_ref[0], preferred_element_type=F32).astype(BF16)


def _inproj(x, y, mod_prev, mod, g_all, w_all, layer, tm=512, tn=1024):
    b, s, d = x.shape
    n = w_all.shape[-1]
    nt = s // tm
    has_res = y is not None
    xspec = pl.BlockSpec((1, tm, d), lambda bi, i, j: (bi, i, 0))
    modspec = pl.BlockSpec((1, N_MOD, d), lambda bi, i, j: (bi, 0, 0))
    in_specs = [xspec]
    args = [x]
    if has_res:
        in_specs += [pl.BlockSpec((tm, d), lambda bi, i, j: (bi * nt + i, 0)), modspec]
        args += [y, mod_prev]
    in_specs += [modspec,
                 pl.BlockSpec((1, 1, d), lambda bi, i, j: (layer, 0, 0)),
                 pl.BlockSpec((1, d, tn), lambda bi, i, j: (layer, 0, j))]
    args += [mod, g_all, w_all]
    pspec = pl.BlockSpec((1, tm, tn), lambda bi, i, j: (bi, i, j))
    pshape = jax.ShapeDtypeStruct((b, s, n), BF16)
    if has_res:
        out_specs, out_shape = [xspec, pspec], [jax.ShapeDtypeStruct(x.shape, F32), pshape]
    else:
        out_specs, out_shape = pspec, pshape
    out = pl.pallas_call(
        functools.partial(_inproj_body, has_res=has_res),
        grid=(b, nt, n // tn),
        in_specs=in_specs, out_specs=out_specs, out_shape=out_shape,
        scratch_shapes=[pltpu.VMEM((tm, d), BF16)],
        compiler_params=_cparams(3),
        name="inproj",
    )(*args)
    return (out[0], out[1]) if has_res else (x, out)


def _t5_bucket_np(n):
    max_exact = NUM_BUCKETS // 2
    nf = np.maximum(n, 1).astype(np.float32)
    large = max_exact + (np.log(nf / np.float32(max_exact)) / np.float32(math.log(MAX_DISTANCE / max_exact))
                         * np.float32(NUM_BUCKETS - max_exact)).astype(np.int32)
    large = np.minimum(large, NUM_BUCKETS - 1)
    return np.where(n < max_exact, n, large).astype(np.int32)


def _bias_body(tab_ref, bd_ref, bu_ref, o_ref):
    m = pl.program_id(0) * 2 + pl.program_id(1)
    bd = bd_ref[...]
    bu = bu_ref[...]
    far = tab_ref[NUM_BUCKETS - 1, m]
    accd = jnp.zeros(bd.shape, F32)
    accu = jnp.zeros(bu.shape, F32)
    for bkt in range(NUM_BUCKETS - 1):
        val = (tab_ref[bkt, m] - far) * LOG2E
        accd = jnp.where(bd == bkt, val, accd)
        accu = jnp.where(bu == bkt, val, accu)
    o_ref[0, 0] = jnp.where(bd < 0, NEG, accd)
    o_ref[0, 1] = accu


def _bias_tiles(rel_bias, t):
    key = np.arange(t)[:, None]
    qry = np.arange(t)[None, :]
    bd = np.where(qry >= key, _t5_bucket_np(np.maximum(qry - key, 0)), -1).astype(np.int32)
    bu = _t5_bucket_np(t + qry - key)
    assert MAX_DISTANCE <= t + 1, "tiles two or more to the left must lie in the far bucket"
    return pl.pallas_call(
        _bias_body,
        grid=(N_HEADS, 2),
        in_specs=[pl.BlockSpec(memory_space=pltpu.SMEM),
                  pl.BlockSpec((t, t), lambda h, m: (0, 0)),
                  pl.BlockSpec((t, t), lambda h, m: (0, 0))],
        out_specs=pl.BlockSpec((1, 2, t, t), lambda h, m: (h, 0, 0, m)),
        out_shape=jax.ShapeDtypeStruct((N_HEADS, 2, t, 2 * t), F32),
        compiler_params=_cparams(2),
        name="bias_tiles",
    )(rel_bias.astype(F32), jnp.asarray(bd), jnp.asarray(bu))


def _attn_body(sc_ref, q_ref, k_ref, v_ref, bias_ref, g_ref, o_ref, qq_scr, vt_scr, m_scr, acc_scr, *, t, cw):
    qi = pl.program_id(2)
    s_len = k_ref.shape[1]

    @pl.when(qi == 0)
    def _():
        for ci in range(s_len // t):
            vt_scr[0:HEAD, ci * t:(ci + 1) * t] = v_ref[0, ci * t:(ci + 1) * t, :].T
        row = lax.broadcasted_iota(jnp.int32, (ONES_ROWS, s_len), 0)
        vt_scr[HEAD:HEAD + ONES_ROWS, :] = jnp.where(row == 0, 1.0, 0.0).astype(BF16)

    q = (q_ref[0].astype(F32) * (QK ** -0.5 * LOG2E)).astype(BF16)
    lane = lax.broadcasted_iota(jnp.int32, q.shape, 1)
    zero = jnp.zeros_like(q)
    qq_scr[0:t, :] = jnp.where(lane < QK, q, zero)
    qq_scr[t:2 * t, :] = jnp.where(lane >= QK, q, zero)
    m_scr[...] = jnp.full(m_scr.shape, NEG, F32)
    acc_scr[...] = jnp.zeros(acc_scr.shape, F32)

    def step(j, bias_of):
        start = pl.multiple_of(j * t, t)
        kb = k_ref[0, pl.ds(start, t), :]
        vt = vt_scr[:, pl.ds(start, t)]
        for ct in range(2 * t // cw):
            cs = slice(ct * cw, (ct + 1) * cw)
            s = lax.dot_general(kb, qq_scr[cs, :], (((1,), (1,)), ((), ())), preferred_element_type=F32)
            if bias_of is not None:
                s = s + bias_of(cs)
            m_prev = m_scr[:, cs]
            m_new = jnp.maximum(m_prev, jnp.max(s, axis=0, keepdims=True))
            alpha = jnp.exp2(m_prev - m_new)
            p = jnp.exp2(s - m_new)
            acc_scr[:, cs] = alpha * acc_scr[:, cs] + jnp.dot(vt, p.astype(BF16), preferred_element_type=F32)
            m_scr[:, cs] = m_new

    def far_step(j, carry):
        step(j, None)
        return carry

    lax.fori_loop(0, jnp.maximum(qi - 1, 0), far_step, 0)

    @pl.when(qi >= 1)
    def _():
        step(qi - 1, lambda cs: bias_ref[0, 1, :, cs])

    step(qi, lambda cs: bias_ref[0, 0, :, cs])

    o_maps = acc_scr[0:HEAD, :] / acc_scr[HEAD:HEAD + 1, :]
    o = o_maps[:, 0:t] - sc_ref[0] * o_maps[:, t:2 * t]
    ms = jnp.mean(o * o, axis=0, keepdims=True)
    o = (o * lax.rsqrt(ms + EPS)).T
    o_ref[0] = ((o * g_ref[0]) * sc_ref[1]).astype(BF16)


def _attention(proj, bias, scalars, subln_all, layer, t=ATTN_T):
    b, s, _ = proj.shape
    nq = s // t
    return pl.pallas_call(
        functools.partial(_attn_body, t=t, cw=ATTN_CW),
        grid=(b, N_HEADS, nq),
        in_specs=[pl.BlockSpec(memory_space=pltpu.SMEM),
                  pl.BlockSpec((1, t, HEAD), lambda bi, h, i: (bi, i, h)),
                  pl.BlockSpec((1, s, HEAD), lambda bi, h, i: (bi, 0, N_HEADS + h)),
                  pl.BlockSpec((1, s, HEAD), lambda bi, h, i: (bi, 0, 2 * N_HEADS + h)),
                  pl.BlockSpec((1, 2, t, 2 * t), lambda bi, h, i: (h, 0, 0, 0)),
                  pl.BlockSpec((1, 1, HEAD), lambda bi, h, i: (layer, 0, 0))],
        out_specs=pl.BlockSpec((1, t, HEAD), lambda bi, h, i: (bi, i, h)),
        out_shape=jax.ShapeDtypeStruct((b, s, N_HEADS * HEAD), BF16),
        scratch_shapes=[pltpu.VMEM((2 * t, HEAD), BF16),
                        pltpu.VMEM((HEAD + ONES_ROWS, s), BF16),
                        pltpu.VMEM((1, 2 * t), F32),
                        pltpu.VMEM((HEAD + ONES_ROWS, 2 * t), F32)],
        compiler_params=_cparams(3),
        name="diff_attention",
    )(scalars, proj, proj, proj, bias, subln_all)


def _hgrn_consts(l, c, sub):
    r = np.arange(l)[:, None]
    s = np.arange(l)[None, :]
    lcum = ((r // c == s // c) & (s <= r)).astype(np.float32)
    rc = np.arange(c)[:, None]
    sc = np.arange(c)[None, :]
    half = 2 * sub
    m16 = ((rc // half == sc // half) & (rc % half >= sub) & (sc % half < sub)).astype(np.float32)
    return jnp.asarray(lcum, BF16), jnp.asarray(m16, F32)


def _hgrn_body(q_ref, f_ref, i_ref, g_ref, lb_ref, ng_ref, lcum_ref, m16_ref, o_ref, st_scr, a_scr, oi_scr,
               *, l, c, sub):
    nchunk = l // c
    half = 2 * sub

    @pl.when(pl.program_id(2) == 0)
    def _():
        st_scr[...] = jnp.zeros(st_scr.shape, F32)

    q = q_ref[0].astype(F32)
    fr = f_ref[0].astype(F32)
    vb = i_ref[0]
    v = vb.astype(F32)
    lb = lb_ref[0]
    f = lb + (1.0 - lb) * jax.nn.sigmoid(fr)
    logf = jnp.log(jnp.maximum(f, jnp.finfo(F32).tiny))
    k = (1.0 - lb) * jax.nn.sigmoid(-fr)

    hi = logf.astype(BF16)
    lo = (logf - hi.astype(F32)).astype(BF16)
    lcum = lcum_ref[...]
    bcum = jnp.dot(lcum, hi, preferred_element_type=F32) + jnp.dot(lcum, lo, preferred_element_type=F32)

    def rows_of(arr, group, row):
        a3 = arr.reshape(l // group, group, HEAD)
        return jnp.broadcast_to(a3[:, row:row + 1, :], a3.shape).reshape(l, HEAD)

    b_end = rows_of(bcum, c, c - 1)
    ref_c = rows_of(bcum, c, c // 2 - 1)
    ref_h = rows_of(bcum, half, sub - 1)

    trow = lax.broadcasted_iota(jnp.int32, (l, 1), 0)
    up_c = (trow % c) >= (c // 2)
    q_c = jnp.where(up_c, q * jnp.exp(jnp.minimum(bcum - ref_c, 0.0)), 0.0).astype(BF16)
    k_c = jnp.where(up_c, 0.0, k * jnp.exp(jnp.minimum(ref_c - bcum, 0.0))).astype(BF16)
    q_h = (q * jnp.exp(jnp.minimum(bcum - ref_h, 0.0))).astype(BF16)
    k_h = (k * jnp.exp(jnp.minimum(ref_h - bcum, 0.0))).astype(BF16)
    q_in = (q * jnp.exp(bcum)).astype(BF16)
    k_out = (k * jnp.exp(b_end - bcum)).astype(BF16)
    dec = jnp.exp(b_end)
    m16 = m16_ref[...]

    nt = (((1,), (1,)), ((), ()))
    tn = (((0,), (0,)), ((), ()))
    for ci in range(nchunk):
        sl = slice(ci * c, (ci + 1) * c)
        a = lax.dot_general(q_c[sl], k_c[sl], nt, preferred_element_type=F32)
        a = a + m16 * lax.dot_general(q_h[sl], k_h[sl], nt, preferred_element_type=F32)
        a_scr[sl, :] = jnp.dot(a.astype(BF16), vb[sl], preferred_element_type=F32)
        st = st_scr[...]
        oi_scr[sl, :] = lax.dot_general(q_in[sl], st.astype(BF16), nt, preferred_element_type=F32)
        st_scr[...] = st * dec[ci * c:ci * c + 1, :] + lax.dot_general(vb[sl], k_out[sl], tn,
                                                                       preferred_element_type=F32)

    nb = l // sub
    b3 = bcum.reshape(nb, sub, HEAD)
    q3 = q.reshape(nb, sub, HEAD)
    k3 = k.reshape(nb, sub, HEAD)
    v3 = v.reshape(nb, sub, HEAD)
    tloc = lax.broadcasted_iota(jnp.int32, (nb, sub, 1), 1)
    od = jnp.zeros((nb, sub, HEAD), F32)
    for si in range(sub):
        e = jnp.exp(jnp.minimum(b3 - b3[:, si:si + 1, :], 0.0))
        pr = q3 * (k3[:, si:si + 1, :] * e)
        w = jnp.sum(pr, axis=-1, keepdims=True)
        w = jnp.where(tloc >= si, w, 0.0)
        od = od + w * v3[:, si:si + 1, :]

    o = a_scr[...] + oi_scr[...] + od.reshape(l, HEAD)
    ms = jnp.mean(o * o, axis=-1, keepdims=True)
    o = o * lax.rsqrt(ms + EPS) * ng_ref[0]
    gate = g_ref[0].astype(F32)
    o_ref[0] = (o * (gate * jax.nn.sigmoid(gate))).astype(BF16)


def _hgrn(proj, lower_bounds, norm_all, layer, l=HGRN_L, c=HGRN_C, sub=HGRN_SUB):
    b, s, _ = proj.shape
    lcum, m16 = _hgrn_consts(l, c, sub)
    base = 3 * N_HEADS

    def col(kind):
        return pl.BlockSpec((1, l, HEAD), lambda bi, h, i: (bi, i, base + kind * N_HEADS + h))

    return pl.pallas_call(
        functools.partial(_hgrn_body, l=l, c=c, sub=sub),
        grid=(b, N_HEADS, s // l),
        in_specs=[col(0), col(1), col(2), col(3),
                  pl.BlockSpec((1, 1, HEAD), lambda bi, h, i: (layer, 0, h)),
                  pl.BlockSpec((1, 1, HEAD), lambda bi, h, i: (layer, 0, 0)),
                  pl.BlockSpec((l, l), lambda bi, h, i: (0, 0)),
                  pl.BlockSpec((c, c), lambda bi, h, i: (0, 0))],
        out_specs=pl.BlockSpec((1, l, HEAD), lambda bi, h, i: (bi, i, h)),
        out_shape=jax.ShapeDtypeStruct((b, s, N_HEADS * HEAD), BF16),
        scratch_shapes=[pltpu.VMEM((HEAD, HEAD), F32),
                        pltpu.VMEM((l, HEAD), F32),
                        pltpu.VMEM((l, HEAD), F32)],
        compiler_params=_cparams(3),
        name="hgrn2",
    )(proj, proj, proj, proj, lower_bounds, norm_all, lcum, m16)


def _first_lane(cond, lane):
    return jnp.min(jnp.where(cond, lane, LANES), axis=-1, keepdims=True)


def _outproj_body(a_ref, hg_ref, w_ref, x_ref, mod_ref, g_ref, wrh_ref, wrl_ref, br_ref, xo_ref, h_ref, *, d):
    da = a_ref.shape[-1]
    mixed = jnp.dot(a_ref[0], w_ref[0, 0:da, :], preferred_element_type=F32)
    mixed = mixed + jnp.dot(hg_ref[0], w_ref[0, da:, :], preferred_element_type=F32)
    x = x_ref[0] + mod_ref[0, 2:3, :] * mixed
    xo_ref[0] = x
    h = _modulated_norm(x, g_ref[0], mod_ref[0, 4:5, :], mod_ref[0, 3:4, :])
    h_ref[0, :, 0:d] = h

    h_hi = h.astype(BF16)
    h_lo = (h - h_hi.astype(F32)).astype(BF16)
    logits = (jnp.dot(h_hi, wrh_ref[...], preferred_element_type=F32)
              + jnp.dot(h_lo, wrh_ref[...], preferred_element_type=F32)
              + jnp.dot(h_hi, wrl_ref[...], preferred_element_type=F32)) + br_ref[...]
    lane = lax.broadcasted_iota(jnp.int32, logits.shape, 1)
    valid = lane < N_EXPERTS
    logits = jnp.where(valid, logits, NEG)
    mx = jnp.max(logits, axis=-1, keepdims=True)
    ex = jnp.where(valid, jnp.exp(logits - mx), 0.0)
    probs = ex / jnp.sum(ex, axis=-1, keepdims=True)

    best = sel = v1 = v2 = i1 = i2 = None
    for gi in range(N_GROUPS):
        ing = (lane >= gi * GROUP_SIZE) & (lane < (gi + 1) * GROUP_SIZE)
        pg = jnp.where(ing, probs, -1.0)
        m1 = jnp.max(pg, axis=-1, keepdims=True)
        a1 = _first_lane(pg == m1, lane)
        pg2 = jnp.where(lane == a1, -1.0, pg)
        m2 = jnp.max(pg2, axis=-1, keepdims=True)
        a2 = _first_lane(pg2 == m2, lane)
        score = m1 + m2
        if gi == 0:
            best, sel, v1, v2, i1, i2 = score, jnp.zeros_like(a1), m1, m2, a1, a2
        else:
            better = score > best
            best = jnp.where(better, score, best)
            sel = jnp.where(better, gi, sel)
            v1 = jnp.where(better, m1, v1)
            v2 = jnp.where(better, m2, v2)
            i1 = jnp.where(better, a1, i1)
            i2 = jnp.where(better, a2, i2)
    tot = v1 + v2
    w1 = v1 / tot
    w2 = v2 / tot
    loc1 = i1 - sel * GROUP_SIZE
    loc2 = i2 - sel * GROUP_SIZE
    ext = jnp.where(lane == loc1, w1, 0.0) + jnp.where(lane == loc2, w2, 0.0)
    ext = jnp.where(lane == GROUP_SIZE, sel.astype(F32), ext)
    h_ref[0, :, d:] = ext


def _outproj_router(attn_o, hgrn_o, w_out_all, x, mod, ffn_norm_all, wr_hi, wr_lo, br, layer, tm=512):
    b, s, d = x.shape
    da = attn_o.shape[-1]
    dmix = w_out_all.shape[1]
    tok = lambda bi, i: (bi, i, 0)
    return pl.pallas_call(
        functools.partial(_outproj_body, d=d),
        grid=(b, s // tm),
        in_specs=[pl.BlockSpec((1, tm, da), tok),
                  pl.BlockSpec((1, tm, dmix - da), tok),
                  pl.BlockSpec((1, dmix, d), lambda bi, i: (layer, 0, 0)),
                  pl.BlockSpec((1, tm, d), tok),
                  pl.BlockSpec((1, N_MOD, d), lambda bi, i: (bi, 0, 0)),
                  pl.BlockSpec((1, 1, d), lambda bi, i: (layer, 0, 0)),
                  pl.BlockSpec((d, LANES), lambda bi, i: (0, 0)),
                  pl.BlockSpec((d, LANES), lambda bi, i: (0, 0)),
                  pl.BlockSpec((1, LANES), lambda bi, i: (0, 0))],
        out_specs=[pl.BlockSpec((1, tm, d), tok),
                   pl.BlockSpec((1, tm, d + LANES), tok)],
        out_shape=[jax.ShapeDtypeStruct((b, s, d), F32),
                   jax.ShapeDtypeStruct((b, s, d + LANES), F32)],
        compiler_params=_cparams(2),
        name="outproj_router",
    )(attn_o, hgrn_o, w_out_all, x, mod, ffn_norm_all, wr_hi, wr_lo, br)


def _moe_body(tg_ref, nv_ref, nu_ref, tok_ref, h_hbm, wg_ref, wu_ref, wd_ref, y_hbm,
              xbuf, xbf, acc, gsem, ssem, *, tm, d):
    i = pl.program_id(0)
    e = pl.program_id(1)
    n_used = nu_ref[0]
    slot = lax.rem(i, 2)

    def gather_start(tile, sl):
        base = tile * tm

        def body(r, carry):
            pltpu.make_async_copy(h_hbm.at[pl.ds(tok_ref[base + r], 1)], xbuf.at[sl, pl.ds(r, 1)], gsem.at[sl]).start()
            return carry

        lax.fori_loop(0, tm, body, 0, unroll=MOE_ISSUE_UNROLL)

    def gather_wait(sl):
        pltpu.make_async_copy(h_hbm.at[pl.ds(0, tm)], xbuf.at[sl], gsem.at[sl]).wait()

    def scatter_start(tile, sl, n):
        base = tile * tm

        def body(r, carry):
            pltpu.make_async_copy(acc.at[sl, pl.ds(r, 1)], y_hbm.at[pl.ds(tok_ref[base + r], 1)], ssem.at[sl]).start()
            return carry

        @pl.when(n == tm)
        def _():
            lax.fori_loop(0, tm, body, 0, unroll=MOE_ISSUE_UNROLL)

        @pl.when(n < tm)
        def _():
            lax.fori_loop(0, n, body, 0)

    def scatter_wait(sl, n):
        @pl.when(n == tm)
        def _():
            pltpu.make_async_copy(acc.at[sl], y_hbm.at[pl.ds(0, tm)], ssem.at[sl]).wait()

        @pl.when(n < tm)
        def _():
            def body(r, carry):
                pltpu.make_async_copy(acc.at[sl, pl.ds(r, 1)], y_hbm.at[pl.ds(r, 1)], ssem.at[sl]).wait()
                return carry

            lax.fori_loop(0, n, body, 0)

    @pl.when(i < n_used)
    def _():
        @pl.when(e == 0)
        def _():
            @pl.when(i == 0)
            def _():
                gather_start(0, 0)

            gather_wait(slot)

            @pl.when(i + 1 < n_used)
            def _():
                gather_start(i + 1, 1 - slot)

            xbf[...] = xbuf[slot, :, 0:d].astype(BF16)

            @pl.when(i >= 2)
            def _():
                scatter_wait(slot, nv_ref[jnp.maximum(i - 2, 0)])

            acc[slot] = jnp.zeros((tm, d), F32)

        ext = xbuf[slot, :, d:d + LANES]
        gate = jnp.zeros((tm, 1), F32)
        for ei in range(GROUP_SIZE):
            gate = jnp.where(e == ei, ext[:, ei:ei + 1], gate)
        x = xbf[...]
        hg = jnp.dot(x, wg_ref[0, 0], preferred_element_type=F32)
        hu = jnp.dot(x, wu_ref[0, 0], preferred_element_type=F32)
        act = (hg * jax.nn.sigmoid(hg)) * hu * gate
        acc[slot] += jnp.dot(act.astype(BF16), wd_ref[0, 0], preferred_element_type=F32)

        @pl.when(e == GROUP_SIZE - 1)
        def _():
            scatter_start(i, slot, nv_ref[i])

            @pl.when(i == n_used - 1)
            def _():
                scatter_wait(slot, nv_ref[i])

                @pl.when(i >= 1)
                def _():
                    scatter_wait(1 - slot, nv_ref[jnp.maximum(i - 1, 0)])


def _route(hext, d, tm):
    b, s, _ = hext.shape
    t = b * s
    n_tiles = t // tm + N_GROUPS
    gid = hext[:, :, d + GROUP_SIZE].reshape(t).astype(jnp.int32)
    onehot = (gid[:, None] == jnp.arange(N_GROUPS, dtype=jnp.int32)[None, :]).astype(jnp.int32)
    cnt = jnp.sum(onehot, axis=0)
    rank = jnp.sum(onehot * (jnp.cumsum(onehot, axis=0) - onehot), axis=1)
    ntile_g = (cnt + tm - 1) // tm
    tile_end_g = jnp.cumsum(ntile_g)
    tile_start_g = tile_end_g - ntile_g
    pos = (tile_start_g * tm)[gid] + rank
    tok_of_row = jnp.zeros((n_tiles * tm,), jnp.int32).at[pos].set(jnp.arange(t, dtype=jnp.int32))
    tile = jnp.arange(n_tiles, dtype=jnp.int32)
    tile_gid = jnp.minimum(jnp.sum((tile[:, None] >= tile_end_g[None, :]).astype(jnp.int32), axis=1), N_GROUPS - 1)
    nvalid = jnp.clip(cnt[tile_gid] - (tile - tile_start_g[tile_gid]) * tm, 0, tm).astype(jnp.int32)
    n_used = tile_end_g[N_GROUPS - 1:].astype(jnp.int32)
    return tile_gid.astype(jnp.int32), nvalid, n_used, tok_of_row


def _moe(hext, wg_all, wu_all, wd_all, layer, tm=MOE_TM):
    b, s, dx = hext.shape
    d = dx - LANES
    t = b * s
    f = wg_all.shape[-1]
    tile_gid, nvalid, n_used, tok_of_row = _route(hext, d, tm)
    n_tiles = tile_gid.shape[0]

    def wmap(i, e, tg, nv, nu, tok):
        return (layer, tg[i] * GROUP_SIZE + e, 0, 0)

    return pl.pallas_call(
        functools.partial(_moe_body, tm=tm, d=d),
        grid_spec=pltpu.PrefetchScalarGridSpec(
            num_scalar_prefetch=4,
            grid=(n_tiles, GROUP_SIZE),
            in_specs=[pl.BlockSpec(memory_space=pl.ANY),
                      pl.BlockSpec((1, 1, d, f), wmap),
                      pl.BlockSpec((1, 1, d, f), wmap),
                      pl.BlockSpec((1, 1, f, d), wmap)],
            out_specs=pl.BlockSpec(memory_space=pl.ANY),
            scratch_shapes=[pltpu.VMEM((2, tm, dx), F32),
                            pltpu.VMEM((tm, d), BF16),
                            pltpu.VMEM((2, tm, d), F32),
                            pltpu.SemaphoreType.DMA((2,)),
                            pltpu.SemaphoreType.DMA((2,))],
        ),
        out_shape=jax.ShapeDtypeStruct((t, d), F32),
        compiler_params=_cparams(2),
        name="moe_experts",
    )(tile_gid, nvalid, n_used, tok_of_row, hext.reshape(t, dx), wg_all, wu_all, wd_all)


def _final_body(x_ref, y_ref, mod_ref, g_ref, o_ref):
    x = x_ref[0] + mod_ref[0, 5:6, :] * y_ref[...]
    ms = jnp.mean(x * x, axis=-1, keepdims=True)
    o_ref[0] = x * lax.rsqrt(ms + EPS) * g_ref[...]


def _final(x, y, mod, g, tm=512):
    b, s, d = x.shape
    nt = s // tm
    return pl.pallas_call(
        _final_body,
        grid=(b, nt),
        in_specs=[pl.BlockSpec((1, tm, d), lambda bi, i: (bi, i, 0)),
                  pl.BlockSpec((tm, d), lambda bi, i: (bi * nt + i, 0)),
                  pl.BlockSpec((1, N_MOD, d), lambda bi, i: (bi, 0, 0)),
                  pl.BlockSpec((1, d), lambda bi, i: (0, 0))],
        out_specs=pl.BlockSpec((1, tm, d), lambda bi, i: (bi, i, 0)),
        out_shape=jax.ShapeDtypeStruct(x.shape, F32),
        compiler_params=_cparams(2),
        name="final_norm",
    )(x, y, mod, g.reshape(1, d))


def kernel(x, c, w_in, w_out, attn_norm, ffn_norm, w_ada, b_ada, lambda_q1, lambda_k1, lambda_q2, lambda_k2,
           diff_subln, hgrn_lb, hgrn_norm, rel_bias, w_router, b_router, w_gate, w_up, w_down, final_norm):
    b, s, d = x.shape
    depth = w_in.shape[0]

    w_in_b = w_in.astype(BF16)
    w_out_b = w_out.astype(BF16)
    w_gate_b = w_gate.astype(BF16)
    w_up_b = w_up.astype(BF16)
    w_down_b = w_down.astype(BF16)
    wr = jnp.pad(w_router.astype(F32), ((0, 0), (0, LANES - N_EXPERTS)))
    wr_hi = wr.astype(BF16)
    wr_lo = (wr - wr_hi.astype(F32)).astype(BF16)
    br = jnp.pad(b_router.astype(F32), (0, LANES - N_EXPERTS)).reshape(1, LANES)
    lb_soft = jax.nn.softmax(hgrn_lb.astype(F32), axis=0)
    lower_bounds = jnp.maximum(jnp.cumsum(lb_soft, axis=0) - lb_soft[0:1], 0.0).reshape(depth, 1, -1)
    attn_norm3 = attn_norm.astype(F32).reshape(depth, 1, d)
    ffn_norm3 = ffn_norm.astype(F32).reshape(depth, 1, d)
    subln3 = diff_subln.astype(F32).reshape(depth, 1, HEAD)
    hnorm3 = hgrn_norm.astype(F32).reshape(depth, 1, HEAD)

    c8 = jnp.pad(c.astype(F32), ((0, 8 - b), (0, 0)))
    mods = _ada(c8, w_ada, b_ada)
    bias = _bias_tiles(rel_bias, ATTN_T)

    y = None
    mod_prev = None
    for layer in range(depth):
        mod = mods[layer, :b].reshape(b, N_MOD, d)
        lambda_init = 0.8 - 0.6 * math.exp(-0.3 * layer)
        lam = (jnp.exp(jnp.sum(lambda_q1[layer].astype(F32) * lambda_k1[layer].astype(F32)))
               - jnp.exp(jnp.sum(lambda_q2[layer].astype(F32) * lambda_k2[layer].astype(F32)))
               + lambda_init)
        scalars = jnp.stack([lam, jnp.asarray(1.0 - lambda_init, F32)]).astype(F32)

        x, proj = _inproj(x, y, mod_prev, mod, attn_norm3, w_in_b, layer)
        attn_o = _attention(proj, bias, scalars, subln3, layer)
        hgrn_o = _hgrn(proj, lower_bounds, hnorm3, layer)
        x, hext = _outproj_router(attn_o, hgrn_o, w_out_b, x, mod, ffn_norm3, wr_hi, wr_lo, br, layer)
        y = _moe(hext, w_gate_b, w_up_b, w_down_b, layer)
        mod_prev = mod
    return _final(x, y, mod_prev, final_norm.astype(F32))
```

```python
import functools
import math

import numpy as np
import jax
import jax.numpy as jnp
from jax import lax
from jax.experimental import pallas as pl
from jax.experimental.pallas import tpu as pltpu

F32 = jnp.float32
BF16 = jnp.bfloat16
EPS = 1e-6

LANES = 128
HEAD = 128
QK = 64
N_HEADS = 8
NUM_BUCKETS = 32
MAX_DISTANCE = 128
N_EXPERTS = 16
N_GROUPS = 4
GROUP_SIZE = N_EXPERTS // N_GROUPS
N_MOD = 6
NEG = -1e30

ATTN_T = 512
ATTN_CW = 512
ONES_ROWS = 16
LOG2E = math.log2(math.e)
HGRN_L = 512
HGRN_C = 64
HGRN_SUB = 8
HGRN_GROUP = 256
HGRN_SAFE_SPAN = 80.0
NORM_ROWS = 256
MOE_TM = 512
MOE_ISSUE_UNROLL = 32
VMEM_LIMIT = 56 * 1024 * 1024


def _cparams(n_axes):
    return pltpu.CompilerParams(dimension_semantics=("arbitrary",) * n_axes, vmem_limit_bytes=VMEM_LIMIT)


def _ada_body(c_ref, w_ref, b_ref, o_ref):
    c = c_ref[...]
    ca = (c * jax.nn.sigmoid(c)).astype(BF16)
    o_ref[0] = jnp.dot(ca, w_ref[0].astype(BF16), preferred_element_type=F32) + b_ref[0]


def _ada(c8, w_ada, b_ada, tn=1024):
    depth, d, n = w_ada.shape
    return pl.pallas_call(
        _ada_body,
        grid=(depth, n // tn),
        in_specs=[pl.BlockSpec((8, d), lambda l, j: (0, 0)),
                  pl.BlockSpec((1, d, tn), lambda l, j: (l, 0, j)),
                  pl.BlockSpec((1, 1, tn), lambda l, j: (l, 0, j))],
        out_specs=pl.BlockSpec((1, 8, tn), lambda l, j: (l, 0, j)),
        out_shape=jax.ShapeDtypeStruct((depth, 8, n), F32),
        compiler_params=_cparams(2),
        name="ada_mod",
    )(c8, w_ada, b_ada.reshape(depth, 1, n))


def _modulated_norm(x, g, scale, shift):
    ms = jnp.mean(x * x, axis=-1, keepdims=True)
    return (x * lax.rsqrt(ms + EPS) * g) * (1.0 + scale) + shift


def _inproj_body(*refs, has_res):
    if has_res:
        x_ref, y_ref, gp_ref, mod_ref, g_ref, w_ref, xo_ref, p_ref, h_scr = refs
    else:
        x_ref, mod_ref, g_ref, w_ref, p_ref, h_scr = refs

    @pl.when(pl.program_id(2) == 0)
    def _():
        tm = h_scr.shape[0]
        for r0 in range(0, tm, NORM_ROWS):
            rows = slice(r0, r0 + NORM_ROWS)
            x = x_ref[0, rows, :]
            if has_res:
                x = x + gp_ref[0, 5:6, :] * y_ref[rows, :]
                xo_ref[0, rows, :] = x
            h = _modulated_norm(x, g_ref[0], mod_ref[0, 1:2, :], mod_ref[0, 0:1, :])
            h_scr[rows, :] = h.astype(BF16)

    p_ref[0] = jnp.dot(h_scr[...], w_ref[0], preferred_element_type=F32).astype(BF16)


def _inproj(x, y, mod_prev, mod, g_all, w_all, layer, tm=1024, tn=1024):
    b, s, d = x.shape
    n = w_all.shape[-1]
    nt = s // tm
    has_res = y is not None
    xspec = pl.BlockSpec((1, tm, d), lambda bi, i, j: (bi, i, 0))
    modspec = pl.BlockSpec((1, N_MOD, d), lambda bi, i, j: (bi, 0, 0))
    once = dict(pipeline_mode=pl.Buffered(1))
    in_specs = [pl.BlockSpec((1, tm, d), lambda bi, i, j: (bi, i, 0), **once)]
    args = [x]
    if has_res:
        in_specs += [pl.BlockSpec((tm, d), lambda bi, i, j: (bi * nt + i, 0), **once), modspec]
        args += [y, mod_prev]
    in_specs += [modspec,
                 pl.BlockSpec((1, 1, d), lambda bi, i, j: (layer, 0, 0)),
                 pl.BlockSpec((1, d, tn), lambda bi, i, j: (layer, 0, j))]
    args += [mod, g_all, w_all]
    pspec = pl.BlockSpec((1, tm, tn), lambda bi, i, j: (bi, i, j))
    pshape = jax.ShapeDtypeStruct((b, s, n), BF16)
    if has_res:
        out_specs, out_shape = [xspec, pspec], [jax.ShapeDtypeStruct(x.shape, F32), pshape]
    else:
        out_specs, out_shape = pspec, pshape
    out = pl.pallas_call(
        functools.partial(_inproj_body, has_res=has_res),
        grid=(b, nt, n // tn),
        in_specs=in_specs, out_specs=out_specs, out_shape=out_shape,
        scratch_shapes=[pltpu.VMEM((tm, d), BF16)],
        compiler_params=_cparams(3),
        name="inproj",
    )(*args)
    return (out[0], out[1]) if has_res else (x, out)


def _t5_bucket_np(n):
    max_exact = NUM_BUCKETS // 2
    nf = np.maximum(n, 1).astype(np.float32)
    large = max_exact + (np.log(nf / np.float32(max_exact)) / np.float32(math.log(MAX_DISTANCE / max_exact))
                         * np.float32(NUM_BUCKETS - max_exact)).astype(np.int32)
    large = np.minimum(large, NUM_BUCKETS - 1)
    return np.where(n < max_exact, n, large).astype(np.int32)


def _bias_body(tab_ref, bd_ref, bu_ref, o_ref):
    m = pl.program_id(0) * 2 + pl.program_id(1)
    bd = bd_ref[...]
    bu = bu_ref[...]
    far = tab_ref[NUM_BUCKETS - 1, m]
    accd = jnp.zeros(bd.shape, F32)
    accu = jnp.zeros(bu.shape, F32)
    for bkt in range(NUM_BUCKETS - 1):
        val = (tab_ref[bkt, m] - far) * LOG2E
        accd = jnp.where(bd == bkt, val, accd)
        accu = jnp.where(bu == bkt, val, accu)
    o_ref[0, 0] = jnp.where(bd < 0, NEG, accd)
    o_ref[0, 1] = accu


def _bias_tiles(rel_bias, t):
    key = np.arange(t)[:, None]
    qry = np.arange(t)[None, :]
    bd = np.where(qry >= key, _t5_bucket_np(np.maximum(qry - key, 0)), -1).astype(np.int32)
    bu = _t5_bucket_np(t + qry - key)
    assert MAX_DISTANCE <= t + 1, "tiles two or more to the left must lie in the far bucket"
    return pl.pallas_call(
        _bias_body,
        grid=(N_HEADS, 2),
        in_specs=[pl.BlockSpec(memory_space=pltpu.SMEM),
                  pl.BlockSpec((t, t), lambda h, m: (0, 0)),
                  pl.BlockSpec((t, t), lambda h, m: (0, 0))],
        out_specs=pl.BlockSpec((1, 2, t, t), lambda h, m: (h, 0, 0, m)),
        out_shape=jax.ShapeDtypeStruct((N_HEADS, 2, t, 2 * t), F32),
        compiler_params=_cparams(2),
        name="bias_tiles",
    )(rel_bias.astype(F32), jnp.asarray(bd), jnp.asarray(bu))


def _attn_body(sc_ref, q_ref, k_ref, v_ref, bias_ref, g_ref, o_ref, qq_scr, vt_scr, m_scr, acc_scr, *, t, cw):
    qi = pl.program_id(2)
    s_len = k_ref.shape[1]

    @pl.when(qi == 0)
    def _():
        for ci in range(s_len // t):
            vt_scr[0:HEAD, ci * t:(ci + 1) * t] = v_ref[0, ci * t:(ci + 1) * t, :].T
        row = lax.broadcasted_iota(jnp.int32, (ONES_ROWS, s_len), 0)
        vt_scr[HEAD:HEAD + ONES_ROWS, :] = jnp.where(row == 0, 1.0, 0.0).astype(BF16)

    q = (q_ref[0].astype(F32) * (QK ** -0.5 * LOG2E)).astype(BF16)
    lane = lax.broadcasted_iota(jnp.int32, q.shape, 1)
    zero = jnp.zeros_like(q)
    qq_scr[0:t, :] = jnp.where(lane < QK, q, zero)
    qq_scr[t:2 * t, :] = jnp.where(lane >= QK, q, zero)
    m_scr[...] = jnp.full(m_scr.shape, NEG, F32)
    acc_scr[...] = jnp.zeros(acc_scr.shape, F32)

    def step(j, bias_of):
        start = pl.multiple_of(j * t, t)
        kb = k_ref[0, pl.ds(start, t), :]
        vt = vt_scr[:, pl.ds(start, t)]
        tiles = [slice(ct * cw, (ct + 1) * cw) for ct in range(2 * t // cw)]
        scores = [lax.dot_general(kb, qq_scr[cs, :], (((1,), (1,)), ((), ())), preferred_element_type=F32)
                  for cs in tiles]
        for cs, s in zip(tiles, scores):
            if bias_of is not None:
                s = s + bias_of(cs)
            m_prev = m_scr[:, cs]
            m_new = jnp.maximum(m_prev, jnp.max(s, axis=0, keepdims=True))
            alpha = jnp.exp2(m_prev - m_new)
            p = jnp.exp2(s - m_new)
            acc_scr[:, cs] = alpha * acc_scr[:, cs] + jnp.dot(vt, p.astype(BF16), preferred_element_type=F32)
            m_scr[:, cs] = m_new

    def far_step(j, carry):
        step(j, None)
        return carry

    lax.fori_loop(0, jnp.maximum(qi - 1, 0), far_step, 0)

    @pl.when(qi >= 1)
    def _():
        step(qi - 1, lambda cs: bias_ref[0, 1, :, cs])

    step(qi, lambda cs: bias_ref[0, 0, :, cs])

    o_maps = acc_scr[0:HEAD, :] / acc_scr[HEAD:HEAD + 1, :]
    o = o_maps[:, 0:t] - sc_ref[0] * o_maps[:, t:2 * t]
    ms = jnp.mean(o * o, axis=0, keepdims=True)
    o = (o * lax.rsqrt(ms + EPS)).T
    o_ref[0] = ((o * g_ref[0]) * sc_ref[1]).astype(BF16)


def _attention(proj, bias, scalars, subln_all, layer, t=ATTN_T):
    b, s, _ = proj.shape
    nq = s // t
    return pl.pallas_call(
        functools.partial(_attn_body, t=t, cw=ATTN_CW),
        grid=(b, N_HEADS, nq),
        in_specs=[pl.BlockSpec(memory_space=pltpu.SMEM),
                  pl.BlockSpec((1, t, HEAD), lambda bi, h, i: (bi, i, h)),
                  pl.BlockSpec((1, s, HEAD), lambda bi, h, i: (bi, 0, N_HEADS + h)),
                  pl.BlockSpec((1, s, HEAD), lambda bi, h, i: (bi, 0, 2 * N_HEADS + h)),
                  pl.BlockSpec((1, 2, t, 2 * t), lambda bi, h, i: (h, 0, 0, 0)),
                  pl.BlockSpec((1, 1, HEAD), lambda bi, h, i: (layer, 0, 0))],
        out_specs=pl.BlockSpec((1, t, HEAD), lambda bi, h, i: (bi, i, h)),
        out_shape=jax.ShapeDtypeStruct((b, s, N_HEADS * HEAD), BF16),
        scratch_shapes=[pltpu.VMEM((2 * t, HEAD), BF16),
                        pltpu.VMEM((HEAD + ONES_ROWS, s), BF16),
                        pltpu.VMEM((1, 2 * t), F32),
                        pltpu.VMEM((HEAD + ONES_ROWS, 2 * t), F32)],
        compiler_params=_cparams(3),
        name="diff_attention",
    )(scalars, proj, proj, proj, bias, subln_all)


def _hgrn_levels(c, sub):
    levels = []
    g = c // 2
    while g >= sub:
        levels.append(g)
        g //= 2
    return levels


def _hgrn_consts(l, c, sub):
    r = np.arange(l)[:, None]
    s = np.arange(l)[None, :]
    lcum = ((r // c == s // c) & (s <= r)).astype(np.float32)
    rc = np.arange(c)[:, None]
    sc = np.arange(c)[None, :]
    masks = [((rc // (2 * g) == sc // (2 * g)) & (rc % (2 * g) >= g) & (sc % (2 * g) < g)).astype(np.float32)
             for g in _hgrn_levels(c, sub)]
    half = c // 2
    rg = np.arange(HGRN_GROUP)[:, None]
    sg = np.arange(HGRN_GROUP)[None, :]
    top = (rg // c == sg // c) & (rg % c >= half) & (sg % c < half)
    near = (rg // half == sg // half) & (sg <= rg)
    fmask = np.stack([top, near]).astype(np.float32)
    return jnp.asarray(lcum, BF16), jnp.asarray(np.stack(masks), F32), jnp.asarray(fmask, F32)


def _hgrn_body(q_ref, f_ref, i_ref, g_ref, lb_ref, ng_ref, lcum_ref, mask_ref, fmask_ref, o_ref,
               st_scr, stb_scr, a_scr, oi_scr, *, l, c, sub):
    nchunk = l // c
    levels = _hgrn_levels(c, sub)

    @pl.when(pl.program_id(2) == 0)
    def _():
        st_scr[...] = jnp.zeros(st_scr.shape, F32)

    q = q_ref[0].astype(F32)
    fr = f_ref[0].astype(F32)
    vb = i_ref[0]
    v = vb.astype(F32)
    lb = lb_ref[0]
    f = lb + (1.0 - lb) * jax.nn.sigmoid(fr)
    logf = jnp.log(jnp.maximum(f, jnp.finfo(F32).tiny))
    k = (1.0 - lb) * jax.nn.sigmoid(-fr)

    hi = logf.astype(BF16)
    lo = (logf - hi.astype(F32)).astype(BF16)
    lcum = lcum_ref[...]
    bcum = jnp.dot(lcum, hi, preferred_element_type=F32) + jnp.dot(lcum, lo, preferred_element_type=F32)

    def rows_of(arr, group, row):
        a3 = arr.reshape(l // group, group, HEAD)
        return jnp.broadcast_to(a3[:, row:row + 1, :], a3.shape).reshape(l, HEAD)

    nt = (((1,), (1,)), ((), ()))
    tn = (((0,), (0,)), ((), ()))

    def level_operands(g):
        ref = rows_of(bcum, 2 * g, g - 1)
        return ((q * jnp.exp(jnp.minimum(bcum - ref, 0.0))).astype(BF16),
                (k * jnp.exp(jnp.minimum(ref - bcum, 0.0))).astype(BF16))

    q_top, k_top = level_operands(levels[0])

    def top_scores(sl):
        return mask_ref[0] * lax.dot_general(q_top[sl], k_top[sl], nt, preferred_element_type=F32)

    half = c // 2
    b_start = rows_of(bcum, half, 0) - rows_of(logf, half, 0)
    span = b_start - bcum
    bounded = jnp.max(span) <= HGRN_SAFE_SPAN

    @pl.when(bounded)
    def _():
        q_f = (q * jnp.exp(-span)).astype(BF16)
        k_f = (k * jnp.exp(span)).astype(BF16)
        grp = fmask_ref.shape[-1]
        for gi in range(l // grp):
            sl = slice(gi * grp, (gi + 1) * grp)
            a = fmask_ref[0] * lax.dot_general(q_top[sl], k_top[sl], nt, preferred_element_type=F32)
            a = a + fmask_ref[1] * lax.dot_general(q_f[sl], k_f[sl], nt, preferred_element_type=F32)
            a_scr[sl, :] = jnp.dot(a.astype(BF16), vb[sl], preferred_element_type=F32)

    @pl.when(jnp.logical_not(bounded))
    def _():
        lower = [level_operands(g) for g in levels[1:]]
        nb = l // sub
        b3 = bcum.reshape(nb, sub, HEAD)
        q3 = q.reshape(nb, sub, HEAD)
        k3 = k.reshape(nb, sub, HEAD)
        v3 = v.reshape(nb, sub, HEAD)
        tloc = lax.broadcasted_iota(jnp.int32, (nb, sub, 1), 1)
        od = jnp.zeros((nb, sub, HEAD), F32)
        for si in range(sub):
            e = jnp.exp(b3 - b3[:, si:si + 1, :])
            pr = q3 * (k3[:, si:si + 1, :] * e)
            w = jnp.sum(pr, axis=-1, keepdims=True)
            w = jnp.where(tloc >= si, w, 0.0)
            od = od + w * v3[:, si:si + 1, :]
        od = od.reshape(l, HEAD)
        for ci in range(nchunk):
            sl = slice(ci * c, (ci + 1) * c)
            a = top_scores(sl)
            for li, (q_g, k_g) in enumerate(lower):
                a = a + mask_ref[li + 1] * lax.dot_general(q_g[sl], k_g[sl], nt, preferred_element_type=F32)
            a_scr[sl, :] = jnp.dot(a.astype(BF16), vb[sl], preferred_element_type=F32) + od[sl]

    b_end = rows_of(bcum, c, c - 1)
    q_in = (q * jnp.exp(bcum)).astype(BF16)
    k_out = (k * jnp.exp(b_end - bcum)).astype(BF16)
    dec = jnp.exp(b_end)
    kv = [lax.dot_general(vb[ci * c:(ci + 1) * c], k_out[ci * c:(ci + 1) * c], tn, preferred_element_type=F32)
          for ci in range(nchunk)]
    st = st_scr[...]
    for ci in range(nchunk):
        stb_scr[ci] = st.astype(BF16)
        st = st * dec[ci * c:ci * c + 1, :] + kv[ci]
    st_scr[...] = st
    for ci in range(nchunk):
        sl = slice(ci * c, (ci + 1) * c)
        oi_scr[sl, :] = lax.dot_general(q_in[sl], stb_scr[ci], nt, preferred_element_type=F32)

    o = a_scr[...] + oi_scr[...]
    ms = jnp.mean(o * o, axis=-1, keepdims=True)
    o = o * lax.rsqrt(ms + EPS) * ng_ref[0]
    gate = g_ref[0].astype(F32)
    o_ref[0] = (o * (gate * jax.nn.sigmoid(gate))).astype(BF16)


def _hgrn(proj, lower_bounds, norm_all, layer, l=HGRN_L, c=HGRN_C, sub=HGRN_SUB):
    b, s, _ = proj.shape
    lcum, masks, fmask = _hgrn_consts(l, c, sub)
    base = 3 * N_HEADS

    def col(kind):
        return pl.BlockSpec((1, l, HEAD), lambda bi, h, i: (bi, i, base + kind * N_HEADS + h))

    return pl.pallas_call(
        functools.partial(_hgrn_body, l=l, c=c, sub=sub),
        grid=(b, N_HEADS, s // l),
        in_specs=[col(0), col(1), col(2), col(3),
                  pl.BlockSpec((1, 1, HEAD), lambda bi, h, i: (layer, 0, h)),
                  pl.BlockSpec((1, 1, HEAD), lambda bi, h, i: (layer, 0, 0)),
                  pl.BlockSpec((l, l), lambda bi, h, i: (0, 0)),
                  pl.BlockSpec(masks.shape, lambda bi, h, i: (0, 0, 0)),
                  pl.BlockSpec(fmask.shape, lambda bi, h, i: (0, 0, 0))],
        out_specs=pl.BlockSpec((1, l, HEAD), lambda bi, h, i: (bi, i, h)),
        out_shape=jax.ShapeDtypeStruct((b, s, N_HEADS * HEAD), BF16),
        scratch_shapes=[pltpu.VMEM((HEAD, HEAD), F32),
                        pltpu.VMEM((l // c, HEAD, HEAD), BF16),
                        pltpu.VMEM((l, HEAD), F32),
                        pltpu.VMEM((l, HEAD), F32)],
        compiler_params=_cparams(3),
        name="hgrn2",
    )(proj, proj, proj, proj, lower_bounds, norm_all, lcum, masks, fmask)


def _first_lane(cond, lane):
    return jnp.min(jnp.where(cond, lane, LANES), axis=-1, keepdims=True)


def _outproj_body(a_ref, hg_ref, w_ref, x_ref, mod_ref, g_ref, wrh_ref, wrl_ref, br_ref, xo_ref, h_ref, *, d):
    da = a_ref.shape[-1]
    mixed = jnp.dot(a_ref[0], w_ref[0, 0:da, :], preferred_element_type=F32)
    mixed = mixed + jnp.dot(hg_ref[0], w_ref[0, da:, :], preferred_element_type=F32)
    x = x_ref[0] + mod_ref[0, 2:3, :] * mixed
    xo_ref[0] = x
    h = _modulated_norm(x, g_ref[0], mod_ref[0, 4:5, :], mod_ref[0, 3:4, :])
    h_ref[0, :, 0:d] = h

    h_hi = h.astype(BF16)
    h_lo = (h - h_hi.astype(F32)).astype(BF16)
    logits = (jnp.dot(h_hi, wrh_ref[...], preferred_element_type=F32)
              + jnp.dot(h_lo, wrh_ref[...], preferred_element_type=F32)
              + jnp.dot(h_hi, wrl_ref[...], preferred_element_type=F32)) + br_ref[...]
    lane = lax.broadcasted_iota(jnp.int32, logits.shape, 1)
    valid = lane < N_EXPERTS
    logits = jnp.where(valid, logits, NEG)
    mx = jnp.max(logits, axis=-1, keepdims=True)
    ex = jnp.where(valid, jnp.exp(logits - mx), 0.0)
    probs = ex / jnp.sum(ex, axis=-1, keepdims=True)

    best = sel = v1 = v2 = i1 = i2 = None
    for gi in range(N_GROUPS):
        ing = (lane >= gi * GROUP_SIZE) & (lane < (gi + 1) * GROUP_SIZE)
        pg = jnp.where(ing, probs, -1.0)
        m1 = jnp.max(pg, axis=-1, keepdims=True)
        a1 = _first_lane(pg == m1, lane)
        pg2 = jnp.where(lane == a1, -1.0, pg)
        m2 = jnp.max(pg2, axis=-1, keepdims=True)
        a2 = _first_lane(pg2 == m2, lane)
        score = m1 + m2
        if gi == 0:
            best, sel, v1, v2, i1, i2 = score, jnp.zeros_like(a1), m1, m2, a1, a2
        else:
            better = score > best
            best = jnp.where(better, score, best)
            sel = jnp.where(better, gi, sel)
            v1 = jnp.where(better, m1, v1)
            v2 = jnp.where(better, m2, v2)
            i1 = jnp.where(better, a1, i1)
            i2 = jnp.where(better, a2, i2)
    tot = v1 + v2
    w1 = v1 / tot
    w2 = v2 / tot
    loc1 = i1 - sel * GROUP_SIZE
    loc2 = i2 - sel * GROUP_SIZE
    ext = jnp.where(lane == loc1, w1, 0.0) + jnp.where(lane == loc2, w2, 0.0)
    ext = jnp.where(lane == GROUP_SIZE, sel.astype(F32), ext)
    h_ref[0, :, d:] = ext


def _outproj_router(attn_o, hgrn_o, w_out_all, x, mod, ffn_norm_all, wr_hi, wr_lo, br, layer, tm=512):
    b, s, d = x.shape
    da = attn_o.shape[-1]
    dmix = w_out_all.shape[1]
    tok = lambda bi, i: (bi, i, 0)
    return pl.pallas_call(
        functools.partial(_outproj_body, d=d),
        grid=(b, s // tm),
        in_specs=[pl.BlockSpec((1, tm, da), tok),
                  pl.BlockSpec((1, tm, dmix - da), tok),
                  pl.BlockSpec((1, dmix, d), lambda bi, i: (layer, 0, 0)),
                  pl.BlockSpec((1, tm, d), tok),
                  pl.BlockSpec((1, N_MOD, d), lambda bi, i: (bi, 0, 0)),
                  pl.BlockSpec((1, 1, d), lambda bi, i: (layer, 0, 0)),
                  pl.BlockSpec((d, LANES), lambda bi, i: (0, 0)),
                  pl.BlockSpec((d, LANES), lambda bi, i: (0, 0)),
                  pl.BlockSpec((1, LANES), lambda bi, i: (0, 0))],
        out_specs=[pl.BlockSpec((1, tm, d), tok),
                   pl.BlockSpec((1, tm, d + LANES), tok)],
        out_shape=[jax.ShapeDtypeStruct((b, s, d), F32),
                   jax.ShapeDtypeStruct((b, s, d + LANES), F32)],
        compiler_params=_cparams(2),
        name="outproj_router",
    )(attn_o, hgrn_o, w_out_all, x, mod, ffn_norm_all, wr_hi, wr_lo, br)


def _moe_body(tg_ref, nv_ref, nu_ref, tok_ref, h_hbm, wg_ref, wu_ref, wd_ref, y_hbm,
              xbuf, xbf, acc, gsem, ssem, *, tm, d):
    i = pl.program_id(0)
    e = pl.program_id(1)
    n_used = nu_ref[0]
    slot = lax.rem(i, 2)

    def gather_start(tile, sl):
        base = tile * tm

        def body(r, carry):
            pltpu.make_async_copy(h_hbm.at[pl.ds(tok_ref[base + r], 1)], xbuf.at[sl, pl.ds(r, 1)], gsem.at[sl]).start()
            return carry

        lax.fori_loop(0, tm, body, 0, unroll=MOE_ISSUE_UNROLL)

    def gather_wait(sl):
        pltpu.make_async_copy(h_hbm.at[pl.ds(0, tm)], xbuf.at[sl], gsem.at[sl]).wait()

    def scatter_start(tile, sl, n):
        base = tile * tm

        def body(r, carry):
            pltpu.make_async_copy(acc.at[sl, pl.ds(r, 1)], y_hbm.at[pl.ds(tok_ref[base + r], 1)], ssem.at[sl]).start()
            return carry

        @pl.when(n == tm)
        def _():
            lax.fori_loop(0, tm, body, 0, unroll=MOE_ISSUE_UNROLL)

        @pl.when(n < tm)
        def _():
            lax.fori_loop(0, n, body, 0)

    def scatter_wait(sl, n):
        @pl.when(n == tm)
        def _():
            pltpu.make_async_copy(acc.at[sl], y_hbm.at[pl.ds(0, tm)], ssem.at[sl]).wait()

        @pl.when(n < tm)
        def _():
            def body(r, carry):
                pltpu.make_async_copy(acc.at[sl, pl.ds(r, 1)], y_hbm.at[pl.ds(r, 1)], ssem.at[sl]).wait()
                return carry

            lax.fori_loop(0, n, body, 0)

    @pl.when(i < n_used)
    def _():
        @pl.when(e == 0)
        def _():
            @pl.when(i == 0)
            def _():
                gather_start(0, 0)

            gather_wait(slot)

            @pl.when(i + 1 < n_used)
            def _():
                gather_start(i + 1, 1 - slot)

            xbf[...] = xbuf[slot, :, 0:d].astype(BF16)

            @pl.when(i >= 2)
            def _():
                scatter_wait(slot, nv_ref[jnp.maximum(i - 2, 0)])

            acc[slot] = jnp.zeros((tm, d), F32)

        ext = xbuf[slot, :, d:d + LANES]
        gate = jnp.zeros((tm, 1), F32)
        for ei in range(GROUP_SIZE):
            gate = jnp.where(e == ei, ext[:, ei:ei + 1], gate)
        x = xbf[...]
        hg = jnp.dot(x, wg_ref[0, 0], preferred_element_type=F32)
        hu = jnp.dot(x, wu_ref[0, 0], preferred_element_type=F32)
        act = (hg * jax.nn.sigmoid(hg)) * hu * gate
        acc[slot] += jnp.dot(act.astype(BF16), wd_ref[0, 0], preferred_element_type=F32)

        @pl.when(e == GROUP_SIZE - 1)
        def _():
            scatter_start(i, slot, nv_ref[i])

            @pl.when(i == n_used - 1)
            def _():
                scatter_wait(slot, nv_ref[i])

                @pl.when(i >= 1)
                def _():
                    scatter_wait(1 - slot, nv_ref[jnp.maximum(i - 1, 0)])


def _route(hext, d, tm):
    b, s, _ = hext.shape
    t = b * s
    n_tiles = t // tm + N_GROUPS
    gid = hext[:, :, d + GROUP_SIZE].reshape(t).astype(jnp.int32)
    onehot = (gid[:, None] == jnp.arange(N_GROUPS, dtype=jnp.int32)[None, :]).astype(jnp.int32)
    cnt = jnp.sum(onehot, axis=0)
    rank = jnp.sum(onehot * (jnp.cumsum(onehot, axis=0) - onehot), axis=1)
    ntile_g = (cnt + tm - 1) // tm
    tile_end_g = jnp.cumsum(ntile_g)
    tile_start_g = tile_end_g - ntile_g
    pos = (tile_start_g * tm)[gid] + rank
    tok_of_row = jnp.zeros((n_tiles * tm,), jnp.int32).at[pos].set(jnp.arange(t, dtype=jnp.int32))
    tile = jnp.arange(n_tiles, dtype=jnp.int32)
    tile_gid = jnp.minimum(jnp.sum((tile[:, None] >= tile_end_g[None, :]).astype(jnp.int32), axis=1), N_GROUPS - 1)
    nvalid = jnp.clip(cnt[tile_gid] - (tile - tile_start_g[tile_gid]) * tm, 0, tm).astype(jnp.int32)
    n_used = tile_end_g[N_GROUPS - 1:].astype(jnp.int32)
    return tile_gid.astype(jnp.int32), nvalid, n_used, tok_of_row


def _moe(hext, wg_all, wu_all, wd_all, layer, tm=MOE_TM):
    b, s, dx = hext.shape
    d = dx - LANES
    t = b * s
    f = wg_all.shape[-1]
    tile_gid, nvalid, n_used, tok_of_row = _route(hext, d, tm)
    n_tiles = tile_gid.shape[0]

    def wmap(i, e, tg, nv, nu, tok):
        return (layer, tg[i] * GROUP_SIZE + e, 0, 0)

    return pl.pallas_call(
        functools.partial(_moe_body, tm=tm, d=d),
        grid_spec=pltpu.PrefetchScalarGridSpec(
            num_scalar_prefetch=4,
            grid=(n_tiles, GROUP_SIZE),
            in_specs=[pl.BlockSpec(memory_space=pl.ANY),
                      pl.BlockSpec((1, 1, d, f), wmap),
                      pl.BlockSpec((1, 1, d, f), wmap),
                      pl.BlockSpec((1, 1, f, d), wmap)],
            out_specs=pl.BlockSpec(memory_space=pl.ANY),
            scratch_shapes=[pltpu.VMEM((2, tm, dx), F32),
                            pltpu.VMEM((tm, d), BF16),
                            pltpu.VMEM((2, tm, d), F32),
                            pltpu.SemaphoreType.DMA((2,)),
                            pltpu.SemaphoreType.DMA((2,))],
        ),
        out_shape=jax.ShapeDtypeStruct((t, d), F32),
        compiler_params=_cparams(2),
        name="moe_experts",
    )(tile_gid, nvalid, n_used, tok_of_row, hext.reshape(t, dx), wg_all, wu_all, wd_all)


def _final_body(x_ref, y_ref, mod_ref, g_ref, o_ref):
    x = x_ref[0] + mod_ref[0, 5:6, :] * y_ref[...]
    ms = jnp.mean(x * x, axis=-1, keepdims=True)
    o_ref[0] = x * lax.rsqrt(ms + EPS) * g_ref[...]


def _final(x, y, mod, g, tm=512):
    b, s, d = x.shape
    nt = s // tm
    return pl.pallas_call(
        _final_body,
        grid=(b, nt),
        in_specs=[pl.BlockSpec((1, tm, d), lambda bi, i: (bi, i, 0)),
                  pl.BlockSpec((tm, d), lambda bi, i: (bi * nt + i, 0)),
                  pl.BlockSpec((1, N_MOD, d), lambda bi, i: (bi, 0, 0)),
                  pl.BlockSpec((1, d), lambda bi, i: (0, 0))],
        out_specs=pl.BlockSpec((1, tm, d), lambda bi, i: (bi, i, 0)),
        out_shape=jax.ShapeDtypeStruct(x.shape, F32),
        compiler_params=_cparams(2),
        name="final_norm",
    )(x, y, mod, g.reshape(1, d))


def kernel(x, c, w_in, w_out, attn_norm, ffn_norm, w_ada, b_ada, lambda_q1, lambda_k1, lambda_q2, lambda_k2,
           diff_subln, hgrn_lb, hgrn_norm, rel_bias, w_router, b_router, w_gate, w_up, w_down, final_norm):
    b, s, d = x.shape
    depth = w_in.shape[0]

    w_in_b = w_in.astype(BF16)
    w_out_b = w_out.astype(BF16)
    w_gate_b = w_gate.astype(BF16)
    w_up_b = w_up.astype(BF16)
    w_down_b = w_down.astype(BF16)
    wr = jnp.pad(w_router.astype(F32), ((0, 0), (0, LANES - N_EXPERTS)))
    wr_hi = wr.astype(BF16)
    wr_lo = (wr - wr_hi.astype(F32)).astype(BF16)
    br = jnp.pad(b_router.astype(F32), (0, LANES - N_EXPERTS)).reshape(1, LANES)
    lb_soft = jax.nn.softmax(hgrn_lb.astype(F32), axis=0)
    lower_bounds = jnp.maximum(jnp.cumsum(lb_soft, axis=0) - lb_soft[0:1], 0.0).reshape(depth, 1, -1)
    attn_norm3 = attn_norm.astype(F32).reshape(depth, 1, d)
    ffn_norm3 = ffn_norm.astype(F32).reshape(depth, 1, d)
    subln3 = diff_subln.astype(F32).reshape(depth, 1, HEAD)
    hnorm3 = hgrn_norm.astype(F32).reshape(depth, 1, HEAD)

    c8 = jnp.pad(c.astype(F32), ((0, 8 - b), (0, 0)))
    mods = _ada(c8, w_ada, b_ada)
    bias = _bias_tiles(rel_bias, ATTN_T)

    y = None
    mod_prev = None
    for layer in range(depth):
        mod = mods[layer, :b].reshape(b, N_MOD, d)
        lambda_init = 0.8 - 0.6 * math.exp(-0.3 * layer)
        lam = (jnp.exp(jnp.sum(lambda_q1[layer].astype(F32) * lambda_k1[layer].astype(F32)))
               - jnp.exp(jnp.sum(lambda_q2[layer].astype(F32) * lambda_k2[layer].astype(F32)))
               + lambda_init)
        scalars = jnp.stack([lam, jnp.asarray(1.0 - lambda_init, F32)]).astype(F32)

        x, proj = _inproj(x, y, mod_prev, mod, attn_norm3, w_in_b, layer)
        attn_o = _attention(proj, bias, scalars, subln3, layer)
        hgrn_o = _hgrn(proj, lower_bounds, hnorm3, layer)
        x, hext = _outproj_router(attn_o, hgrn_o, w_out_b, x, mod, ffn_norm3, wr_hi, wr_lo, br, layer)
        y = _moe(hext, w_gate_b, w_up_b, w_down_b, layer)
        mod_prev = mod
    return _final(x, y, mod_prev, final_norm.astype(F32))
```

```python
import functools
import math

import numpy as np
import jax
import jax.numpy as jnp
from jax import lax
from jax.experimental import pallas as pl
from jax.experimental.pallas import tpu as pltpu

F32 = jnp.float32
BF16 = jnp.bfloat16
EPS = 1e-6

LANES = 128
HEAD = 128
QK = 64
N_HEADS = 8
NUM_BUCKETS = 32
MAX_DISTANCE = 128
N_EXPERTS = 16
N_GROUPS = 4
GROUP_SIZE = N_EXPERTS // N_GROUPS
N_MOD = 6
NEG = -1e30

ATTN_T = 512
ATTN_CW = 512
ONES_ROWS = 16
LOG2E = math.log2(math.e)
HGRN_L = 512
HGRN_C = 64
HGRN_SUB = 8
HGRN_GROUP = 256
HGRN_SAFE_SPAN = 80.0
NORM_ROWS = 256
MOE_TM = 512
MOE_ISSUE_UNROLL = 32
VMEM_LIMIT = 56 * 1024 * 1024


def _cparams(n_axes):
    return pltpu.CompilerParams(dimension_semantics=("arbitrary",) * n_axes, vmem_limit_bytes=VMEM_LIMIT)


def _ada_body(c_ref, w_ref, b_ref, o_ref):
    c = c_ref[...]
    ca = (c * jax.nn.sigmoid(c)).astype(BF16)
    o_ref[0] = jnp.dot(ca, w_ref[0].astype(BF16), preferred_element_type=F32) + b_ref[0]


def _ada(c8, w_ada, b_ada, tn=1024):
    depth, d, n = w_ada.shape
    return pl.pallas_call(
        _ada_body,
        grid=(depth, n // tn),
        in_specs=[pl.BlockSpec((8, d), lambda l, j: (0, 0)),
                  pl.BlockSpec((1, d, tn), lambda l, j: (l, 0, j)),
                  pl.BlockSpec((1, 1, tn), lambda l, j: (l, 0, j))],
        out_specs=pl.BlockSpec((1, 8, tn), lambda l, j: (l, 0, j)),
        out_shape=jax.ShapeDtypeStruct((depth, 8, n), F32),
        compiler_params=_cparams(2),
        name="ada_mod",
    )(c8, w_ada, b_ada.reshape(depth, 1, n))


def _modulated_norm(x, g, scale, shift):
    ms = jnp.mean(x * x, axis=-1, keepdims=True)
    return (x * lax.rsqrt(ms + EPS) * g) * (1.0 + scale) + shift


def _inproj_body(*refs, has_res):
    if has_res:
        x_ref, y_ref, gp_ref, mod_ref, g_ref, w_ref, xo_ref, p_ref, h_scr = refs
    else:
        x_ref, mod_ref, g_ref, w_ref, p_ref, h_scr = refs

    @pl.when(pl.program_id(2) == 0)
    def _():
        tm = h_scr.shape[0]
        for r0 in range(0, tm, NORM_ROWS):
            rows = slice(r0, r0 + NORM_ROWS)
            x = x_ref[0, rows, :]
            if has_res:
                x = x + gp_ref[0, 5:6, :] * y_ref[rows, :]
                xo_ref[0, rows, :] = x
            h = _modulated_norm(x, g_ref[0], mod_ref[0, 1:2, :], mod_ref[0, 0:1, :])
            h_scr[rows, :] = h.astype(BF16)

    p_ref[0] = jnp.dot(h_scr[...], w_ref[0], preferred_element_type=F32).astype(BF16)


def _inproj(x, y, mod_prev, mod, g_all, w_all, layer, tm=1024, tn=1024):
    b, s, d = x.shape
    n = w_all.shape[-1]
    nt = s // tm
    has_res = y is not None
    xspec = pl.BlockSpec((1, tm, d), lambda bi, i, j: (bi, i, 0))
    modspec = pl.BlockSpec((1, N_MOD, d), lambda bi, i, j: (bi, 0, 0))
    once = dict(pipeline_mode=pl.Buffered(1))
    in_specs = [pl.BlockSpec((1, tm, d), lambda bi, i, j: (bi, i, 0), **once)]
    args = [x]
    if has_res:
        in_specs += [pl.BlockSpec((tm, d), lambda bi, i, j: (bi * nt + i, 0), **once), modspec]
        args += [y, mod_prev]
    in_specs += [modspec,
                 pl.BlockSpec((1, 1, d), lambda bi, i, j: (layer, 0, 0)),
                 pl.BlockSpec((1, d, tn), lambda bi, i, j: (layer, 0, j))]
    args += [mod, g_all, w_all]
    pspec = pl.BlockSpec((1, tm, tn), lambda bi, i, j: (bi, i, j))
    pshape = jax.ShapeDtypeStruct((b, s, n), BF16)
    if has_res:
        out_specs, out_shape = [xspec, pspec], [jax.ShapeDtypeStruct(x.shape, F32), pshape]
    else:
        out_specs, out_shape = pspec, pshape
    out = pl.pallas_call(
        functools.partial(_inproj_body, has_res=has_res),
        grid=(b, nt, n // tn),
        in_specs=in_specs, out_specs=out_specs, out_shape=out_shape,
        scratch_shapes=[pltpu.VMEM((tm, d), BF16)],
        compiler_params=_cparams(3),
        name="inproj",
    )(*args)
    return (out[0], out[1]) if has_res else (x, out)


def _t5_bucket_np(n):
    max_exact = NUM_BUCKETS // 2
    nf = np.maximum(n, 1).astype(np.float32)
    large = max_exact + (np.log(nf / np.float32(max_exact)) / np.float32(math.log(MAX_DISTANCE / max_exact))
                         * np.float32(NUM_BUCKETS - max_exact)).astype(np.int32)
    large = np.minimum(large, NUM_BUCKETS - 1)
    return np.where(n < max_exact, n, large).astype(np.int32)


def _bias_body(tab_ref, bd_ref, bu_ref, o_ref):
    m = pl.program_id(0) * 2 + pl.program_id(1)
    bd = bd_ref[...]
    bu = bu_ref[...]
    far = tab_ref[NUM_BUCKETS - 1, m]
    accd = jnp.zeros(bd.shape, F32)
    accu = jnp.zeros(bu.shape, F32)
    for bkt in range(NUM_BUCKETS - 1):
        val = (tab_ref[bkt, m] - far) * LOG2E
        accd = jnp.where(bd == bkt, val, accd)
        accu = jnp.where(bu == bkt, val, accu)
    o_ref[0, 0] = jnp.where(bd < 0, NEG, accd)
    o_ref[0, 1] = accu


def _bias_tiles(rel_bias, t):
    key = np.arange(t)[:, None]
    qry = np.arange(t)[None, :]
    bd = np.where(qry >= key, _t5_bucket_np(np.maximum(qry - key, 0)), -1).astype(np.int32)
    bu = _t5_bucket_np(t + qry - key)
    assert MAX_DISTANCE <= t + 1, "tiles two or more to the left must lie in the far bucket"
    return pl.pallas_call(
        _bias_body,
        grid=(N_HEADS, 2),
        in_specs=[pl.BlockSpec(memory_space=pltpu.SMEM),
                  pl.BlockSpec((t, t), lambda h, m: (0, 0)),
                  pl.BlockSpec((t, t), lambda h, m: (0, 0))],
        out_specs=pl.BlockSpec((1, 2, t, t), lambda h, m: (h, 0, 0, m)),
        out_shape=jax.ShapeDtypeStruct((N_HEADS, 2, t, 2 * t), F32),
        compiler_params=_cparams(2),
        name="bias_tiles",
    )(rel_bias.astype(F32), jnp.asarray(bd), jnp.asarray(bu))


def _attn_body(sc_ref, q_ref, k_ref, v_ref, bias_ref, g_ref, o_ref, qq_scr, vt_scr, m_scr, acc_scr, s0_scr, s1_scr,
               *, t, cw):
    qi = pl.program_id(2)
    s_len = k_ref.shape[1]

    @pl.when(qi == 0)
    def _():
        for ci in range(s_len // t):
            vt_scr[0:HEAD, ci * t:(ci + 1) * t] = v_ref[0, ci * t:(ci + 1) * t, :].T
        row = lax.broadcasted_iota(jnp.int32, (ONES_ROWS, s_len), 0)
        vt_scr[HEAD:HEAD + ONES_ROWS, :] = jnp.where(row == 0, 1.0, 0.0).astype(BF16)

    q = (q_ref[0].astype(F32) * (QK ** -0.5 * LOG2E)).astype(BF16)
    lane = lax.broadcasted_iota(jnp.int32, q.shape, 1)
    zero = jnp.zeros_like(q)
    qq_scr[0:t, :] = jnp.where(lane < QK, q, zero)
    qq_scr[t:2 * t, :] = jnp.where(lane >= QK, q, zero)
    m_scr[...] = jnp.full(m_scr.shape, NEG, F32)
    acc_scr[...] = jnp.zeros(acc_scr.shape, F32)

    tiles = [slice(ct * cw, (ct + 1) * cw) for ct in range(2 * t // cw)]

    bufs = (s0_scr, s1_scr)

    def scores_into(j, buf):
        kb = k_ref[0, pl.ds(pl.multiple_of(j * t, t), t), :]
        for cs in tiles:
            buf[:, cs] = lax.dot_general(kb, qq_scr[cs, :], (((1,), (1,)), ((), ())), preferred_element_type=F32)

    def consume(j, buf, bias_of):
        vt = vt_scr[:, pl.ds(pl.multiple_of(j * t, t), t)]
        for cs in tiles:
            s = buf[:, cs]
            if bias_of is not None:
                s = s + bias_of(cs)
            m_prev = m_scr[:, cs]
            m_new = jnp.maximum(m_prev, jnp.max(s, axis=0, keepdims=True))
            alpha = jnp.exp2(m_prev - m_new)
            p = jnp.exp2(s - m_new)
            acc_scr[:, cs] = alpha * acc_scr[:, cs] + jnp.dot(vt, p.astype(BF16), preferred_element_type=F32)
            m_scr[:, cs] = m_new

    near = lambda cs: bias_ref[0, 1, :, cs]
    diag = lambda cs: bias_ref[0, 0, :, cs]

    scores_into(0, bufs[0])
    n_far = jnp.maximum(qi - 1, 0)

    def far_pair(pi, carry):
        j = 2 * pi
        scores_into(j + 1, bufs[1])
        consume(j, bufs[0], None)
        scores_into(j + 2, bufs[0])
        consume(j + 1, bufs[1], None)
        return carry

    lax.fori_loop(0, n_far // 2, far_pair, 0)

    @pl.when(qi == 0)
    def _():
        consume(0, bufs[0], diag)

    @pl.when(jnp.logical_and(qi >= 1, lax.rem(qi, 2) == 1))
    def _():
        scores_into(qi, bufs[1])
        consume(qi - 1, bufs[0], near)
        consume(qi, bufs[1], diag)

    @pl.when(jnp.logical_and(qi >= 2, lax.rem(qi, 2) == 0))
    def _():
        scores_into(qi - 1, bufs[1])
        consume(qi - 2, bufs[0], None)
        scores_into(qi, bufs[0])
        consume(qi - 1, bufs[1], near)
        consume(qi, bufs[0], diag)

    o_maps = acc_scr[0:HEAD, :] / acc_scr[HEAD:HEAD + 1, :]
    o = o_maps[:, 0:t] - sc_ref[0] * o_maps[:, t:2 * t]
    ms = jnp.mean(o * o, axis=0, keepdims=True)
    o = (o * lax.rsqrt(ms + EPS)).T
    o_ref[0] = ((o * g_ref[0]) * sc_ref[1]).astype(BF16)


def _attention(proj, bias, scalars, subln_all, layer, t=ATTN_T):
    b, s, _ = proj.shape
    nq = s // t
    return pl.pallas_call(
        functools.partial(_attn_body, t=t, cw=ATTN_CW),
        grid=(b, N_HEADS, nq),
        in_specs=[pl.BlockSpec(memory_space=pltpu.SMEM),
                  pl.BlockSpec((1, t, HEAD), lambda bi, h, i: (bi, i, h)),
                  pl.BlockSpec((1, s, HEAD), lambda bi, h, i: (bi, 0, N_HEADS + h)),
                  pl.BlockSpec((1, s, HEAD), lambda bi, h, i: (bi, 0, 2 * N_HEADS + h)),
                  pl.BlockSpec((1, 2, t, 2 * t), lambda bi, h, i: (h, 0, 0, 0)),
                  pl.BlockSpec((1, 1, HEAD), lambda bi, h, i: (layer, 0, 0))],
        out_specs=pl.BlockSpec((1, t, HEAD), lambda bi, h, i: (bi, i, h)),
        out_shape=jax.ShapeDtypeStruct((b, s, N_HEADS * HEAD), BF16),
        scratch_shapes=[pltpu.VMEM((2 * t, HEAD), BF16),
                        pltpu.VMEM((HEAD + ONES_ROWS, s), BF16),
                        pltpu.VMEM((1, 2 * t), F32),
                        pltpu.VMEM((HEAD + ONES_ROWS, 2 * t), F32),
                        pltpu.VMEM((t, 2 * t), F32),
                        pltpu.VMEM((t, 2 * t), F32)],
        compiler_params=_cparams(3),
        name="diff_attention",
    )(scalars, proj, proj, proj, bias, subln_all)


def _hgrn_levels(c, sub):
    levels = []
    g = c // 2
    while g >= sub:
        levels.append(g)
        g //= 2
    return levels


def _hgrn_consts(l, c, sub):
    r = np.arange(l)[:, None]
    s = np.arange(l)[None, :]
    lcum = ((r // c == s // c) & (s <= r)).astype(np.float32)
    rc = np.arange(c)[:, None]
    sc = np.arange(c)[None, :]
    masks = [((rc // (2 * g) == sc // (2 * g)) & (rc % (2 * g) >= g) & (sc % (2 * g) < g)).astype(np.float32)
             for g in _hgrn_levels(c, sub)]
    half = c // 2
    rg = np.arange(HGRN_GROUP)[:, None]
    sg = np.arange(HGRN_GROUP)[None, :]
    top = (rg // c == sg // c) & (rg % c >= half) & (sg % c < half)
    near = (rg // half == sg // half) & (sg <= rg)
    fmask = np.stack([top, near]).astype(np.float32)
    return jnp.asarray(lcum, BF16), jnp.asarray(np.stack(masks), F32), jnp.asarray(fmask, F32)


def _hgrn_body(q_ref, f_ref, i_ref, g_ref, lb_ref, ng_ref, lcum_ref, mask_ref, fmask_ref, o_ref,
               st_scr, stb_scr, a_scr, oi_scr, *, l, c, sub):
    nchunk = l // c
    levels = _hgrn_levels(c, sub)

    @pl.when(pl.program_id(2) == 0)
    def _():
        st_scr[...] = jnp.zeros(st_scr.shape, F32)

    q = q_ref[0].astype(F32)
    fr = f_ref[0].astype(F32)
    vb = i_ref[0]
    v = vb.astype(F32)
    lb = lb_ref[0]
    f = lb + (1.0 - lb) * jax.nn.sigmoid(fr)
    logf = jnp.log(jnp.maximum(f, jnp.finfo(F32).tiny))
    k = (1.0 - lb) * jax.nn.sigmoid(-fr)

    hi = logf.astype(BF16)
    lo = (logf - hi.astype(F32)).astype(BF16)
    lcum = lcum_ref[...]
    bcum = jnp.dot(lcum, hi, preferred_element_type=F32) + jnp.dot(lcum, lo, preferred_element_type=F32)

    def rows_of(arr, group, row):
        a3 = arr.reshape(l // group, group, HEAD)
        return jnp.broadcast_to(a3[:, row:row + 1, :], a3.shape).reshape(l, HEAD)

    nt = (((1,), (1,)), ((), ()))
    tn = (((0,), (0,)), ((), ()))

    def level_operands(g):
        ref = rows_of(bcum, 2 * g, g - 1)
        return ((q * jnp.exp(jnp.minimum(bcum - ref, 0.0))).astype(BF16),
                (k * jnp.exp(jnp.minimum(ref - bcum, 0.0))).astype(BF16))

    q_top, k_top = level_operands(levels[0])

    def top_scores(sl):
        return mask_ref[0] * lax.dot_general(q_top[sl], k_top[sl], nt, preferred_element_type=F32)

    half = c // 2
    b_start = rows_of(bcum, half, 0) - rows_of(logf, half, 0)
    span = b_start - bcum
    bounded = jnp.max(span) <= HGRN_SAFE_SPAN

    @pl.when(bounded)
    def _():
        q_f = (q * jnp.exp(-span)).astype(BF16)
        k_f = (k * jnp.exp(span)).astype(BF16)
        grp = fmask_ref.shape[-1]
        for gi in range(l // grp):
            sl = slice(gi * grp, (gi + 1) * grp)
            a = fmask_ref[0] * lax.dot_general(q_top[sl], k_top[sl], nt, preferred_element_type=F32)
            a = a + fmask_ref[1] * lax.dot_general(q_f[sl], k_f[sl], nt, preferred_element_type=F32)
            a_scr[sl, :] = jnp.dot(a.astype(BF16), vb[sl], preferred_element_type=F32)

    @pl.when(jnp.logical_not(bounded))
    def _():
        lower = [level_operands(g) for g in levels[1:]]
        nb = l // sub
        b3 = bcum.reshape(nb, sub, HEAD)
        q3 = q.reshape(nb, sub, HEAD)
        k3 = k.reshape(nb, sub, HEAD)
        v3 = v.reshape(nb, sub, HEAD)
        tloc = lax.broadcasted_iota(jnp.int32, (nb, sub, 1), 1)
        od = jnp.zeros((nb, sub, HEAD), F32)
        for si in range(sub):
            e = jnp.exp(b3 - b3[:, si:si + 1, :])
            pr = q3 * (k3[:, si:si + 1, :] * e)
            w = jnp.sum(pr, axis=-1, keepdims=True)
            w = jnp.where(tloc >= si, w, 0.0)
            od = od + w * v3[:, si:si + 1, :]
        od = od.reshape(l, HEAD)
        for ci in range(nchunk):
            sl = slice(ci * c, (ci + 1) * c)
            a = top_scores(sl)
            for li, (q_g, k_g) in enumerate(lower):
                a = a + mask_ref[li + 1] * lax.dot_general(q_g[sl], k_g[sl], nt, preferred_element_type=F32)
            a_scr[sl, :] = jnp.dot(a.astype(BF16), vb[sl], preferred_element_type=F32) + od[sl]

    b_end = rows_of(bcum, c, c - 1)
    q_in = (q * jnp.exp(bcum)).astype(BF16)
    k_out = (k * jnp.exp(b_end - bcum)).astype(BF16)
    dec = jnp.exp(b_end)
    kv = [lax.dot_general(vb[ci * c:(ci + 1) * c], k_out[ci * c:(ci + 1) * c], tn, preferred_element_type=F32)
          for ci in range(nchunk)]
    st = st_scr[...]
    for ci in range(nchunk):
        stb_scr[ci] = st.astype(BF16)
        st = st * dec[ci * c:ci * c + 1, :] + kv[ci]
    st_scr[...] = st
    for ci in range(nchunk):
        sl = slice(ci * c, (ci + 1) * c)
        oi_scr[sl, :] = lax.dot_general(q_in[sl], stb_scr[ci], nt, preferred_element_type=F32)

    o = a_scr[...] + oi_scr[...]
    ms = jnp.mean(o * o, axis=-1, keepdims=True)
    o = o * lax.rsqrt(ms + EPS) * ng_ref[0]
    gate = g_ref[0].astype(F32)
    o_ref[0] = (o * (gate * jax.nn.sigmoid(gate))).astype(BF16)


def _hgrn(proj, lower_bounds, norm_all, layer, l=HGRN_L, c=HGRN_C, sub=HGRN_SUB):
    b, s, _ = proj.shape
    lcum, masks, fmask = _hgrn_consts(l, c, sub)
    base = 3 * N_HEADS

    def col(kind):
        return pl.BlockSpec((1, l, HEAD), lambda bi, h, i: (bi, i, base + kind * N_HEADS + h))

    return pl.pallas_call(
        functools.partial(_hgrn_body, l=l, c=c, sub=sub),
        grid=(b, N_HEADS, s // l),
        in_specs=[col(0), col(1), col(2), col(3),
                  pl.BlockSpec((1, 1, HEAD), lambda bi, h, i: (layer, 0, h)),
                  pl.BlockSpec((1, 1, HEAD), lambda bi, h, i: (layer, 0, 0)),
                  pl.BlockSpec((l, l), lambda bi, h, i: (0, 0)),
                  pl.BlockSpec(masks.shape, lambda bi, h, i: (0, 0, 0)),
                  pl.BlockSpec(fmask.shape, lambda bi, h, i: (0, 0, 0))],
        out_specs=pl.BlockSpec((1, l, HEAD), lambda bi, h, i: (bi, i, h)),
        out_shape=jax.ShapeDtypeStruct((b, s, N_HEADS * HEAD), BF16),
        scratch_shapes=[pltpu.VMEM((HEAD, HEAD), F32),
                        pltpu.VMEM((l // c, HEAD, HEAD), BF16),
                        pltpu.VMEM((l, HEAD), F32),
                        pltpu.VMEM((l, HEAD), F32)],
        compiler_params=_cparams(3),
        name="hgrn2",
    )(proj, proj, proj, proj, lower_bounds, norm_all, lcum, masks, fmask)


def _first_lane(cond, lane):
    return jnp.min(jnp.where(cond, lane, LANES), axis=-1, keepdims=True)


def _outproj_body(a_ref, hg_ref, w_ref, x_ref, mod_ref, g_ref, wrh_ref, wrl_ref, br_ref, xo_ref, h_ref, *, d):
    da = a_ref.shape[-1]
    mixed = jnp.dot(a_ref[0], w_ref[0, 0:da, :], preferred_element_type=F32)
    mixed = mixed + jnp.dot(hg_ref[0], w_ref[0, da:, :], preferred_element_type=F32)
    x = x_ref[0] + mod_ref[0, 2:3, :] * mixed
    xo_ref[0] = x
    h = _modulated_norm(x, g_ref[0], mod_ref[0, 4:5, :], mod_ref[0, 3:4, :])
    h_ref[0, :, 0:d] = h

    h_hi = h.astype(BF16)
    h_lo = (h - h_hi.astype(F32)).astype(BF16)
    logits = (jnp.dot(h_hi, wrh_ref[...], preferred_element_type=F32)
              + jnp.dot(h_lo, wrh_ref[...], preferred_element_type=F32)
              + jnp.dot(h_hi, wrl_ref[...], preferred_element_type=F32)) + br_ref[...]
    lane = lax.broadcasted_iota(jnp.int32, logits.shape, 1)
    valid = lane < N_EXPERTS
    logits = jnp.where(valid, logits, NEG)
    mx = jnp.max(logits, axis=-1, keepdims=True)
    ex = jnp.where(valid, jnp.exp(logits - mx), 0.0)
    probs = ex / jnp.sum(ex, axis=-1, keepdims=True)

    best = sel = v1 = v2 = i1 = i2 = None
    for gi in range(N_GROUPS):
        ing = (lane >= gi * GROUP_SIZE) & (lane < (gi + 1) * GROUP_SIZE)
        pg = jnp.where(ing, probs, -1.0)
        m1 = jnp.max(pg, axis=-1, keepdims=True)
        a1 = _first_lane(pg == m1, lane)
        pg2 = jnp.where(lane == a1, -1.0, pg)
        m2 = jnp.max(pg2, axis=-1, keepdims=True)
        a2 = _first_lane(pg2 == m2, lane)
        score = m1 + m2
        if gi == 0:
            best, sel, v1, v2, i1, i2 = score, jnp.zeros_like(a1), m1, m2, a1, a2
        else:
            better = score > best
            best = jnp.where(better, score, best)
            sel = jnp.where(better, gi, sel)
            v1 = jnp.where(better, m1, v1)
            v2 = jnp.where(better, m2, v2)
            i1 = jnp.where(better, a1, i1)
            i2 = jnp.where(better, a2, i2)
    tot = v1 + v2
    w1 = v1 / tot
    w2 = v2 / tot
    loc1 = i1 - sel * GROUP_SIZE
    loc2 = i2 - sel * GROUP_SIZE
    ext = jnp.where(lane == loc1, w1, 0.0) + jnp.where(lane == loc2, w2, 0.0)
    ext = jnp.where(lane == GROUP_SIZE, sel.astype(F32), ext)
    h_ref[0, :, d:] = ext


def _outproj_router(attn_o, hgrn_o, w_out_all, x, mod, ffn_norm_all, wr_hi, wr_lo, br, layer, tm=512):
    b, s, d = x.shape
    da = attn_o.shape[-1]
    dmix = w_out_all.shape[1]
    tok = lambda bi, i: (bi, i, 0)
    return pl.pallas_call(
        functools.partial(_outproj_body, d=d),
        grid=(b, s // tm),
        in_specs=[pl.BlockSpec((1, tm, da), tok),
                  pl.BlockSpec((1, tm, dmix - da), tok),
                  pl.BlockSpec((1, dmix, d), lambda bi, i: (layer, 0, 0)),
                  pl.BlockSpec((1, tm, d), tok),
                  pl.BlockSpec((1, N_MOD, d), lambda bi, i: (bi, 0, 0)),
                  pl.BlockSpec((1, 1, d), lambda bi, i: (layer, 0, 0)),
                  pl.BlockSpec((d, LANES), lambda bi, i: (0, 0)),
                  pl.BlockSpec((d, LANES), lambda bi, i: (0, 0)),
                  pl.BlockSpec((1, LANES), lambda bi, i: (0, 0))],
        out_specs=[pl.BlockSpec((1, tm, d), tok),
                   pl.BlockSpec((1, tm, d + LANES), tok)],
        out_shape=[jax.ShapeDtypeStruct((b, s, d), F32),
                   jax.ShapeDtypeStruct((b, s, d + LANES), F32)],
        compiler_params=_cparams(2),
        name="outproj_router",
    )(attn_o, hgrn_o, w_out_all, x, mod, ffn_norm_all, wr_hi, wr_lo, br)


def _moe_body(tg_ref, nv_ref, nu_ref, tok_ref, h_hbm, wg_ref, wu_ref, wd_ref, y_hbm,
              xbuf, xbf, acc, gsem, ssem, *, tm, d):
    i = pl.program_id(0)
    e = pl.program_id(1)
    n_used = nu_ref[0]
    slot = lax.rem(i, 2)

    def gather_start(tile, sl):
        base = tile * tm

        def body(r, carry):
            pltpu.make_async_copy(h_hbm.at[pl.ds(tok_ref[base + r], 1)], xbuf.at[sl, pl.ds(r, 1)], gsem.at[sl]).start()
            return carry

        lax.fori_loop(0, tm, body, 0, unroll=MOE_ISSUE_UNROLL)

    def gather_wait(sl):
        pltpu.make_async_copy(h_hbm.at[pl.ds(0, tm)], xbuf.at[sl], gsem.at[sl]).wait()

    def scatter_start(tile, sl, n):
        base = tile * tm

        def body(r, carry):
            pltpu.make_async_copy(acc.at[sl, pl.ds(r, 1)], y_hbm.at[pl.ds(tok_ref[base + r], 1)], ssem.at[sl]).start()
            return carry

        @pl.when(n == tm)
        def _():
            lax.fori_loop(0, tm, body, 0, unroll=MOE_ISSUE_UNROLL)

        @pl.when(n < tm)
        def _():
            lax.fori_loop(0, n, body, 0)

    def scatter_wait(sl, n):
        @pl.when(n == tm)
        def _():
            pltpu.make_async_copy(acc.at[sl], y_hbm.at[pl.ds(0, tm)], ssem.at[sl]).wait()

        @pl.when(n < tm)
        def _():
            def body(r, carry):
                pltpu.make_async_copy(acc.at[sl, pl.ds(r, 1)], y_hbm.at[pl.ds(r, 1)], ssem.at[sl]).wait()
                return carry

            lax.fori_loop(0, n, body, 0)

    @pl.when(i < n_used)
    def _():
        @pl.when(e == 0)
        def _():
            @pl.when(i == 0)
            def _():
                gather_start(0, 0)

            gather_wait(slot)

            @pl.when(i + 1 < n_used)
            def _():
                gather_start(i + 1, 1 - slot)

            xbf[...] = xbuf[slot, :, 0:d].astype(BF16)

            @pl.when(i >= 2)
            def _():
                scatter_wait(slot, nv_ref[jnp.maximum(i - 2, 0)])

            acc[slot] = jnp.zeros((tm, d), F32)

        ext = xbuf[slot, :, d:d + LANES]
        gate = jnp.zeros((tm, 1), F32)
        for ei in range(GROUP_SIZE):
            gate = jnp.where(e == ei, ext[:, ei:ei + 1], gate)
        x = xbf[...]
        hg = jnp.dot(x, wg_ref[0, 0], preferred_element_type=F32)
        hu = jnp.dot(x, wu_ref[0, 0], preferred_element_type=F32)
        act = (hg * jax.nn.sigmoid(hg)) * hu * gate
        acc[slot] += jnp.dot(act.astype(BF16), wd_ref[0, 0], preferred_element_type=F32)

        @pl.when(e == GROUP_SIZE - 1)
        def _():
            scatter_start(i, slot, nv_ref[i])

            @pl.when(i == n_used - 1)
            def _():
                scatter_wait(slot, nv_ref[i])

                @pl.when(i >= 1)
                def _():
                    scatter_wait(1 - slot, nv_ref[jnp.maximum(i - 1, 0)])


def _route(hext, d, tm):
    b, s, _ = hext.shape
    t = b * s
    n_tiles = t // tm + N_GROUPS
    gid = hext[:, :, d + GROUP_SIZE].reshape(t).astype(jnp.int32)
    onehot = (gid[:, None] == jnp.arange(N_GROUPS, dtype=jnp.int32)[None, :]).astype(jnp.int32)
    cnt = jnp.sum(onehot, axis=0)
    rank = jnp.sum(onehot * (jnp.cumsum(onehot, axis=0) - onehot), axis=1)
    ntile_g = (cnt + tm - 1) // tm
    tile_end_g = jnp.cumsum(ntile_g)
    tile_start_g = tile_end_g - ntile_g
    pos = (tile_start_g * tm)[gid] + rank
    tok_of_row = jnp.zeros((n_tiles * tm,), jnp.int32).at[pos].set(jnp.arange(t, dtype=jnp.int32))
    tile = jnp.arange(n_tiles, dtype=jnp.int32)
    tile_gid = jnp.minimum(jnp.sum((tile[:, None] >= tile_end_g[None, :]).astype(jnp.int32), axis=1), N_GROUPS - 1)
    nvalid = jnp.clip(cnt[tile_gid] - (tile - tile_start_g[tile_gid]) * tm, 0, tm).astype(jnp.int32)
    n_used = tile_end_g[N_GROUPS - 1:].astype(jnp.int32)
    return tile_gid.astype(jnp.int32), nvalid, n_used, tok_of_row


def _moe(hext, wg_all, wu_all, wd_all, layer, tm=MOE_TM):
    b, s, dx = hext.shape
    d = dx - LANES
    t = b * s
    f = wg_all.shape[-1]
    tile_gid, nvalid, n_used, tok_of_row = _route(hext, d, tm)
    n_tiles = tile_gid.shape[0]

    def wmap(i, e, tg, nv, nu, tok):
        return (layer, tg[i] * GROUP_SIZE + e, 0, 0)

    return pl.pallas_call(
        functools.partial(_moe_body, tm=tm, d=d),
        grid_spec=pltpu.PrefetchScalarGridSpec(
            num_scalar_prefetch=4,
            grid=(n_tiles, GROUP_SIZE),
            in_specs=[pl.BlockSpec(memory_space=pl.ANY),
                      pl.BlockSpec((1, 1, d, f), wmap),
                      pl.BlockSpec((1, 1, d, f), wmap),
                      pl.BlockSpec((1, 1, f, d), wmap)],
            out_specs=pl.BlockSpec(memory_space=pl.ANY),
            scratch_shapes=[pltpu.VMEM((2, tm, dx), F32),
                            pltpu.VMEM((tm, d), BF16),
                            pltpu.VMEM((2, tm, d), F32),
                            pltpu.SemaphoreType.DMA((2,)),
                            pltpu.SemaphoreType.DMA((2,))],
        ),
        out_shape=jax.ShapeDtypeStruct((t, d), F32),
        compiler_params=_cparams(2),
        name="moe_experts",
    )(tile_gid, nvalid, n_used, tok_of_row, hext.reshape(t, dx), wg_all, wu_all, wd_all)


def _final_body(x_ref, y_ref, mod_ref, g_ref, o_ref):
    x = x_ref[0] + mod_ref[0, 5:6, :] * y_ref[...]
    ms = jnp.mean(x * x, axis=-1, keepdims=True)
    o_ref[0] = x * lax.rsqrt(ms + EPS) * g_ref[...]


def _final(x, y, mod, g, tm=512):
    b, s, d = x.shape
    nt = s // tm
    return pl.pallas_call(
        _final_body,
        grid=(b, nt),
        in_specs=[pl.BlockSpec((1, tm, d), lambda bi, i: (bi, i, 0)),
                  pl.BlockSpec((tm, d), lambda bi, i: (bi * nt + i, 0)),
                  pl.BlockSpec((1, N_MOD, d), lambda bi, i: (bi, 0, 0)),
                  pl.BlockSpec((1, d), lambda bi, i: (0, 0))],
        out_specs=pl.BlockSpec((1, tm, d), lambda bi, i: (bi, i, 0)),
        out_shape=jax.ShapeDtypeStruct(x.shape, F32),
        compiler_params=_cparams(2),
        name="final_norm",
    )(x, y, mod, g.reshape(1, d))


def kernel(x, c, w_in, w_out, attn_norm, ffn_norm, w_ada, b_ada, lambda_q1, lambda_k1, lambda_q2, lambda_k2,
           diff_subln, hgrn_lb, hgrn_norm, rel_bias, w_router, b_router, w_gate, w_up, w_down, final_norm):
    b, s, d = x.shape
    depth = w_in.shape[0]

    w_in_b = w_in.astype(BF16)
    w_out_b = w_out.astype(BF16)
    w_gate_b = w_gate.astype(BF16)
    w_up_b = w_up.astype(BF16)
    w_down_b = w_down.astype(BF16)
    wr = jnp.pad(w_router.astype(F32), ((0, 0), (0, LANES - N_EXPERTS)))
    wr_hi = wr.astype(BF16)
    wr_lo = (wr - wr_hi.astype(F32)).astype(BF16)
    br = jnp.pad(b_router.astype(F32), (0, LANES - N_EXPERTS)).reshape(1, LANES)
    lb_soft = jax.nn.softmax(hgrn_lb.astype(F32), axis=0)
    lower_bounds = jnp.maximum(jnp.cumsum(lb_soft, axis=0) - lb_soft[0:1], 0.0).reshape(depth, 1, -1)
    attn_norm3 = attn_norm.astype(F32).reshape(depth, 1, d)
    ffn_norm3 = ffn_norm.astype(F32).reshape(depth, 1, d)
    subln3 = diff_subln.astype(F32).reshape(depth, 1, HEAD)
    hnorm3 = hgrn_norm.astype(F32).reshape(depth, 1, HEAD)

    c8 = jnp.pad(c.astype(F32), ((0, 8 - b), (0, 0)))
    mods = _ada(c8, w_ada, b_ada)
    bias = _bias_tiles(rel_bias, ATTN_T)

    y = None
    mod_prev = None
    for layer in range(depth):
        mod = mods[layer, :b].reshape(b, N_MOD, d)
        lambda_init = 0.8 - 0.6 * math.exp(-0.3 * layer)
        lam = (jnp.exp(jnp.sum(lambda_q1[layer].astype(F32) * lambda_k1[layer].astype(F32)))
               - jnp.exp(jnp.sum(lambda_q2[layer].astype(F32) * lambda_k2[layer].astype(F32)))
               + lambda_init)
        scalars = jnp.stack([lam, jnp.asarray(1.0 - lambda_init, F32)]).astype(F32)

        x, proj = _inproj(x, y, mod_prev, mod, attn_norm3, w_in_b, layer)
        attn_o = _attention(proj, bias, scalars, subln3, layer)
        hgrn_o = _hgrn(proj, lower_bounds, hnorm3, layer)
        x, hext = _outproj_router(attn_o, hgrn_o, w_out_b, x, mod, ffn_norm3, wr_hi, wr_lo, br, layer)
        y = _moe(hext, w_gate_b, w_up_b, w_down_b, layer)
        mod_prev = mod
    return _final(x, y, mod_prev, final_norm.astype(F32))
```

```python
import functools
import math

import numpy as np
import jax
import jax.numpy as jnp
from jax import lax
from jax.experimental import pallas as pl
from jax.experimental.pallas import tpu as pltpu

F32 = jnp.float32
BF16 = jnp.bfloat16
EPS = 1e-6

LANES = 128
HEAD = 128
QK = 64
N_HEADS = 8
NUM_BUCKETS = 32
MAX_DISTANCE = 128
N_EXPERTS = 16
N_GROUPS = 4
GROUP_SIZE = N_EXPERTS // N_GROUPS
N_MOD = 6
NEG = -1e30

ATTN_T = 512
ATTN_CW = 512
ONES_ROWS = 16
LOG2E = math.log2(math.e)
HGRN_L = 512
HGRN_C = 64
HGRN_SUB = 8
HGRN_GROUP = 256
HGRN_SAFE_SPAN = 80.0
NORM_ROWS = 256
MOE_TM = 512
MOE_ISSUE_UNROLL = MOE_TM
VMEM_LIMIT = 56 * 1024 * 1024


def _cparams(n_axes):
    return pltpu.CompilerParams(dimension_semantics=("arbitrary",) * n_axes, vmem_limit_bytes=VMEM_LIMIT)


def _ada_body(c_ref, w_ref, b_ref, o_ref):
    c = c_ref[...]
    ca = (c * jax.nn.sigmoid(c)).astype(BF16)
    o_ref[0] = jnp.dot(ca, w_ref[0].astype(BF16), preferred_element_type=F32) + b_ref[0]


def _ada(c8, w_ada, b_ada, tn=1024):
    depth, d, n = w_ada.shape
    return pl.pallas_call(
        _ada_body,
        grid=(depth, n // tn),
        in_specs=[pl.BlockSpec((8, d), lambda l, j: (0, 0)),
                  pl.BlockSpec((1, d, tn), lambda l, j: (l, 0, j)),
                  pl.BlockSpec((1, 1, tn), lambda l, j: (l, 0, j))],
        out_specs=pl.BlockSpec((1, 8, tn), lambda l, j: (l, 0, j)),
        out_shape=jax.ShapeDtypeStruct((depth, 8, n), F32),
        compiler_params=_cparams(2),
        name="ada_mod",
    )(c8, w_ada, b_ada.reshape(depth, 1, n))


def _modulated_norm(x, g, scale, shift):
    ms = jnp.mean(x * x, axis=-1, keepdims=True)
    return (x * lax.rsqrt(ms + EPS) * g) * (1.0 + scale) + shift


def _inproj_body(*refs, has_res):
    if has_res:
        x_ref, y_ref, gp_ref, mod_ref, g_ref, w_ref, xo_ref, p_ref, h_scr = refs
    else:
        x_ref, mod_ref, g_ref, w_ref, p_ref, h_scr = refs

    @pl.when(pl.program_id(2) == 0)
    def _():
        tm = h_scr.shape[0]
        for r0 in range(0, tm, NORM_ROWS):
            rows = slice(r0, r0 + NORM_ROWS)
            x = x_ref[0, rows, :]
            if has_res:
                x = x + gp_ref[0, 5:6, :] * y_ref[rows, :]
                xo_ref[0, rows, :] = x
            h = _modulated_norm(x, g_ref[0], mod_ref[0, 1:2, :], mod_ref[0, 0:1, :])
            h_scr[rows, :] = h.astype(BF16)

    p_ref[0] = jnp.dot(h_scr[...], w_ref[0], preferred_element_type=F32).astype(BF16)


def _inproj(x, y, mod_prev, mod, g_all, w_all, layer, tm=1024, tn=1024):
    b, s, d = x.shape
    n = w_all.shape[-1]
    nt = s // tm
    has_res = y is not None
    xspec = pl.BlockSpec((1, tm, d), lambda bi, i, j: (bi, i, 0))
    modspec = pl.BlockSpec((1, N_MOD, d), lambda bi, i, j: (bi, 0, 0))
    once = dict(pipeline_mode=pl.Buffered(1))
    in_specs = [pl.BlockSpec((1, tm, d), lambda bi, i, j: (bi, i, 0), **once)]
    args = [x]
    if has_res:
        in_specs += [pl.BlockSpec((tm, d), lambda bi, i, j: (bi * nt + i, 0), **once), modspec]
        args += [y, mod_prev]
    in_specs += [modspec,
                 pl.BlockSpec((1, 1, d), lambda bi, i, j: (layer, 0, 0)),
                 pl.BlockSpec((1, d, tn), lambda bi, i, j: (layer, 0, j))]
    args += [mod, g_all, w_all]
    pspec = pl.BlockSpec((1, tm, tn), lambda bi, i, j: (bi, i, j))
    pshape = jax.ShapeDtypeStruct((b, s, n), BF16)
    if has_res:
        out_specs, out_shape = [xspec, pspec], [jax.ShapeDtypeStruct(x.shape, F32), pshape]
    else:
        out_specs, out_shape = pspec, pshape
    out = pl.pallas_call(
        functools.partial(_inproj_body, has_res=has_res),
        grid=(b, nt, n // tn),
        in_specs=in_specs, out_specs=out_specs, out_shape=out_shape,
        scratch_shapes=[pltpu.VMEM((tm, d), BF16)],
        compiler_params=_cparams(3),
        name="inproj",
    )(*args)
    return (out[0], out[1]) if has_res else (x, out)


def _t5_bucket_np(n):
    max_exact = NUM_BUCKETS // 2
    nf = np.maximum(n, 1).astype(np.float32)
    large = max_exact + (np.log(nf / np.float32(max_exact)) / np.float32(math.log(MAX_DISTANCE / max_exact))
                         * np.float32(NUM_BUCKETS - max_exact)).astype(np.int32)
    large = np.minimum(large, NUM_BUCKETS - 1)
    return np.where(n < max_exact, n, large).astype(np.int32)


def _bias_body(tab_ref, bd_ref, bu_ref, o_ref):
    m = pl.program_id(0) * 2 + pl.program_id(1)
    bd = bd_ref[...]
    bu = bu_ref[...]
    far = tab_ref[NUM_BUCKETS - 1, m]
    accd = jnp.zeros(bd.shape, F32)
    accu = jnp.zeros(bu.shape, F32)
    for bkt in range(NUM_BUCKETS - 1):
        val = (tab_ref[bkt, m] - far) * LOG2E
        accd = jnp.where(bd == bkt, val, accd)
        accu = jnp.where(bu == bkt, val, accu)
    o_ref[0, 0] = jnp.where(bd < 0, NEG, accd)
    o_ref[0, 1] = accu


def _bias_tiles(rel_bias, t):
    key = np.arange(t)[:, None]
    qry = np.arange(t)[None, :]
    bd = np.where(qry >= key, _t5_bucket_np(np.maximum(qry - key, 0)), -1).astype(np.int32)
    bu = _t5_bucket_np(t + qry - key)
    assert MAX_DISTANCE <= t + 1, "tiles two or more to the left must lie in the far bucket"
    return pl.pallas_call(
        _bias_body,
        grid=(N_HEADS, 2),
        in_specs=[pl.BlockSpec(memory_space=pltpu.SMEM),
                  pl.BlockSpec((t, t), lambda h, m: (0, 0)),
                  pl.BlockSpec((t, t), lambda h, m: (0, 0))],
        out_specs=pl.BlockSpec((1, 2, t, t), lambda h, m: (h, 0, 0, m)),
        out_shape=jax.ShapeDtypeStruct((N_HEADS, 2, t, 2 * t), F32),
        compiler_params=_cparams(2),
        name="bias_tiles",
    )(rel_bias.astype(F32), jnp.asarray(bd), jnp.asarray(bu))


def _attn_body(sc_ref, q_ref, k_ref, v_ref, bias_ref, g_ref, o_ref, qq_scr, vt_scr, m_scr, acc_scr, s0_scr, s1_scr,
               *, t, cw):
    qi = pl.program_id(2)
    s_len = k_ref.shape[1]

    @pl.when(qi == 0)
    def _():
        for ci in range(s_len // t):
            vt_scr[0:HEAD, ci * t:(ci + 1) * t] = v_ref[0, ci * t:(ci + 1) * t, :].T
        row = lax.broadcasted_iota(jnp.int32, (ONES_ROWS, s_len), 0)
        vt_scr[HEAD:HEAD + ONES_ROWS, :] = jnp.where(row == 0, 1.0, 0.0).astype(BF16)

    q = (q_ref[0].astype(F32) * (QK ** -0.5 * LOG2E)).astype(BF16)
    lane = lax.broadcasted_iota(jnp.int32, q.shape, 1)
    zero = jnp.zeros_like(q)
    qq_scr[0:t, :] = jnp.where(lane < QK, q, zero)
    qq_scr[t:2 * t, :] = jnp.where(lane >= QK, q, zero)
    m_scr[...] = jnp.full(m_scr.shape, NEG, F32)
    acc_scr[...] = jnp.zeros(acc_scr.shape, F32)

    tiles = [slice(ct * cw, (ct + 1) * cw) for ct in range(2 * t // cw)]

    bufs = (s0_scr, s1_scr)

    def scores_into(j, buf):
        kb = k_ref[0, pl.ds(pl.multiple_of(j * t, t), t), :]
        for cs in tiles:
            buf[:, cs] = lax.dot_general(kb, qq_scr[cs, :], (((1,), (1,)), ((), ())), preferred_element_type=F32)

    def consume(j, buf, bias_of):
        vt = vt_scr[:, pl.ds(pl.multiple_of(j * t, t), t)]
        for cs in tiles:
            s = buf[:, cs]
            if bias_of is not None:
                s = s + bias_of(cs)
            m_prev = m_scr[:, cs]
            m_new = jnp.maximum(m_prev, jnp.max(s, axis=0, keepdims=True))
            alpha = jnp.exp2(m_prev - m_new)
            p = jnp.exp2(s - m_new)
            acc_scr[:, cs] = alpha * acc_scr[:, cs] + jnp.dot(vt, p.astype(BF16), preferred_element_type=F32)
            m_scr[:, cs] = m_new

    near = lambda cs: bias_ref[0, 1, :, cs]
    diag = lambda cs: bias_ref[0, 0, :, cs]

    scores_into(0, bufs[0])
    n_far = jnp.maximum(qi - 1, 0)

    def far_pair(pi, carry):
        j = 2 * pi
        scores_into(j + 1, bufs[1])
        consume(j, bufs[0], None)
        scores_into(j + 2, bufs[0])
        consume(j + 1, bufs[1], None)
        return carry

    lax.fori_loop(0, n_far // 2, far_pair, 0)

    @pl.when(qi == 0)
    def _():
        consume(0, bufs[0], diag)

    @pl.when(jnp.logical_and(qi >= 1, lax.rem(qi, 2) == 1))
    def _():
        scores_into(qi, bufs[1])
        consume(qi - 1, bufs[0], near)
        consume(qi, bufs[1], diag)

    @pl.when(jnp.logical_and(qi >= 2, lax.rem(qi, 2) == 0))
    def _():
        scores_into(qi - 1, bufs[1])
        consume(qi - 2, bufs[0], None)
        scores_into(qi, bufs[0])
        consume(qi - 1, bufs[1], near)
        consume(qi, bufs[0], diag)

    inv = 1.0 / acc_scr[HEAD:HEAD + 1, :]
    o = acc_scr[0:HEAD, 0:t] * inv[:, 0:t] - acc_scr[0:HEAD, t:2 * t] * (sc_ref[0] * inv[:, t:2 * t])
    ms = jnp.mean(o * o, axis=0, keepdims=True)
    o = (o * lax.rsqrt(ms + EPS)).T
    o_ref[0] = ((o * g_ref[0]) * sc_ref[1]).astype(BF16)


def _attention(proj, bias, scalars, subln_all, layer, t=ATTN_T):
    b, s, _ = proj.shape
    nq = s // t
    return pl.pallas_call(
        functools.partial(_attn_body, t=t, cw=ATTN_CW),
        grid=(b, N_HEADS, nq),
        in_specs=[pl.BlockSpec(memory_space=pltpu.SMEM),
                  pl.BlockSpec((1, t, HEAD), lambda bi, h, i: (bi, i, h)),
                  pl.BlockSpec((1, s, HEAD), lambda bi, h, i: (bi, 0, N_HEADS + h)),
                  pl.BlockSpec((1, s, HEAD), lambda bi, h, i: (bi, 0, 2 * N_HEADS + h)),
                  pl.BlockSpec((1, 2, t, 2 * t), lambda bi, h, i: (h, 0, 0, 0)),
                  pl.BlockSpec((1, 1, HEAD), lambda bi, h, i: (layer, 0, 0))],
        out_specs=pl.BlockSpec((1, t, HEAD), lambda bi, h, i: (bi, i, h)),
        out_shape=jax.ShapeDtypeStruct((b, s, N_HEADS * HEAD), BF16),
        scratch_shapes=[pltpu.VMEM((2 * t, HEAD), BF16),
                        pltpu.VMEM((HEAD + ONES_ROWS, s), BF16),
                        pltpu.VMEM((1, 2 * t), F32),
                        pltpu.VMEM((HEAD + ONES_ROWS, 2 * t), F32),
                        pltpu.VMEM((t, 2 * t), F32),
                        pltpu.VMEM((t, 2 * t), F32)],
        compiler_params=_cparams(3),
        name="diff_attention",
    )(scalars, proj, proj, proj, bias, subln_all)


def _hgrn_levels(c, sub):
    levels = []
    g = c // 2
    while g >= sub:
        levels.append(g)
        g //= 2
    return levels


def _hgrn_consts(l, c, sub):
    r = np.arange(HGRN_GROUP)[:, None]
    s = np.arange(HGRN_GROUP)[None, :]
    lcum = ((r // c == s // c) & (s <= r)).astype(np.float32)
    rc = np.arange(c)[:, None]
    sc = np.arange(c)[None, :]
    masks = [((rc // (2 * g) == sc // (2 * g)) & (rc % (2 * g) >= g) & (sc % (2 * g) < g)).astype(np.float32)
             for g in _hgrn_levels(c, sub)]
    half = c // 2
    rg = np.arange(HGRN_GROUP)[:, None]
    sg = np.arange(HGRN_GROUP)[None, :]
    top = (rg // c == sg // c) & (rg % c >= half) & (sg % c < half)
    near = (rg // half == sg // half) & (sg <= rg)
    fmask = np.stack([top, near]).astype(np.float32)
    return jnp.asarray(lcum, BF16), jnp.asarray(np.stack(masks), F32), jnp.asarray(fmask, F32)


def _hgrn_body(q_ref, f_ref, i_ref, g_ref, lb_ref, ng_ref, lcum_ref, mask_ref, fmask_ref, o_ref,
               st_scr, stb_scr, a_scr, oi_scr, *, l, c, sub):
    nchunk = l // c
    levels = _hgrn_levels(c, sub)

    @pl.when(pl.program_id(2) == 0)
    def _():
        st_scr[...] = jnp.zeros(st_scr.shape, F32)

    q = q_ref[0].astype(F32)
    fr = f_ref[0].astype(F32)
    vb = i_ref[0]
    v = vb.astype(F32)
    lb = lb_ref[0]
    f = lb + (1.0 - lb) * jax.nn.sigmoid(fr)
    logf = jnp.log(jnp.maximum(f, jnp.finfo(F32).tiny))
    k = (1.0 - lb) * jax.nn.sigmoid(-fr)

    hi = logf.astype(BF16)
    lo = (logf - hi.astype(F32)).astype(BF16)
    hilo = jnp.concatenate([hi, lo], axis=1)
    lcum = lcum_ref[...]
    grp = lcum.shape[0]
    parts = []
    for gi in range(l // grp):
        both = jnp.dot(lcum, hilo[gi * grp:(gi + 1) * grp], preferred_element_type=F32)
        parts.append(both[:, 0:HEAD] + both[:, HEAD:2 * HEAD])
    bcum = jnp.concatenate(parts, axis=0)

    def rows_of(arr, group, row):
        a3 = arr.reshape(l // group, group, HEAD)
        return jnp.broadcast_to(a3[:, row:row + 1, :], a3.shape).reshape(l, HEAD)

    nt = (((1,), (1,)), ((), ()))
    tn = (((0,), (0,)), ((), ()))

    def level_operands(g):
        ref = rows_of(bcum, 2 * g, g - 1)
        return ((q * jnp.exp(jnp.minimum(bcum - ref, 0.0))).astype(BF16),
                (k * jnp.exp(jnp.minimum(ref - bcum, 0.0))).astype(BF16))

    q_top, k_top = level_operands(levels[0])

    def top_scores(sl):
        return mask_ref[0] * lax.dot_general(q_top[sl], k_top[sl], nt, preferred_element_type=F32)

    half = c // 2
    b_start = rows_of(bcum, half, 0) - rows_of(logf, half, 0)
    span = b_start - bcum
    bounded = jnp.max(span) <= HGRN_SAFE_SPAN

    @pl.when(bounded)
    def _():
        q_f = (q * jnp.exp(-span)).astype(BF16)
        k_f = (k * jnp.exp(span)).astype(BF16)
        grp = fmask_ref.shape[-1]
        for gi in range(l // grp):
            sl = slice(gi * grp, (gi + 1) * grp)
            a = fmask_ref[0] * lax.dot_general(q_top[sl], k_top[sl], nt, preferred_element_type=F32)
            a = a + fmask_ref[1] * lax.dot_general(q_f[sl], k_f[sl], nt, preferred_element_type=F32)
            a_scr[sl, :] = jnp.dot(a.astype(BF16), vb[sl], preferred_element_type=F32)

    @pl.when(jnp.logical_not(bounded))
    def _():
        lower = [level_operands(g) for g in levels[1:]]
        nb = l // sub
        b3 = bcum.reshape(nb, sub, HEAD)
        q3 = q.reshape(nb, sub, HEAD)
        k3 = k.reshape(nb, sub, HEAD)
        v3 = v.reshape(nb, sub, HEAD)
        tloc = lax.broadcasted_iota(jnp.int32, (nb, sub, 1), 1)
        od = jnp.zeros((nb, sub, HEAD), F32)
        for si in range(sub):
            e = jnp.exp(b3 - b3[:, si:si + 1, :])
            pr = q3 * (k3[:, si:si + 1, :] * e)
            w = jnp.sum(pr, axis=-1, keepdims=True)
            w = jnp.where(tloc >= si, w, 0.0)
            od = od + w * v3[:, si:si + 1, :]
        od = od.reshape(l, HEAD)
        for ci in range(nchunk):
            sl = slice(ci * c, (ci + 1) * c)
            a = top_scores(sl)
            for li, (q_g, k_g) in enumerate(lower):
                a = a + mask_ref[li + 1] * lax.dot_general(q_g[sl], k_g[sl], nt, preferred_element_type=F32)
            a_scr[sl, :] = jnp.dot(a.astype(BF16), vb[sl], preferred_element_type=F32) + od[sl]

    b_end = rows_of(bcum, c, c - 1)
    q_in = (q * jnp.exp(bcum)).astype(BF16)
    k_out = (k * jnp.exp(b_end - bcum)).astype(BF16)
    dec = jnp.exp(b_end)
    kv = [lax.dot_general(vb[ci * c:(ci + 1) * c], k_out[ci * c:(ci + 1) * c], tn, preferred_element_type=F32)
          for ci in range(nchunk)]
    st = st_scr[...]
    for ci in range(nchunk):
        stb_scr[ci] = st.astype(BF16)
        st = st * dec[ci * c:ci * c + 1, :] + kv[ci]
    st_scr[...] = st
    for ci in range(nchunk):
        sl = slice(ci * c, (ci + 1) * c)
        oi_scr[sl, :] = lax.dot_general(q_in[sl], stb_scr[ci], nt, preferred_element_type=F32)

    o = a_scr[...] + oi_scr[...]
    ms = jnp.mean(o * o, axis=-1, keepdims=True)
    o = o * lax.rsqrt(ms + EPS) * ng_ref[0]
    gate = g_ref[0].astype(F32)
    o_ref[0] = (o * (gate * jax.nn.sigmoid(gate))).astype(BF16)


def _hgrn(proj, lower_bounds, norm_all, layer, l=HGRN_L, c=HGRN_C, sub=HGRN_SUB):
    b, s, _ = proj.shape
    lcum, masks, fmask = _hgrn_consts(l, c, sub)
    base = 3 * N_HEADS

    def col(kind):
        return pl.BlockSpec((1, l, HEAD), lambda bi, h, i: (bi, i, base + kind * N_HEADS + h))

    return pl.pallas_call(
        functools.partial(_hgrn_body, l=l, c=c, sub=sub),
        grid=(b, N_HEADS, s // l),
        in_specs=[col(0), col(1), col(2), col(3),
                  pl.BlockSpec((1, 1, HEAD), lambda bi, h, i: (layer, 0, h)),
                  pl.BlockSpec((1, 1, HEAD), lambda bi, h, i: (layer, 0, 0)),
                  pl.BlockSpec(lcum.shape, lambda bi, h, i: (0, 0)),
                  pl.BlockSpec(masks.shape, lambda bi, h, i: (0, 0, 0)),
                  pl.BlockSpec(fmask.shape, lambda bi, h, i: (0, 0, 0))],
        out_specs=pl.BlockSpec((1, l, HEAD), lambda bi, h, i: (bi, i, h)),
        out_shape=jax.ShapeDtypeStruct((b, s, N_HEADS * HEAD), BF16),
        scratch_shapes=[pltpu.VMEM((HEAD, HEAD), F32),
                        pltpu.VMEM((l // c, HEAD, HEAD), BF16),
                        pltpu.VMEM((l, HEAD), F32),
                        pltpu.VMEM((l, HEAD), F32)],
        compiler_params=_cparams(3),
        name="hgrn2",
    )(proj, proj, proj, proj, lower_bounds, norm_all, lcum, masks, fmask)


def _first_lane(cond, lane):
    return jnp.min(jnp.where(cond, lane, LANES), axis=-1, keepdims=True)


def _outproj_body(a_ref, hg_ref, w_ref, x_ref, mod_ref, g_ref, wr_ref, br_ref, xo_ref, h_ref, *, d):
    da = a_ref.shape[-1]
    mixed = jnp.dot(a_ref[0], w_ref[0, 0:da, :], preferred_element_type=F32)
    mixed = mixed + jnp.dot(hg_ref[0], w_ref[0, da:, :], preferred_element_type=F32)
    x = x_ref[0] + mod_ref[0, 2:3, :] * mixed
    xo_ref[0] = x
    h = _modulated_norm(x, g_ref[0], mod_ref[0, 4:5, :], mod_ref[0, 3:4, :])
    h_ref[0, :, 0:d] = h

    h_hi = h.astype(BF16)
    h_lo = (h - h_hi.astype(F32)).astype(BF16)
    both = (jnp.dot(h_hi, wr_ref[...], preferred_element_type=F32)
            + jnp.dot(h_lo, wr_ref[...], preferred_element_type=F32))
    logits = both + pltpu.roll(both, LANES - N_EXPERTS, 1) + br_ref[...]
    lane = lax.broadcasted_iota(jnp.int32, logits.shape, 1)
    valid = lane < N_EXPERTS
    logits = jnp.where(valid, logits, NEG)
    mx = jnp.max(logits, axis=-1, keepdims=True)
    ex = jnp.where(valid, jnp.exp(logits - mx), 0.0)
    probs = ex / jnp.sum(ex, axis=-1, keepdims=True)

    best = sel = v1 = v2 = i1 = i2 = None
    for gi in range(N_GROUPS):
        ing = (lane >= gi * GROUP_SIZE) & (lane < (gi + 1) * GROUP_SIZE)
        pg = jnp.where(ing, probs, -1.0)
        m1 = jnp.max(pg, axis=-1, keepdims=True)
        a1 = _first_lane(pg == m1, lane)
        pg2 = jnp.where(lane == a1, -1.0, pg)
        m2 = jnp.max(pg2, axis=-1, keepdims=True)
        a2 = _first_lane(pg2 == m2, lane)
        score = m1 + m2
        if gi == 0:
            best, sel, v1, v2, i1, i2 = score, jnp.zeros_like(a1), m1, m2, a1, a2
        else:
            better = score > best
            best = jnp.where(better, score, best)
            sel = jnp.where(better, gi, sel)
            v1 = jnp.where(better, m1, v1)
            v2 = jnp.where(better, m2, v2)
            i1 = jnp.where(better, a1, i1)
            i2 = jnp.where(better, a2, i2)
    tot = v1 + v2
    w1 = v1 / tot
    w2 = v2 / tot
    loc1 = i1 - sel * GROUP_SIZE
    loc2 = i2 - sel * GROUP_SIZE
    ext = jnp.where(lane == loc1, w1, 0.0) + jnp.where(lane == loc2, w2, 0.0)
    ext = jnp.where(lane == GROUP_SIZE, sel.astype(F32), ext)
    h_ref[0, :, d:] = ext


def _outproj_router(attn_o, hgrn_o, w_out_all, x, mod, ffn_norm_all, wr_cat, br, layer, tm=512):
    b, s, d = x.shape
    da = attn_o.shape[-1]
    dmix = w_out_all.shape[1]
    tok = lambda bi, i: (bi, i, 0)
    return pl.pallas_call(
        functools.partial(_outproj_body, d=d),
        grid=(b, s // tm),
        in_specs=[pl.BlockSpec((1, tm, da), tok),
                  pl.BlockSpec((1, tm, dmix - da), tok),
                  pl.BlockSpec((1, dmix, d), lambda bi, i: (layer, 0, 0)),
                  pl.BlockSpec((1, tm, d), tok),
                  pl.BlockSpec((1, N_MOD, d), lambda bi, i: (bi, 0, 0)),
                  pl.BlockSpec((1, 1, d), lambda bi, i: (layer, 0, 0)),
                  pl.BlockSpec((d, LANES), lambda bi, i: (0, 0)),
                  pl.BlockSpec((1, LANES), lambda bi, i: (0, 0))],
        out_specs=[pl.BlockSpec((1, tm, d), tok),
                   pl.BlockSpec((1, tm, d + LANES), tok)],
        out_shape=[jax.ShapeDtypeStruct((b, s, d), F32),
                   jax.ShapeDtypeStruct((b, s, d + LANES), F32)],
        compiler_params=_cparams(2),
        name="outproj_router",
    )(attn_o, hgrn_o, w_out_all, x, mod, ffn_norm_all, wr_cat, br)


def _moe_body(tg_ref, nv_ref, nu_ref, tok_ref, h_hbm, wg_ref, wu_ref, wd_ref, y_hbm,
              xbuf, xbf, acc, gsem, ssem, *, tm, d):
    i = pl.program_id(0)
    e = pl.program_id(1)
    n_used = nu_ref[0]
    slot = lax.rem(i, 2)

    def gather_start(tile, sl):
        base = tile * tm

        def body(r, carry):
            pltpu.make_async_copy(h_hbm.at[pl.ds(tok_ref[base + r], 1)], xbuf.at[sl, pl.ds(r, 1)], gsem.at[sl]).start()
            return carry

        lax.fori_loop(0, tm, body, 0, unroll=MOE_ISSUE_UNROLL)

    def gather_wait(sl):
        pltpu.make_async_copy(h_hbm.at[pl.ds(0, tm)], xbuf.at[sl], gsem.at[sl]).wait()

    def scatter_start(tile, sl, n):
        base = tile * tm

        def body(r, carry):
            pltpu.make_async_copy(acc.at[sl, pl.ds(r, 1)], y_hbm.at[pl.ds(tok_ref[base + r], 1)], ssem.at[sl]).start()
            return carry

        @pl.when(n == tm)
        def _():
            lax.fori_loop(0, tm, body, 0, unroll=MOE_ISSUE_UNROLL)

        @pl.when(n < tm)
        def _():
            lax.fori_loop(0, n, body, 0)

    def scatter_wait(sl, n):
        @pl.when(n == tm)
        def _():
            pltpu.make_async_copy(acc.at[sl], y_hbm.at[pl.ds(0, tm)], ssem.at[sl]).wait()

        @pl.when(n < tm)
        def _():
            def body(r, carry):
                pltpu.make_async_copy(acc.at[sl, pl.ds(r, 1)], y_hbm.at[pl.ds(r, 1)], ssem.at[sl]).wait()
                return carry

            lax.fori_loop(0, n, body, 0)

    @pl.when(i < n_used)
    def _():
        @pl.when(e == 0)
        def _():
            @pl.when(i == 0)
            def _():
                gather_start(0, 0)

            gather_wait(slot)

            @pl.when(i + 1 < n_used)
            def _():
                gather_start(i + 1, 1 - slot)

            xbf[...] = xbuf[slot, :, 0:d].astype(BF16)

            @pl.when(i >= 2)
            def _():
                scatter_wait(slot, nv_ref[jnp.maximum(i - 2, 0)])

            acc[slot] = jnp.zeros((tm, d), F32)

        ext = xbuf[slot, :, d:d + LANES]
        gate = jnp.zeros((tm, 1), F32)
        for ei in range(GROUP_SIZE):
            gate = jnp.where(e == ei, ext[:, ei:ei + 1], gate)
        x = xbf[...]
        hg = jnp.dot(x, wg_ref[0, 0], preferred_element_type=F32)
        hu = jnp.dot(x, wu_ref[0, 0], preferred_element_type=F32)
        act = (hg * jax.nn.sigmoid(hg)) * hu * gate
        acc[slot] += jnp.dot(act.astype(BF16), wd_ref[0, 0], preferred_element_type=F32)

        @pl.when(e == GROUP_SIZE - 1)
        def _():
            scatter_start(i, slot, nv_ref[i])

            @pl.when(i == n_used - 1)
            def _():
                scatter_wait(slot, nv_ref[i])

                @pl.when(i >= 1)
                def _():
                    scatter_wait(1 - slot, nv_ref[jnp.maximum(i - 1, 0)])


def _route(hext, d, tm):
    b, s, _ = hext.shape
    t = b * s
    n_tiles = t // tm + N_GROUPS
    gid = hext[:, :, d + GROUP_SIZE].reshape(t).astype(jnp.int32)
    onehot = (gid[:, None] == jnp.arange(N_GROUPS, dtype=jnp.int32)[None, :]).astype(jnp.int32)
    cnt = jnp.sum(onehot, axis=0)
    rank = jnp.sum(onehot * (jnp.cumsum(onehot, axis=0) - onehot), axis=1)
    ntile_g = (cnt + tm - 1) // tm
    tile_end_g = jnp.cumsum(ntile_g)
    tile_start_g = tile_end_g - ntile_g
    pos = (tile_start_g * tm)[gid] + rank
    tok_of_row = jnp.zeros((n_tiles * tm,), jnp.int32).at[pos].set(jnp.arange(t, dtype=jnp.int32))
    tile = jnp.arange(n_tiles, dtype=jnp.int32)
    tile_gid = jnp.minimum(jnp.sum((tile[:, None] >= tile_end_g[None, :]).astype(jnp.int32), axis=1), N_GROUPS - 1)
    nvalid = jnp.clip(cnt[tile_gid] - (tile - tile_start_g[tile_gid]) * tm, 0, tm).astype(jnp.int32)
    n_used = tile_end_g[N_GROUPS - 1:].astype(jnp.int32)
    return tile_gid.astype(jnp.int32), nvalid, n_used, tok_of_row


def _moe(hext, wg_all, wu_all, wd_all, layer, tm=MOE_TM):
    b, s, dx = hext.shape
    d = dx - LANES
    t = b * s
    f = wg_all.shape[-1]
    tile_gid, nvalid, n_used, tok_of_row = _route(hext, d, tm)
    n_tiles = tile_gid.shape[0]

    def wmap(i, e, tg, nv, nu, tok):
        return (layer, tg[i] * GROUP_SIZE + e, 0, 0)

    return pl.pallas_call(
        functools.partial(_moe_body, tm=tm, d=d),
        grid_spec=pltpu.PrefetchScalarGridSpec(
            num_scalar_prefetch=4,
            grid=(n_tiles, GROUP_SIZE),
            in_specs=[pl.BlockSpec(memory_space=pl.ANY),
                      pl.BlockSpec((1, 1, d, f), wmap),
                      pl.BlockSpec((1, 1, d, f), wmap),
                      pl.BlockSpec((1, 1, f, d), wmap)],
            out_specs=pl.BlockSpec(memory_space=pl.ANY),
            scratch_shapes=[pltpu.VMEM((2, tm, dx), F32),
                            pltpu.VMEM((tm, d), BF16),
                            pltpu.VMEM((2, tm, d), F32),
                            pltpu.SemaphoreType.DMA((2,)),
                            pltpu.SemaphoreType.DMA((2,))],
        ),
        out_shape=jax.ShapeDtypeStruct((t, d), F32),
        compiler_params=_cparams(2),
        name="moe_experts",
    )(tile_gid, nvalid, n_used, tok_of_row, hext.reshape(t, dx), wg_all, wu_all, wd_all)


def _final_body(x_ref, y_ref, mod_ref, g_ref, o_ref):
    x = x_ref[0] + mod_ref[0, 5:6, :] * y_ref[...]
    ms = jnp.mean(x * x, axis=-1, keepdims=True)
    o_ref[0] = x * lax.rsqrt(ms + EPS) * g_ref[...]


def _final(x, y, mod, g, tm=512):
    b, s, d = x.shape
    nt = s // tm
    return pl.pallas_call(
        _final_body,
        grid=(b, nt),
        in_specs=[pl.BlockSpec((1, tm, d), lambda bi, i: (bi, i, 0)),
                  pl.BlockSpec((tm, d), lambda bi, i: (bi * nt + i, 0)),
                  pl.BlockSpec((1, N_MOD, d), lambda bi, i: (bi, 0, 0)),
                  pl.BlockSpec((1, d), lambda bi, i: (0, 0))],
        out_specs=pl.BlockSpec((1, tm, d), lambda bi, i: (bi, i, 0)),
        out_shape=jax.ShapeDtypeStruct(x.shape, F32),
        compiler_params=_cparams(2),
        name="final_norm",
    )(x, y, mod, g.reshape(1, d))


def kernel(x, c, w_in, w_out, attn_norm, ffn_norm, w_ada, b_ada, lambda_q1, lambda_k1, lambda_q2, lambda_k2,
           diff_subln, hgrn_lb, hgrn_norm, rel_bias, w_router, b_router, w_gate, w_up, w_down, final_norm):
    b, s, d = x.shape
    depth = w_in.shape[0]

    w_in_b = w_in.astype(BF16)
    w_out_b = w_out.astype(BF16)
    w_gate_b = w_gate.astype(BF16)
    w_up_b = w_up.astype(BF16)
    w_down_b = w_down.astype(BF16)
    wr = w_router.astype(F32)
    wr_hi = wr.astype(BF16)
    wr_lo = (wr - wr_hi.astype(F32)).astype(BF16)
    wr_cat = jnp.pad(jnp.concatenate([wr_hi, wr_lo], axis=1), ((0, 0), (0, LANES - 2 * N_EXPERTS)))
    br =jnp.pad(b_router.astype(F32), (0, LANES - N_EXPERTS)).reshape(1, LANES)
    lb_soft = jax.nn.softmax(hgrn_lb.astype(F32), axis=0)
    lower_bounds = jnp.maximum(jnp.cumsum(lb_soft, axis=0) - lb_soft[0:1], 0.0).reshape(depth, 1, -1)
    attn_norm3 = attn_norm.astype(F32).reshape(depth, 1, d)
    ffn_norm3 = ffn_norm.astype(F32).reshape(depth, 1, d)
    subln3 = diff_subln.astype(F32).reshape(depth, 1, HEAD)
    hnorm3 = hgrn_norm.astype(F32).reshape(depth, 1, HEAD)

    c8 = jnp.pad(c.astype(F32), ((0, 8 - b), (0, 0)))
    mods = _ada(c8, w_ada, b_ada)
    bias = _bias_tiles(rel_bias, ATTN_T)

    y = None
    mod_prev = None
    for layer in range(depth):
        mod = mods[layer, :b].reshape(b, N_MOD, d)
        lambda_init = 0.8 - 0.6 * math.exp(-0.3 * layer)
        lam = (jnp.exp(jnp.sum(lambda_q1[layer].astype(F32) * lambda_k1[layer].astype(F32)))
               - jnp.exp(jnp.sum(lambda_q2[layer].astype(F32) * lambda_k2[layer].astype(F32)))
               + lambda_init)
        scalars = jnp.stack([lam, jnp.asarray(1.0 - lambda_init, F32)]).astype(F32)

        x, proj = _inproj(x, y, mod_prev, mod, attn_norm3, w_in_b, layer)
        attn_o = _attention(proj, bias, scalars, subln3, layer)
        hgrn_o = _hgrn(proj, lower_bounds, hnorm3, layer)
        x, hext = _outproj_router(attn_o, hgrn_o, w_out_b, x, mod, ffn_norm3, wr_cat, br, layer)
        y = _moe(hext, w_gate_b, w_up_b, w_down_b, layer)
        mod_prev = mod
    return _final(x, y, mod_prev, final_norm.astype(F32))
```

```python
import functools
import math

import numpy as np
import jax
import jax.numpy as jnp
from jax import lax
from jax.experimental import pallas as pl
from jax.experimental.pallas import tpu as pltpu

F32 = jnp.float32
BF16 = jnp.bfloat16
EPS = 1e-6

LANES = 128
HEAD = 128
QK = 64
N_HEADS = 8
NUM_BUCKETS = 32
MAX_DISTANCE = 128
N_EXPERTS = 16
N_GROUPS = 4
GROUP_SIZE = N_EXPERTS // N_GROUPS
N_MOD = 6
NEG = -1e30

ATTN_T = 512
ATTN_CW = 512
ONES_ROWS = 16
LOG2E = math.log2(math.e)
HGRN_L = 512
HGRN_C = 64
HGRN_SUB = 8
HGRN_GROUP = 256
HGRN_SAFE_SPAN = 80.0
NORM_ROWS = 256
INPROJ_TM = 1024
INPROJ_TM_RES = 512
MOE_TM = 256
MOE_ISSUE_UNROLL = MOE_TM
VMEM_LIMIT = 56 * 1024 * 1024


def _cparams(n_axes, vmem_limit=VMEM_LIMIT):
    return pltpu.CompilerParams(dimension_semantics=("arbitrary",) * n_axes, vmem_limit_bytes=vmem_limit)


def _ada_body(c_ref, w_ref, b_ref, o_ref):
    c = c_ref[...]
    ca = (c * jax.nn.sigmoid(c)).astype(BF16)
    o_ref[0] = jnp.dot(ca, w_ref[0].astype(BF16), preferred_element_type=F32) + b_ref[0]


def _ada(c8, w_ada, b_ada, tn=1024):
    depth, d, n = w_ada.shape
    return pl.pallas_call(
        _ada_body,
        grid=(depth, n // tn),
        in_specs=[pl.BlockSpec((8, d), lambda l, j: (0, 0)),
                  pl.BlockSpec((1, d, tn), lambda l, j: (l, 0, j)),
                  pl.BlockSpec((1, 1, tn), lambda l, j: (l, 0, j))],
        out_specs=pl.BlockSpec((1, 8, tn), lambda l, j: (l, 0, j)),
        out_shape=jax.ShapeDtypeStruct((depth, 8, n), F32),
        compiler_params=_cparams(2),
        name="ada_mod",
    )(c8, w_ada, b_ada.reshape(depth, 1, n))


def _modulated_norm(x, g, scale, shift):
    ms = jnp.mean(x * x, axis=-1, keepdims=True)
    return (x * lax.rsqrt(ms + EPS) * g) * (1.0 + scale) + shift


def _combine(gates, y0, y1):
    return gates[:, 0:1] * y0 + gates[:, 1:2] * y1


def _inproj_body(*refs, has_res):
    if has_res:
        x_ref, y0_ref, y1_ref, gate_ref, gp_ref, mod_ref, g_ref, w_ref, xo_ref, p_ref, h_scr = refs
    else:
        x_ref, mod_ref, g_ref, w_ref, p_ref, h_scr = refs

    @pl.when(pl.program_id(2) == 0)
    def _():
        tm = h_scr.shape[0]
        for r0 in range(0, tm, NORM_ROWS):
            rows = slice(r0, r0 + NORM_ROWS)
            x = x_ref[0, rows, :]
            if has_res:
                x = x + gp_ref[0, 5:6, :] * _combine(gate_ref[0, rows, :], y0_ref[rows, :], y1_ref[rows, :])
                xo_ref[0, rows, :] = x
            h = _modulated_norm(x, g_ref[0], mod_ref[0, 1:2, :], mod_ref[0, 0:1, :])
            h_scr[rows, :] = h.astype(BF16)

    p_ref[0] = jnp.dot(h_scr[...], w_ref[0], preferred_element_type=F32).astype(BF16)


def _inproj(x, y, hext, mod_prev, mod, g_all, w_all, layer, tn=1024):
    b, s, d = x.shape
    n = w_all.shape[-1]
    has_res = y is not None
    tm = INPROJ_TM_RES if has_res else INPROJ_TM
    nt = s // tm
    xspec = pl.BlockSpec((1, tm, d), lambda bi, i, j: (bi, i, 0))
    modspec = pl.BlockSpec((1, N_MOD, d), lambda bi, i, j: (bi, 0, 0))
    once = dict(pipeline_mode=pl.Buffered(1))
    in_specs = [pl.BlockSpec((1, tm, d), lambda bi, i, j: (bi, i, 0), **once)]
    args = [x]
    if has_res:
        in_specs += [pl.BlockSpec((tm, d), lambda bi, i, j: (bi * nt + i, 0), **once),
                     pl.BlockSpec((tm, d), lambda bi, i, j: (b * nt + bi * nt + i, 0), **once),
                     pl.BlockSpec((1, tm, LANES), lambda bi, i, j: (bi, i, d // LANES), **once),
                     modspec]
        args += [y, y, hext, mod_prev]
    in_specs += [modspec,
                 pl.BlockSpec((1, 1, d), lambda bi, i, j: (layer, 0, 0)),
                 pl.BlockSpec((1, d, tn), lambda bi, i, j: (layer, 0, j))]
    args += [mod, g_all, w_all]
    pspec = pl.BlockSpec((1, tm, tn), lambda bi, i, j: (bi, i, j))
    pshape = jax.ShapeDtypeStruct((b, s, n), BF16)
    if has_res:
        out_specs, out_shape = [xspec, pspec], [jax.ShapeDtypeStruct(x.shape, F32), pshape]
    else:
        out_specs, out_shape = pspec, pshape
    out = pl.pallas_call(
        functools.partial(_inproj_body, has_res=has_res),
        grid=(b, nt, n // tn),
        in_specs=in_specs, out_specs=out_specs, out_shape=out_shape,
        scratch_shapes=[pltpu.VMEM((tm, d), BF16)],
        compiler_params=_cparams(3),
        name="inproj",
    )(*args)
    return (out[0], out[1]) if has_res else (x, out)


def _t5_bucket_np(n):
    max_exact = NUM_BUCKETS // 2
    nf = np.maximum(n, 1).astype(np.float32)
    large = max_exact + (np.log(nf / np.float32(max_exact)) / np.float32(math.log(MAX_DISTANCE / max_exact))
                         * np.float32(NUM_BUCKETS - max_exact)).astype(np.int32)
    large = np.minimum(large, NUM_BUCKETS - 1)
    return np.where(n < max_exact, n, large).astype(np.int32)


def _bias_body(tab_ref, bd_ref, bu_ref, o_ref):
    m = pl.program_id(0) * 2 + pl.program_id(1)
    bd = bd_ref[...]
    bu = bu_ref[...]
    far = tab_ref[NUM_BUCKETS - 1, m]
    accd = jnp.zeros(bd.shape, F32)
    accu = jnp.zeros(bu.shape, F32)
    for bkt in range(NUM_BUCKETS - 1):
        val = (tab_ref[bkt, m] - far) * LOG2E
        accd = jnp.where(bd == bkt, val, accd)
        accu = jnp.where(bu == bkt, val, accu)
    o_ref[0, 0] = jnp.where(bd < 0, NEG, accd)
    o_ref[0, 1] = accu


def _bias_tiles(rel_bias, t):
    key = np.arange(t)[:, None]
    qry = np.arange(t)[None, :]
    bd = np.where(qry >= key, _t5_bucket_np(np.maximum(qry - key, 0)), -1).astype(np.int32)
    bu = _t5_bucket_np(t + qry - key)
    assert MAX_DISTANCE <= t + 1, "tiles two or more to the left must lie in the far bucket"
    return pl.pallas_call(
        _bias_body,
        grid=(N_HEADS, 2),
        in_specs=[pl.BlockSpec(memory_space=pltpu.SMEM),
                  pl.BlockSpec((t, t), lambda h, m: (0, 0)),
                  pl.BlockSpec((t, t), lambda h, m: (0, 0))],
        out_specs=pl.BlockSpec((1, 2, t, t), lambda h, m: (h, 0, 0, m)),
        out_shape=jax.ShapeDtypeStruct((N_HEADS, 2, t, 2 * t), F32),
        compiler_params=_cparams(2),
        name="bias_tiles",
    )(rel_bias.astype(F32), jnp.asarray(bd), jnp.asarray(bu))


def _attn_body(sc_ref, q_ref, k_ref, v_ref, bias_ref, g_ref, o_ref, qq_scr, vt_scr, m_scr, acc_scr, s0_scr, s1_scr,
               *, t, cw):
    qi = pl.program_id(2)
    s_len = k_ref.shape[1]

    @pl.when(qi == 0)
    def _():
        for ci in range(s_len // t):
            vt_scr[0:HEAD, ci * t:(ci + 1) * t] = v_ref[0, ci * t:(ci + 1) * t, :].T
        row = lax.broadcasted_iota(jnp.int32, (ONES_ROWS, s_len), 0)
        vt_scr[HEAD:HEAD + ONES_ROWS, :] = jnp.where(row == 0, 1.0, 0.0).astype(BF16)

    q = (q_ref[0].astype(F32) * (QK ** -0.5 * LOG2E)).astype(BF16)
    lane = lax.broadcasted_iota(jnp.int32, q.shape, 1)
    zero = jnp.zeros_like(q)
    qq_scr[0:t, :] = jnp.where(lane < QK, q, zero)
    qq_scr[t:2 * t, :] = jnp.where(lane >= QK, q, zero)
    m_scr[...] = jnp.full(m_scr.shape, NEG, F32)
    acc_scr[...] = jnp.zeros(acc_scr.shape, F32)

    tiles = [slice(ct * cw, (ct + 1) * cw) for ct in range(2 * t // cw)]

    bufs = (s0_scr, s1_scr)

    def scores_into(j, buf):
        kb = k_ref[0, pl.ds(pl.multiple_of(j * t, t), t), :]
        for cs in tiles:
            buf[:, cs] = lax.dot_general(kb, qq_scr[cs, :], (((1,), (1,)), ((), ())), preferred_element_type=F32)

    def consume(j, buf, bias_of):
        vt = vt_scr[:, pl.ds(pl.multiple_of(j * t, t), t)]
        for cs in tiles:
            s = buf[:, cs]
            if bias_of is not None:
                s = s + bias_of(cs)
            m_prev = m_scr[:, cs]
            m_new = jnp.maximum(m_prev, jnp.max(s, axis=0, keepdims=True))
            alpha = jnp.exp2(m_prev - m_new)
            p = jnp.exp2(s - m_new)
            acc_scr[:, cs] = alpha * acc_scr[:, cs] + jnp.dot(vt, p.astype(BF16), preferred_element_type=F32)
            m_scr[:, cs] = m_new

    near = lambda cs: bias_ref[0, 1, :, cs]
    diag = lambda cs: bias_ref[0, 0, :, cs]

    scores_into(0, bufs[0])
    n_far = jnp.maximum(qi - 1, 0)

    def far_pair(pi, carry):
        j = 2 * pi
        scores_into(j + 1, bufs[1])
        consume(j, bufs[0], None)
        scores_into(j + 2, bufs[0])
        consume(j + 1, bufs[1], None)
        return carry

    lax.fori_loop(0, n_far // 2, far_pair, 0)

    @pl.when(qi == 0)
    def _():
        consume(0, bufs[0], diag)

    @pl.when(jnp.logical_and(qi >= 1, lax.rem(qi, 2) == 1))
    def _():
        scores_into(qi, bufs[1])
        consume(qi - 1, bufs[0], near)
        consume(qi, bufs[1], diag)

    @pl.when(jnp.logical_and(qi >= 2, lax.rem(qi, 2) == 0))
    def _():
        scores_into(qi - 1, bufs[1])
        consume(qi - 2, bufs[0], None)
        scores_into(qi, bufs[0])
        consume(qi - 1, bufs[1], near)
        consume(qi, bufs[0], diag)

    inv = 1.0 / acc_scr[HEAD:HEAD + 1, :]
    o = acc_scr[0:HEAD, 0:t] * inv[:, 0:t] - acc_scr[0:HEAD, t:2 * t] * (sc_ref[0] * inv[:, t:2 * t])
    ms = jnp.mean(o * o, axis=0, keepdims=True)
    o = (o * lax.rsqrt(ms + EPS)).T
    o_ref[0] = ((o * g_ref[0]) * sc_ref[1]).astype(BF16)


def _attention(proj, bias, scalars, subln_all, layer, t=ATTN_T):
    b, s, _ = proj.shape
    nq = s // t
    return pl.pallas_call(
        functools.partial(_attn_body, t=t, cw=ATTN_CW),
        grid=(b, N_HEADS, nq),
        in_specs=[pl.BlockSpec(memory_space=pltpu.SMEM),
                  pl.BlockSpec((1, t, HEAD), lambda bi, h, i: (bi, i, h)),
                  pl.BlockSpec((1, s, HEAD), lambda bi, h, i: (bi, 0, N_HEADS + h)),
                  pl.BlockSpec((1, s, HEAD), lambda bi, h, i: (bi, 0, 2 * N_HEADS + h)),
                  pl.BlockSpec((1, 2, t, 2 * t), lambda bi, h, i: (h, 0, 0, 0)),
                  pl.BlockSpec((1, 1, HEAD), lambda bi, h, i: (layer, 0, 0))],
        out_specs=pl.BlockSpec((1, t, HEAD), lambda bi, h, i: (bi, i, h)),
        out_shape=jax.ShapeDtypeStruct((b, s, N_HEADS * HEAD), BF16),
        scratch_shapes=[pltpu.VMEM((2 * t, HEAD), BF16),
                        pltpu.VMEM((HEAD + ONES_ROWS, s), BF16),
                        pltpu.VMEM((1, 2 * t), F32),
                        pltpu.VMEM((HEAD + ONES_ROWS, 2 * t), F32),
                        pltpu.VMEM((t, 2 * t), F32),
                        pltpu.VMEM((t, 2 * t), F32)],
        compiler_params=_cparams(3),
        name="diff_attention",
    )(scalars, proj, proj, proj, bias, subln_all)


def _hgrn_levels(c, sub):
    levels = []
    g = c // 2
    while g >= sub:
        levels.append(g)
        g //= 2
    return levels


def _hgrn_consts(l, c, sub):
    r = np.arange(HGRN_GROUP)[:, None]
    s = np.arange(HGRN_GROUP)[None, :]
    lcum = ((r // c == s // c) & (s <= r)).astype(np.float32)
    rc = np.arange(c)[:, None]
    sc = np.arange(c)[None, :]
    masks = [((rc // (2 * g) == sc // (2 * g)) & (rc % (2 * g) >= g) & (sc % (2 * g) < g)).astype(np.float32)
             for g in _hgrn_levels(c, sub)]
    half = c // 2
    rg = np.arange(HGRN_GROUP)[:, None]
    sg = np.arange(HGRN_GROUP)[None, :]
    top = (rg // c == sg // c) & (rg % c >= half) & (sg % c < half)
    near = (rg // half == sg // half) & (sg <= rg)
    fmask = np.stack([top, near]).astype(np.float32)
    return jnp.asarray(lcum, BF16), jnp.asarray(np.stack(masks), F32), jnp.asarray(fmask, F32)


def _hgrn_body(q_ref, f_ref, i_ref, g_ref, lb_ref, ng_ref, lcum_ref, mask_ref, fmask_ref, o_ref,
               st_scr, stb_scr, a_scr, oi_scr, *, l, c, sub):
    nchunk = l // c
    levels = _hgrn_levels(c, sub)

    @pl.when(pl.program_id(2) == 0)
    def _():
        st_scr[...] = jnp.zeros(st_scr.shape, F32)

    q = q_ref[0].astype(F32)
    fr = f_ref[0].astype(F32)
    vb = i_ref[0]
    lb = lb_ref[0]
    f = lb + (1.0 - lb) * jax.nn.sigmoid(fr)
    logf = jnp.log(jnp.maximum(f, jnp.finfo(F32).tiny))
    k = (1.0 - lb) * jax.nn.sigmoid(-fr)

    hi = logf.astype(BF16)
    lo = (logf - hi.astype(F32)).astype(BF16)
    hilo = jnp.concatenate([hi, lo], axis=1)
    lcum = lcum_ref[...]
    grp = lcum.shape[0]
    parts = []
    for gi in range(l // grp):
        both = jnp.dot(lcum, hilo[gi * grp:(gi + 1) * grp], preferred_element_type=F32)
        parts.append(both[:, 0:HEAD] + both[:, HEAD:2 * HEAD])
    bcum = jnp.concatenate(parts, axis=0)

    def rows_of(arr, group, row):
        a3 = arr.reshape(l // group, group, HEAD)
        return jnp.broadcast_to(a3[:, row:row + 1, :], a3.shape).reshape(l, HEAD)

    nt = (((1,), (1,)), ((), ()))
    tn = (((0,), (0,)), ((), ()))

    def level_operands(g):
        ref = rows_of(bcum, 2 * g, g - 1)
        return ((q * jnp.exp(jnp.minimum(bcum - ref, 0.0))).astype(BF16),
                (k * jnp.exp(jnp.minimum(ref - bcum, 0.0))).astype(BF16))

    q_top, k_top = level_operands(levels[0])

    def top_scores(sl):
        return mask_ref[0] * lax.dot_general(q_top[sl], k_top[sl], nt, preferred_element_type=F32)

    half = c // 2
    b_start = rows_of(bcum, half, 0) - rows_of(logf, half, 0)
    span = b_start - bcum
    bounded = jnp.max(span) <= HGRN_SAFE_SPAN

    @pl.when(bounded)
    def _():
        q_f = (q * jnp.exp(-span)).astype(BF16)
        k_f = (k * jnp.exp(span)).astype(BF16)
        grp = fmask_ref.shape[-1]
        for gi in range(l // grp):
            sl = slice(gi * grp, (gi + 1) * grp)
            a = fmask_ref[0] * lax.dot_general(q_top[sl], k_top[sl], nt, preferred_element_type=F32)
            a = a + fmask_ref[1] * lax.dot_general(q_f[sl], k_f[sl], nt, preferred_element_type=F32)
            a_scr[sl, :] = jnp.dot(a.astype(BF16), vb[sl], preferred_element_type=F32)

    @pl.when(jnp.logical_not(bounded))
    def _():
        lower = [level_operands(g) for g in levels[1:]]
        nb = l // sub
        b3 = bcum.reshape(nb, sub, HEAD)
        q3 = q.reshape(nb, sub, HEAD)
        k3 = k.reshape(nb, sub, HEAD)
        v3 = vb.astype(F32).reshape(nb, sub, HEAD)
        tloc = lax.broadcasted_iota(jnp.int32, (nb, sub, 1), 1)
        od = jnp.zeros((nb, sub, HEAD), F32)
        for si in range(sub):
            e = jnp.exp(b3 - b3[:, si:si + 1, :])
            pr = q3 * (k3[:, si:si + 1, :] * e)
            w = jnp.sum(pr, axis=-1, keepdims=True)
            w = jnp.where(tloc >= si, w, 0.0)
            od = od + w * v3[:, si:si + 1, :]
        od = od.reshape(l, HEAD)
        for ci in range(nchunk):
            sl = slice(ci * c, (ci + 1) * c)
            a = top_scores(sl)
            for li, (q_g, k_g) in enumerate(lower):
                a = a + mask_ref[li + 1] * lax.dot_general(q_g[sl], k_g[sl], nt, preferred_element_type=F32)
            a_scr[sl, :] = jnp.dot(a.astype(BF16), vb[sl], preferred_element_type=F32) + od[sl]

    b_end = rows_of(bcum, c, c - 1)
    q_in = (q * jnp.exp(bcum)).astype(BF16)
    k_out = (k * jnp.exp(b_end - bcum)).astype(BF16)
    dec = jnp.exp(b_end)
    kv = [lax.dot_general(vb[ci * c:(ci + 1) * c], k_out[ci * c:(ci + 1) * c], tn, preferred_element_type=F32)
          for ci in range(nchunk)]
    st = st_scr[...]
    for ci in range(nchunk):
        stb_scr[ci] = st.astype(BF16)
        st = st * dec[ci * c:ci * c + 1, :] + kv[ci]
    st_scr[...] = st
    for ci in range(nchunk):
        sl = slice(ci * c, (ci + 1) * c)
        oi_scr[sl, :] = lax.dot_general(q_in[sl], stb_scr[ci], nt, preferred_element_type=F32)

    o = a_scr[...] + oi_scr[...]
    ms = jnp.mean(o * o, axis=-1, keepdims=True)
    o = o * lax.rsqrt(ms + EPS) * ng_ref[0]
    gate = g_ref[0].astype(F32)
    o_ref[0] = (o * (gate * jax.nn.sigmoid(gate))).astype(BF16)


def _hgrn(proj, lower_bounds, norm_all, layer, l=HGRN_L, c=HGRN_C, sub=HGRN_SUB):
    b, s, _ = proj.shape
    lcum, masks, fmask = _hgrn_consts(l, c, sub)
    base = 3 * N_HEADS

    def col(kind):
        return pl.BlockSpec((1, l, HEAD), lambda bi, h, i: (bi, i, base + kind * N_HEADS + h))

    return pl.pallas_call(
        functools.partial(_hgrn_body, l=l, c=c, sub=sub),
        grid=(b, N_HEADS, s // l),
        in_specs=[col(0), col(1), col(2), col(3),
                  pl.BlockSpec((1, 1, HEAD), lambda bi, h, i: (layer, 0, h)),
                  pl.BlockSpec((1, 1, HEAD), lambda bi, h, i: (layer, 0, 0)),
                  pl.BlockSpec(lcum.shape, lambda bi, h, i: (0, 0)),
                  pl.BlockSpec(masks.shape, lambda bi, h, i: (0, 0, 0)),
                  pl.BlockSpec(fmask.shape, lambda bi, h, i: (0, 0, 0))],
        out_specs=pl.BlockSpec((1, l, HEAD), lambda bi, h, i: (bi, i, h)),
        out_shape=jax.ShapeDtypeStruct((b, s, N_HEADS * HEAD), BF16),
        scratch_shapes=[pltpu.VMEM((HEAD, HEAD), F32),
                        pltpu.VMEM((l // c, HEAD, HEAD), BF16),
                        pltpu.VMEM((l, HEAD), F32),
                        pltpu.VMEM((l, HEAD), F32)],
        compiler_params=_cparams(3),
        name="hgrn2",
    )(proj, proj, proj, proj, lower_bounds, norm_all, lcum, masks, fmask)


def _first_lane(cond, lane):
    return jnp.min(jnp.where(cond, lane, LANES), axis=-1, keepdims=True)


def _outproj_body(a_ref, hg_ref, w_ref, x_ref, mod_ref, g_ref, wrh_ref, wrl_ref, br_ref, xo_ref, h_ref, *, d):
    da = a_ref.shape[-1]
    mixed = jnp.dot(a_ref[0], w_ref[0, 0:da, :], preferred_element_type=F32)
    mixed = mixed + jnp.dot(hg_ref[0], w_ref[0, da:, :], preferred_element_type=F32)
    x = x_ref[0] + mod_ref[0, 2:3, :] * mixed
    xo_ref[0] = x
    h = _modulated_norm(x, g_ref[0], mod_ref[0, 4:5, :], mod_ref[0, 3:4, :])
    h_ref[0, :, 0:d] = h

    h_hi = h.astype(BF16)
    h_lo = (h - h_hi.astype(F32)).astype(BF16)
    logits = (jnp.dot(h_hi, wrh_ref[...], preferred_element_type=F32)
              + jnp.dot(h_lo, wrh_ref[...], preferred_element_type=F32)
              + jnp.dot(h_hi, wrl_ref[...], preferred_element_type=F32)) + br_ref[...]
    lane = lax.broadcasted_iota(jnp.int32, logits.shape, 1)
    valid = lane < N_EXPERTS
    logits = jnp.where(valid, logits, NEG)
    mx = jnp.max(logits, axis=-1, keepdims=True)
    ex = jnp.where(valid, jnp.exp(logits - mx), 0.0)
    probs = ex / jnp.sum(ex, axis=-1, keepdims=True)

    best = sel = v1 = v2 = i1 = i2 = None
    for gi in range(N_GROUPS):
        ing = (lane >= gi * GROUP_SIZE) & (lane < (gi + 1) * GROUP_SIZE)
        pg = jnp.where(ing, probs, -1.0)
        m1 = jnp.max(pg, axis=-1, keepdims=True)
        a1 = _first_lane(pg == m1, lane)
        pg2 = jnp.where(lane == a1, -1.0, pg)
        m2 = jnp.max(pg2, axis=-1, keepdims=True)
        a2 = _first_lane(pg2 == m2, lane)
        score = m1 + m2
        if gi == 0:
            best, sel, v1, v2, i1, i2 = score, jnp.zeros_like(a1), m1, m2, a1, a2
        else:
            better = score > best
            best = jnp.where(better, score, best)
            sel = jnp.where(better, gi, sel)
            v1 = jnp.where(better, m1, v1)
            v2 = jnp.where(better, m2, v2)
            i1 = jnp.where(better, a1, i1)
            i2 = jnp.where(better, a2, i2)
    tot = v1 + v2
    w1 = v1 / tot
    w2 = v2 / tot
    ext = jnp.where(lane == 0, w1, 0.0) + jnp.where(lane == 1, w2, 0.0)
    ext = ext + jnp.where(lane == 2, i1.astype(F32), 0.0) + jnp.where(lane == 3, i2.astype(F32), 0.0)
    h_ref[0, :, d:] = ext


def _outproj_router(attn_o, hgrn_o, w_out_all, x, mod, ffn_norm_all, wr_hi, wr_lo, br, layer, tm=512):
    b, s, d = x.shape
    da = attn_o.shape[-1]
    dmix = w_out_all.shape[1]
    tok = lambda bi, i: (bi, i, 0)
    return pl.pallas_call(
        functools.partial(_outproj_body, d=d),
        grid=(b, s // tm),
        in_specs=[pl.BlockSpec((1, tm, da), tok),
                  pl.BlockSpec((1, tm, dmix - da), tok),
                  pl.BlockSpec((1, dmix, d), lambda bi, i: (layer, 0, 0)),
                  pl.BlockSpec((1, tm, d), tok),
                  pl.BlockSpec((1, N_MOD, d), lambda bi, i: (bi, 0, 0)),
                  pl.BlockSpec((1, 1, d), lambda bi, i: (layer, 0, 0)),
                  pl.BlockSpec((d, LANES), lambda bi, i: (0, 0)),
                  pl.BlockSpec((d, LANES), lambda bi, i: (0, 0)),
                  pl.BlockSpec((1, LANES), lambda bi, i: (0, 0))],
        out_specs=[pl.BlockSpec((1, tm, d), tok),
                   pl.BlockSpec((1, tm, d + LANES), tok)],
        out_shape=[jax.ShapeDtypeStruct((b, s, d), F32),
                   jax.ShapeDtypeStruct((b, s, d + LANES), F32)],
        compiler_params=_cparams(2),
        name="outproj_router",
    )(attn_o, hgrn_o, w_out_all, x, mod, ffn_norm_all, wr_hi, wr_lo, br)


def _moe_body(te_ref, nv_ref, nu_ref, row_ref, h_hbm, wg_ref, wu_ref, wd_ref, y_hbm,
              xbuf, ybuf, wgb, wub, wdb, gsem, ssem, *, tm, d, t):
    i = pl.program_id(0)
    n_used = nu_ref[0]
    slot = lax.rem(i, 2)

    def gather_start(tile, sl):
        base = tile * tm

        def body(r, carry):
            tok = lax.shift_right_logical(row_ref[base + r], 1)
            pltpu.make_async_copy(h_hbm.at[pl.ds(tok, 1), pl.ds(0, d)], xbuf.at[sl, pl.ds(r, 1)], gsem.at[sl]).start()
            return carry

        lax.fori_loop(0, tm, body, 0, unroll=MOE_ISSUE_UNROLL)

    def gather_wait(sl):
        pltpu.make_async_copy(h_hbm.at[pl.ds(0, tm), pl.ds(0, d)], xbuf.at[sl], gsem.at[sl]).wait()

    def scatter_start(tile, sl, n):
        base = tile * tm

        def body(r, carry):
            code = row_ref[base + r]
            dst = lax.shift_right_logical(code, 1) + (code & 1) * t
            pltpu.make_async_copy(ybuf.at[sl, pl.ds(r, 1)], y_hbm.at[pl.ds(dst, 1)], ssem.at[sl]).start()
            return carry

        @pl.when(n == tm)
        def _():
            lax.fori_loop(0, tm, body, 0, unroll=MOE_ISSUE_UNROLL)

        @pl.when(n < tm)
        def _():
            lax.fori_loop(0, n, body, 0)

    def scatter_wait(sl, n):
        @pl.when(n == tm)
        def _():
            pltpu.make_async_copy(ybuf.at[sl], y_hbm.at[pl.ds(0, tm)], ssem.at[sl]).wait()

        @pl.when(n < tm)
        def _():
            def body(r, carry):
                pltpu.make_async_copy(ybuf.at[sl, pl.ds(r, 1)], y_hbm.at[pl.ds(r, 1)], ssem.at[sl]).wait()
                return carry

            lax.fori_loop(0, n, body, 0)

    @pl.when(i < n_used)
    def _():
        @pl.when(i == 0)
        def _():
            gather_start(0, 0)

        @pl.when(jnp.logical_or(i == 0, te_ref[i] != te_ref[jnp.maximum(i - 1, 0)]))
        def _():
            wgb[...] = wg_ref[0, 0].astype(BF16)
            wub[...] = wu_ref[0, 0].astype(BF16)
            wdb[...] = wd_ref[0, 0].astype(BF16)

        gather_wait(slot)

        @pl.when(i + 1 < n_used)
        def _():
            gather_start(i + 1, 1 - slot)

        x = xbuf[slot].astype(BF16)
        hg = jnp.dot(x, wgb[...], preferred_element_type=F32)
        hu = jnp.dot(x, wub[...], preferred_element_type=F32)
        act = (hg * jax.nn.sigmoid(hg)) * hu
        y = jnp.dot(act.astype(BF16), wdb[...], preferred_element_type=F32)

        @pl.when(i >= 2)
        def _():
            scatter_wait(slot, nv_ref[jnp.maximum(i - 2, 0)])

        ybuf[slot] = y
        scatter_start(i, slot, nv_ref[i])

        @pl.when(i == n_used - 1)
        def _():
            scatter_wait(slot, nv_ref[i])

            @pl.when(i >= 1)
            def _():
                scatter_wait(1 - slot, nv_ref[jnp.maximum(i - 1, 0)])


def _route(hext, d, tm):
    b, s, _ = hext.shape
    t = b * s
    n_rows = 2 * t
    n_tiles = n_rows // tm + N_EXPERTS
    eid = hext[:, :, d + 2:d + 4].reshape(n_rows).astype(jnp.int32)
    onehot = (eid[:, None] == jnp.arange(N_EXPERTS, dtype=jnp.int32)[None, :]).astype(jnp.int32)
    cnt = jnp.sum(onehot, axis=0)
    rank = jnp.sum(onehot * (jnp.cumsum(onehot, axis=0) - onehot), axis=1)
    ntile_e = (cnt + tm - 1) // tm
    tile_end_e = jnp.cumsum(ntile_e)
    tile_start_e = tile_end_e - ntile_e
    pos = (tile_start_e * tm)[eid] + rank
    code_of_row = jnp.zeros((n_tiles * tm,), jnp.int32).at[pos].set(jnp.arange(n_rows, dtype=jnp.int32))
    tile = jnp.arange(n_tiles, dtype=jnp.int32)
    tile_eid = jnp.minimum(jnp.sum((tile[:, None] >= tile_end_e[None, :]).astype(jnp.int32), axis=1), N_EXPERTS - 1)
    nvalid = jnp.clip(cnt[tile_eid] - (tile - tile_start_e[tile_eid]) * tm, 0, tm).astype(jnp.int32)
    n_used = tile_end_e[N_EXPERTS - 1:].astype(jnp.int32)
    return tile_eid.astype(jnp.int32), nvalid, n_used, code_of_row


def _moe(hext, wg_all, wu_all, wd_all, layer, tm=MOE_TM):
    b, s, dx = hext.shape
    d = dx - LANES
    t = b * s
    f = wg_all.shape[-1]
    tile_eid, nvalid, n_used, code_of_row = _route(hext, d, tm)
    n_tiles = tile_eid.shape[0]

    def wmap(i, te, nv, nu, row):
        return (layer, te[i], 0, 0)

    return pl.pallas_call(
        functools.partial(_moe_body, tm=tm, d=d, t=t),
        grid_spec=pltpu.PrefetchScalarGridSpec(
            num_scalar_prefetch=4,
            grid=(n_tiles,),
            in_specs=[pl.BlockSpec(memory_space=pl.ANY),
                      pl.BlockSpec((1, 1, d, f), wmap),
                      pl.BlockSpec((1, 1, d, f), wmap),
                      pl.BlockSpec((1, 1, f, d), wmap)],
            out_specs=pl.BlockSpec(memory_space=pl.ANY),
            scratch_shapes=[pltpu.VMEM((2, tm, d), F32),
                            pltpu.VMEM((2, tm, d), F32),
                            pltpu.VMEM((d, f), BF16),
                            pltpu.VMEM((d, f), BF16),
                            pltpu.VMEM((f, d), BF16),
                            pltpu.SemaphoreType.DMA((2,)),
                            pltpu.SemaphoreType.DMA((2,))],
        ),
        out_shape=jax.ShapeDtypeStruct((2 * t, d), F32),
        compiler_params=_cparams(1),
        name="moe_experts",
    )(tile_eid, nvalid, n_used, code_of_row, hext.reshape(t, dx), wg_all, wu_all, wd_all)


def _final_body(x_ref, y0_ref, y1_ref, gate_ref, mod_ref, g_ref, o_ref):
    x = x_ref[0] + mod_ref[0, 5:6, :] * _combine(gate_ref[0], y0_ref[...], y1_ref[...])
    ms = jnp.mean(x * x, axis=-1, keepdims=True)
    o_ref[0] = x * lax.rsqrt(ms + EPS) * g_ref[...]


def _final(x, y, hext, mod, g, tm=512):
    b, s, d = x.shape
    nt = s // tm
    return pl.pallas_call(
        _final_body,
        grid=(b, nt),
        in_specs=[pl.BlockSpec((1, tm, d), lambda bi, i: (bi, i, 0)),
                  pl.BlockSpec((tm, d), lambda bi, i: (bi * nt + i, 0)),
                  pl.BlockSpec((tm, d), lambda bi, i: (b * nt + bi * nt + i, 0)),
                  pl.BlockSpec((1, tm, LANES), lambda bi, i: (bi, i, d // LANES)),
                  pl.BlockSpec((1, N_MOD, d), lambda bi, i: (bi, 0, 0)),
                  pl.BlockSpec((1, d), lambda bi, i: (0, 0))],
        out_specs=pl.BlockSpec((1, tm, d), lambda bi, i: (bi, i, 0)),
        out_shape=jax.ShapeDtypeStruct(x.shape, F32),
        compiler_params=_cparams(2),
        name="final_norm",
    )(x, y, y, hext, mod, g.reshape(1, d))


def kernel(x, c, w_in, w_out, attn_norm, ffn_norm, w_ada, b_ada, lambda_q1, lambda_k1, lambda_q2, lambda_k2,
           diff_subln, hgrn_lb, hgrn_norm, rel_bias, w_router, b_router, w_gate, w_up, w_down, final_norm):
    b, s, d = x.shape
    depth = w_in.shape[0]

    w_in_b = w_in.astype(BF16)
    w_out_b = w_out.astype(BF16)
    wr = jnp.pad(w_router.astype(F32), ((0, 0), (0, LANES - N_EXPERTS)))
    wr_hi = wr.astype(BF16)
    wr_lo = (wr - wr_hi.astype(F32)).astype(BF16)
    br = jnp.pad(b_router.astype(F32), (0, LANES - N_EXPERTS)).reshape(1, LANES)
    lb_soft = jax.nn.softmax(hgrn_lb.astype(F32), axis=0)
    lower_bounds = jnp.maximum(jnp.cumsum(lb_soft, axis=0) - lb_soft[0:1], 0.0).reshape(depth, 1, -1)
    attn_norm3 = attn_norm.astype(F32).reshape(depth, 1, d)
    ffn_norm3 = ffn_norm.astype(F32).reshape(depth, 1, d)
    subln3 = diff_subln.astype(F32).reshape(depth, 1, HEAD)
    hnorm3 = hgrn_norm.astype(F32).reshape(depth, 1, HEAD)

    c8 = jnp.pad(c.astype(F32), ((0, 8 - b), (0, 0)))
    mods = _ada(c8, w_ada, b_ada)
    bias = _bias_tiles(rel_bias, ATTN_T)

    y = None
    hext = None
    mod_prev = None
    for layer in range(depth):
        mod = mods[layer, :b].reshape(b, N_MOD, d)
        lambda_init = 0.8 - 0.6 * math.exp(-0.3 * layer)
        lam = (jnp.exp(jnp.sum(lambda_q1[layer].astype(F32) * lambda_k1[layer].astype(F32)))
               - jnp.exp(jnp.sum(lambda_q2[layer].astype(F32) * lambda_k2[layer].astype(F32)))
               + lambda_init)
        scalars = jnp.stack([lam, jnp.asarray(1.0 - lambda_init, F32)]).astype(F32)

        x, proj = _inproj(x, y, hext, mod_prev, mod, attn_norm3, w_in_b, layer)
        attn_o = _attention(proj, bias, scalars, subln3, layer)
        hgrn_o = _hgrn(proj, lower_bounds, hnorm3, layer)
        x, hext = _outproj_router(attn_o, hgrn_o, w_out_b, x, mod, ffn_norm3, wr_hi, wr_lo, br, layer)
        y = _moe(hext, w_gate, w_up, w_down, layer)
        mod_prev = mod
    return _final(x, y, hext, mod_prev, final_norm.astype(F32))
```

```python
import functools
import math

import numpy as np
import jax
import jax.numpy as jnp
from jax import lax
from jax.experimental import pallas as pl
from jax.experimental.pallas import tpu as pltpu

F32 = jnp.float32
BF16 = jnp.bfloat16
EPS = 1e-6

LANES = 128
HEAD = 128
QK = 64
N_HEADS = 8
NUM_BUCKETS = 32
MAX_DISTANCE = 128
N_EXPERTS = 16
N_GROUPS = 4
GROUP_SIZE = N_EXPERTS // N_GROUPS
N_MOD = 6
NEG = -1e30

ATTN_T = 512
ATTN_CW = 512
ONES_ROWS = 16
LOG2E = math.log2(math.e)
HGRN_L = 512
HGRN_C = 64
HGRN_SUB = 8
HGRN_GROUP = 256
HGRN_SAFE_SPAN = 80.0
NORM_ROWS = 256
INPROJ_TM = 1024
INPROJ_TM_RES = 512
MOE_TM = 512
MOE_ISSUE_UNROLL = MOE_TM
VMEM_LIMIT = 56 * 1024 * 1024


def _cparams(n_axes, vmem_limit=VMEM_LIMIT):
    return pltpu.CompilerParams(dimension_semantics=("arbitrary",) * n_axes, vmem_limit_bytes=vmem_limit)


def _ada_body(c_ref, w_ref, b_ref, o_ref):
    c = c_ref[...]
    ca = (c * jax.nn.sigmoid(c)).astype(BF16)
    o_ref[0] = jnp.dot(ca, w_ref[0].astype(BF16), preferred_element_type=F32) + b_ref[0]


def _ada(c8, w_ada, b_ada, tn=1024):
    depth, d, n = w_ada.shape
    return pl.pallas_call(
        _ada_body,
        grid=(depth, n // tn),
        in_specs=[pl.BlockSpec((8, d), lambda l, j: (0, 0)),
                  pl.BlockSpec((1, d, tn), lambda l, j: (l, 0, j)),
                  pl.BlockSpec((1, 1, tn), lambda l, j: (l, 0, j))],
        out_specs=pl.BlockSpec((1, 8, tn), lambda l, j: (l, 0, j)),
        out_shape=jax.ShapeDtypeStruct((depth, 8, n), F32),
        compiler_params=_cparams(2),
        name="ada_mod",
    )(c8, w_ada, b_ada.reshape(depth, 1, n))


def _modulated_norm(x, g, scale, shift):
    ms = jnp.mean(x * x, axis=-1, keepdims=True)
    return (x * lax.rsqrt(ms + EPS) * g) * (1.0 + scale) + shift


def _combine(gates, y0, y1):
    return gates[:, 0:1] * y0 + gates[:, 1:2] * y1


def _inproj_body(*refs, has_res):
    if has_res:
        x_ref, y0_ref, y1_ref, gate_ref, gp_ref, mod_ref, g_ref, w_ref, xo_ref, p_ref, h_scr = refs
    else:
        x_ref, mod_ref, g_ref, w_ref, p_ref, h_scr = refs

    @pl.when(pl.program_id(2) == 0)
    def _():
        tm = h_scr.shape[0]
        for r0 in range(0, tm, NORM_ROWS):
            rows = slice(r0, r0 + NORM_ROWS)
            x = x_ref[0, rows, :]
            if has_res:
                x = x + gp_ref[0, 5:6, :] * _combine(gate_ref[0, rows, :], y0_ref[rows, :], y1_ref[rows, :])
                xo_ref[0, rows, :] = x
            h = _modulated_norm(x, g_ref[0], mod_ref[0, 1:2, :], mod_ref[0, 0:1, :])
            h_scr[rows, :] = h.astype(BF16)

    p_ref[0] = jnp.dot(h_scr[...], w_ref[0], preferred_element_type=F32).astype(BF16)


def _inproj(x, y, hext, mod_prev, mod, g_all, w_all, layer, tn=1024):
    b, s, d = x.shape
    n = w_all.shape[-1]
    has_res = y is not None
    tm = INPROJ_TM_RES if has_res else INPROJ_TM
    nt = s // tm
    xspec = pl.BlockSpec((1, tm, d), lambda bi, i, j: (bi, i, 0))
    modspec = pl.BlockSpec((1, N_MOD, d), lambda bi, i, j: (bi, 0, 0))
    once = dict(pipeline_mode=pl.Buffered(1))
    in_specs = [pl.BlockSpec((1, tm, d), lambda bi, i, j: (bi, i, 0), **once)]
    args = [x]
    if has_res:
        in_specs += [pl.BlockSpec((tm, d), lambda bi, i, j: (bi * nt + i, 0), **once),
                     pl.BlockSpec((tm, d), lambda bi, i, j: (b * nt + bi * nt + i, 0), **once),
                     pl.BlockSpec((1, tm, LANES), lambda bi, i, j: (bi, i, d // LANES), **once),
                     modspec]
        args += [y, y, hext, mod_prev]
    in_specs += [modspec,
                 pl.BlockSpec((1, 1, d), lambda bi, i, j: (layer, 0, 0)),
                 pl.BlockSpec((1, d, tn), lambda bi, i, j: (layer, 0, j))]
    args += [mod, g_all, w_all]
    pspec = pl.BlockSpec((1, tm, tn), lambda bi, i, j: (bi, i, j))
    pshape = jax.ShapeDtypeStruct((b, s, n), BF16)
    if has_res:
        out_specs, out_shape = [xspec, pspec], [jax.ShapeDtypeStruct(x.shape, F32), pshape]
    else:
        out_specs, out_shape = pspec, pshape
    out = pl.pallas_call(
        functools.partial(_inproj_body, has_res=has_res),
        grid=(b, nt, n // tn),
        in_specs=in_specs, out_specs=out_specs, out_shape=out_shape,
        scratch_shapes=[pltpu.VMEM((tm, d), BF16)],
        compiler_params=_cparams(3),
        name="inproj",
    )(*args)
    return (out[0], out[1]) if has_res else (x, out)


def _t5_bucket_np(n):
    max_exact = NUM_BUCKETS // 2
    nf = np.maximum(n, 1).astype(np.float32)
    large = max_exact + (np.log(nf / np.float32(max_exact)) / np.float32(math.log(MAX_DISTANCE / max_exact))
                         * np.float32(NUM_BUCKETS - max_exact)).astype(np.int32)
    large = np.minimum(large, NUM_BUCKETS - 1)
    return np.where(n < max_exact, n, large).astype(np.int32)


def _bias_body(tab_ref, bd_ref, bu_ref, o_ref):
    m = pl.program_id(0) * 2 + pl.program_id(1)
    bd = bd_ref[...]
    bu = bu_ref[...]
    far = tab_ref[NUM_BUCKETS - 1, m]
    accd = jnp.zeros(bd.shape, F32)
    accu = jnp.zeros(bu.shape, F32)
    for bkt in range(NUM_BUCKETS - 1):
        val = (tab_ref[bkt, m] - far) * LOG2E
        accd = jnp.where(bd == bkt, val, accd)
        accu = jnp.where(bu == bkt, val, accu)
    o_ref[0, 0] = jnp.where(bd < 0, NEG, accd)
    o_ref[0, 1] = accu


def _bias_tiles(rel_bias, t):
    key = np.arange(t)[:, None]
    qry = np.arange(t)[None, :]
    bd = np.where(qry >= key, _t5_bucket_np(np.maximum(qry - key, 0)), -1).astype(np.int32)
    bu = _t5_bucket_np(t + qry - key)
    assert MAX_DISTANCE <= t + 1, "tiles two or more to the left must lie in the far bucket"
    return pl.pallas_call(
        _bias_body,
        grid=(N_HEADS, 2),
        in_specs=[pl.BlockSpec(memory_space=pltpu.SMEM),
                  pl.BlockSpec((t, t), lambda h, m: (0, 0)),
                  pl.BlockSpec((t, t), lambda h, m: (0, 0))],
        out_specs=pl.BlockSpec((1, 2, t, t), lambda h, m: (h, 0, 0, m)),
        out_shape=jax.ShapeDtypeStruct((N_HEADS, 2, t, 2 * t), F32),
        compiler_params=_cparams(2),
        name="bias_tiles",
    )(rel_bias.astype(F32), jnp.asarray(bd), jnp.asarray(bu))


def _attn_body(sc_ref, q_ref, k_ref, v_ref, bias_ref, g_ref, o_ref, qq_scr, vt_scr, m_scr, acc_scr, s0_scr, s1_scr,
               *, t, cw):
    qi = pl.program_id(2)
    s_len = k_ref.shape[1]

    @pl.when(qi == 0)
    def _():
        for ci in range(s_len // t):
            vt_scr[0:HEAD, ci * t:(ci + 1) * t] = v_ref[0, ci * t:(ci + 1) * t, :].T
        row = lax.broadcasted_iota(jnp.int32, (ONES_ROWS, s_len), 0)
        vt_scr[HEAD:HEAD + ONES_ROWS, :] = jnp.where(row == 0, 1.0, 0.0).astype(BF16)

    q = (q_ref[0].astype(F32) * (QK ** -0.5 * LOG2E)).astype(BF16)
    lane = lax.broadcasted_iota(jnp.int32, q.shape, 1)
    zero = jnp.zeros_like(q)
    qq_scr[0:t, :] = jnp.where(lane < QK, q, zero)
    qq_scr[t:2 * t, :] = jnp.where(lane >= QK, q, zero)
    m_scr[...] = jnp.full(m_scr.shape, NEG, F32)
    acc_scr[...] = jnp.zeros(acc_scr.shape, F32)

    tiles = [slice(ct * cw, (ct + 1) * cw) for ct in range(2 * t // cw)]

    bufs = (s0_scr, s1_scr)

    nt_dims = (((1,), (1,)), ((), ()))
    hk = t // 2
    diag_parts = [(k0, k1, slice(mi * t + k0, (mi + 1) * t)) for mi in range(2) for k0, k1 in ((0, hk), (hk, t))]

    def scores_into(j, buf):
        kb = k_ref[0, pl.ds(pl.multiple_of(j * t, t), t), :]
        for cs in tiles:
            buf[:, cs] = lax.dot_general(kb, qq_scr[cs, :], nt_dims, preferred_element_type=F32)

    def diag_scores_into(j, buf):
        for k0, k1, cs in diag_parts:
            kb = k_ref[0, pl.ds(pl.multiple_of(j * t + k0, hk), k1 - k0), :]
            buf[k0:k1, cs] = lax.dot_general(kb, qq_scr[cs, :], nt_dims, preferred_element_type=F32)

    def consume_part(j, buf, k0, k1, cs, bias_of):
        vt = vt_scr[:, pl.ds(pl.multiple_of(j * t + k0, hk), k1 - k0)]
        s = buf[k0:k1, cs]
        if bias_of is not None:
            s = s + bias_of(k0, k1, cs)
        m_prev = m_scr[:, cs]
        m_new = jnp.maximum(m_prev, jnp.max(s, axis=0, keepdims=True))
        alpha = jnp.exp2(m_prev - m_new)
        p = jnp.exp2(s - m_new)
        acc_scr[:, cs] = alpha * acc_scr[:, cs] + jnp.dot(vt, p.astype(BF16), preferred_element_type=F32)
        m_scr[:, cs] = m_new

    def consume(j, buf, bias_of):
        for cs in tiles:
            consume_part(j, buf, 0, t, cs, bias_of)

    def diag_consume(j, buf):
        for k0, k1, cs in diag_parts:
            consume_part(j, buf, k0, k1, cs, lambda a, b, c: bias_ref[0, 0, a:b, c])

    near = lambda a, b, c: bias_ref[0, 1, a:b, c]

    @pl.when(qi == 0)
    def _():
        diag_scores_into(0, bufs[0])

    @pl.when(qi > 0)
    def _():
        scores_into(0, bufs[0])

    n_far = jnp.maximum(qi - 1, 0)

    def far_pair(pi, carry):
        j = 2 * pi
        scores_into(j + 1, bufs[1])
        consume(j, bufs[0], None)
        scores_into(j + 2, bufs[0])
        consume(j + 1, bufs[1], None)
        return carry

    lax.fori_loop(0, n_far // 2, far_pair, 0)

    @pl.when(qi == 0)
    def _():
        diag_consume(0, bufs[0])

    @pl.when(jnp.logical_and(qi >= 1, lax.rem(qi, 2) == 1))
    def _():
        diag_scores_into(qi, bufs[1])
        consume(qi - 1, bufs[0], near)
        diag_consume(qi, bufs[1])

    @pl.when(jnp.logical_and(qi >= 2, lax.rem(qi, 2) == 0))
    def _():
        scores_into(qi - 1, bufs[1])
        consume(qi - 2, bufs[0], None)
        diag_scores_into(qi, bufs[0])
        consume(qi - 1, bufs[1], near)
        diag_consume(qi, bufs[0])

    inv = 1.0 / acc_scr[HEAD:HEAD + 1, :]
    o = acc_scr[0:HEAD, 0:t] * inv[:, 0:t] - acc_scr[0:HEAD, t:2 * t] * (sc_ref[0] * inv[:, t:2 * t])
    ms = jnp.mean(o * o, axis=0, keepdims=True)
    o = (o * lax.rsqrt(ms + EPS)).T
    o_ref[0] = ((o * g_ref[0]) * sc_ref[1]).astype(BF16)


def _attention(proj, bias, scalars, subln_all, layer, t=ATTN_T):
    b, s, _ = proj.shape
    nq = s // t
    return pl.pallas_call(
        functools.partial(_attn_body, t=t, cw=ATTN_CW),
        grid=(b, N_HEADS, nq),
        in_specs=[pl.BlockSpec(memory_space=pltpu.SMEM),
                  pl.BlockSpec((1, t, HEAD), lambda bi, h, i: (bi, i, h)),
                  pl.BlockSpec((1, s, HEAD), lambda bi, h, i: (bi, 0, N_HEADS + h)),
                  pl.BlockSpec((1, s, HEAD), lambda bi, h, i: (bi, 0, 2 * N_HEADS + h)),
                  pl.BlockSpec((1, 2, t, 2 * t), lambda bi, h, i: (h, 0, 0, 0)),
                  pl.BlockSpec((1, 1, HEAD), lambda bi, h, i: (layer, 0, 0))],
        out_specs=pl.BlockSpec((1, t, HEAD), lambda bi, h, i: (bi, i, h)),
        out_shape=jax.ShapeDtypeStruct((b, s, N_HEADS * HEAD), BF16),
        scratch_shapes=[pltpu.VMEM((2 * t, HEAD), BF16),
                        pltpu.VMEM((HEAD + ONES_ROWS, s), BF16),
                        pltpu.VMEM((1, 2 * t), F32),
                        pltpu.VMEM((HEAD + ONES_ROWS, 2 * t), F32),
                        pltpu.VMEM((t, 2 * t), F32),
                        pltpu.VMEM((t, 2 * t), F32)],
        compiler_params=_cparams(3),
        name="diff_attention",
    )(scalars, proj, proj, proj, bias, subln_all)


def _hgrn_levels(c, sub):
    levels = []
    g = c // 2
    while g >= sub:
        levels.append(g)
        g //= 2
    return levels


def _hgrn_consts(l, c, sub):
    r = np.arange(HGRN_GROUP)[:, None]
    s = np.arange(HGRN_GROUP)[None, :]
    lcum = ((r // c == s // c) & (s <= r)).astype(np.float32)
    rc = np.arange(c)[:, None]
    sc = np.arange(c)[None, :]
    masks = [((rc // (2 * g) == sc // (2 * g)) & (rc % (2 * g) >= g) & (sc % (2 * g) < g)).astype(np.float32)
             for g in _hgrn_levels(c, sub)]
    half = c // 2
    rg = np.arange(HGRN_GROUP)[:, None]
    sg = np.arange(HGRN_GROUP)[None, :]
    top = (rg // c == sg // c) & (rg % c >= half) & (sg % c < half)
    near = (rg // half == sg // half) & (sg <= rg)
    fmask = np.stack([top, near]).astype(np.float32)
    return jnp.asarray(lcum, BF16), jnp.asarray(np.stack(masks), F32), jnp.asarray(fmask, F32)


def _hgrn_body(q_ref, f_ref, i_ref, g_ref, lb_ref, ng_ref, lcum_ref, mask_ref, fmask_ref, o_ref,
               st_scr, stb_scr, a_scr, oi_scr, *, l, c, sub):
    nchunk = l // c
    levels = _hgrn_levels(c, sub)

    @pl.when(pl.program_id(2) == 0)
    def _():
        st_scr[...] = jnp.zeros(st_scr.shape, F32)

    q = q_ref[0].astype(F32)
    fr = f_ref[0].astype(F32)
    vb = i_ref[0]
    lb = lb_ref[0]
    f = lb + (1.0 - lb) * jax.nn.sigmoid(fr)
    logf = jnp.log(jnp.maximum(f, jnp.finfo(F32).tiny))
    k = (1.0 - lb) * jax.nn.sigmoid(-fr)

    hi = logf.astype(BF16)
    lo = (logf - hi.astype(F32)).astype(BF16)
    hilo = jnp.concatenate([hi, lo], axis=1)
    lcum = lcum_ref[...]
    grp = lcum.shape[0]
    parts = []
    for gi in range(l // grp):
        both = jnp.dot(lcum, hilo[gi * grp:(gi + 1) * grp], preferred_element_type=F32)
        parts.append(both[:, 0:HEAD] + both[:, HEAD:2 * HEAD])
    bcum = jnp.concatenate(parts, axis=0)

    def rows_of(arr, group, row):
        a3 = arr.reshape(l // group, group, HEAD)
        return jnp.broadcast_to(a3[:, row:row + 1, :], a3.shape).reshape(l, HEAD)

    nt = (((1,), (1,)), ((), ()))
    tn = (((0,), (0,)), ((), ()))

    def level_operands(g):
        ref = rows_of(bcum, 2 * g, g - 1)
        return ((q * jnp.exp(jnp.minimum(bcum - ref, 0.0))).astype(BF16),
                (k * jnp.exp(jnp.minimum(ref - bcum, 0.0))).astype(BF16))

    q_top, k_top = level_operands(levels[0])

    def top_scores(sl):
        return mask_ref[0] * lax.dot_general(q_top[sl], k_top[sl], nt, preferred_element_type=F32)

    half = c // 2
    b_start = rows_of(bcum, half, 0) - rows_of(logf, half, 0)
    span = b_start - bcum
    bounded = jnp.max(span) <= HGRN_SAFE_SPAN

    @pl.when(bounded)
    def _():
        q_f = (q * jnp.exp(-span)).astype(BF16)
        k_f = (k * jnp.exp(span)).astype(BF16)
        grp = fmask_ref.shape[-1]
        for gi in range(l // grp):
            sl = slice(gi * grp, (gi + 1) * grp)
            a = fmask_ref[0] * lax.dot_general(q_top[sl], k_top[sl], nt, preferred_element_type=F32)
            a = a + fmask_ref[1] * lax.dot_general(q_f[sl], k_f[sl], nt, preferred_element_type=F32)
            a_scr[sl, :] = jnp.dot(a.astype(BF16), vb[sl], preferred_element_type=F32)

    @pl.when(jnp.logical_not(bounded))
    def _():
        lower = [level_operands(g) for g in levels[1:]]
        nb = l // sub
        b3 = bcum.reshape(nb, sub, HEAD)
        q3 = q.reshape(nb, sub, HEAD)
        k3 = k.reshape(nb, sub, HEAD)
        v3 = vb.astype(F32).reshape(nb, sub, HEAD)
        tloc = lax.broadcasted_iota(jnp.int32, (nb, sub, 1), 1)
        od = jnp.zeros((nb, sub, HEAD), F32)
        for si in range(sub):
            e = jnp.exp(b3 - b3[:, si:si + 1, :])
            pr = q3 * (k3[:, si:si + 1, :] * e)
            w = jnp.sum(pr, axis=-1, keepdims=True)
            w = jnp.where(tloc >= si, w, 0.0)
            od = od + w * v3[:, si:si + 1, :]
        od = od.reshape(l, HEAD)
        for ci in range(nchunk):
            sl = slice(ci * c, (ci + 1) * c)
            a = top_scores(sl)
            for li, (q_g, k_g) in enumerate(lower):
                a = a + mask_ref[li + 1] * lax.dot_general(q_g[sl], k_g[sl], nt, preferred_element_type=F32)
            a_scr[sl, :] = jnp.dot(a.astype(BF16), vb[sl], preferred_element_type=F32) + od[sl]

    b_end = rows_of(bcum, c, c - 1)
    q_in = (q * jnp.exp(bcum)).astype(BF16)
    k_out = (k * jnp.exp(b_end - bcum)).astype(BF16)
    dec = jnp.exp(b_end)
    kv = [lax.dot_general(vb[ci * c:(ci + 1) * c], k_out[ci * c:(ci + 1) * c], tn, preferred_element_type=F32)
          for ci in range(nchunk)]
    st = st_scr[...]
    for ci in range(nchunk):
        stb_scr[ci] = st.astype(BF16)
        st = st * dec[ci * c:ci * c + 1, :] + kv[ci]
    st_scr[...] = st
    for ci in range(nchunk):
        sl = slice(ci * c, (ci + 1) * c)
        oi_scr[sl, :] = lax.dot_general(q_in[sl], stb_scr[ci], nt, preferred_element_type=F32)

    o = a_scr[...] + oi_scr[...]
    ms = jnp.mean(o * o, axis=-1, keepdims=True)
    o = o * lax.rsqrt(ms + EPS) * ng_ref[0]
    gate = g_ref[0].astype(F32)
    o_ref[0] = (o * (gate * jax.nn.sigmoid(gate))).astype(BF16)


def _hgrn(proj, lower_bounds, norm_all, layer, l=HGRN_L, c=HGRN_C, sub=HGRN_SUB):
    b, s, _ = proj.shape
    lcum, masks, fmask = _hgrn_consts(l, c, sub)
    base = 3 * N_HEADS

    def col(kind):
        return pl.BlockSpec((1, l, HEAD), lambda bi, h, i: (bi, i, base + kind * N_HEADS + h))

    return pl.pallas_call(
        functools.partial(_hgrn_body, l=l, c=c, sub=sub),
        grid=(b, N_HEADS, s // l),
        in_specs=[col(0), col(1), col(2), col(3),
                  pl.BlockSpec((1, 1, HEAD), lambda bi, h, i: (layer, 0, h)),
                  pl.BlockSpec((1, 1, HEAD), lambda bi, h, i: (layer, 0, 0)),
                  pl.BlockSpec(lcum.shape, lambda bi, h, i: (0, 0)),
                  pl.BlockSpec(masks.shape, lambda bi, h, i: (0, 0, 0)),
                  pl.BlockSpec(fmask.shape, lambda bi, h, i: (0, 0, 0))],
        out_specs=pl.BlockSpec((1, l, HEAD), lambda bi, h, i: (bi, i, h)),
        out_shape=jax.ShapeDtypeStruct((b, s, N_HEADS * HEAD), BF16),
        scratch_shapes=[pltpu.VMEM((HEAD, HEAD), F32),
                        pltpu.VMEM((l // c, HEAD, HEAD), BF16),
                        pltpu.VMEM((l, HEAD), F32),
                        pltpu.VMEM((l, HEAD), F32)],
        compiler_params=_cparams(3),
        name="hgrn2",
    )(proj, proj, proj, proj, lower_bounds, norm_all, lcum, masks, fmask)


def _first_lane(cond, lane):
    return jnp.min(jnp.where(cond, lane, LANES), axis=-1, keepdims=True)


def _outproj_body(a_ref, hg_ref, w_ref, x_ref, mod_ref, g_ref, wrh_ref, wrl_ref, br_ref, xo_ref, h_ref, *, d):
    da = a_ref.shape[-1]
    mixed = jnp.dot(a_ref[0], w_ref[0, 0:da, :], preferred_element_type=F32)
    mixed = mixed + jnp.dot(hg_ref[0], w_ref[0, da:, :], preferred_element_type=F32)
    x = x_ref[0] + mod_ref[0, 2:3, :] * mixed
    xo_ref[0] = x
    h = _modulated_norm(x, g_ref[0], mod_ref[0, 4:5, :], mod_ref[0, 3:4, :])
    h_ref[0, :, 0:d] = h

    h_hi = h.astype(BF16)
    h_lo = (h - h_hi.astype(F32)).astype(BF16)
    logits = (jnp.dot(h_hi, wrh_ref[...], preferred_element_type=F32)
              + jnp.dot(h_lo, wrh_ref[...], preferred_element_type=F32)
              + jnp.dot(h_hi, wrl_ref[...], preferred_element_type=F32)) + br_ref[...]
    lane = lax.broadcasted_iota(jnp.int32, logits.shape, 1)
    valid = lane < N_EXPERTS
    logits = jnp.where(valid, logits, NEG)
    mx = jnp.max(logits, axis=-1, keepdims=True)
    ex = jnp.where(valid, jnp.exp(logits - mx), 0.0)
    probs = ex / jnp.sum(ex, axis=-1, keepdims=True)

    best = sel = v1 = v2 = i1 = i2 = None
    for gi in range(N_GROUPS):
        ing = (lane >= gi * GROUP_SIZE) & (lane < (gi + 1) * GROUP_SIZE)
        pg = jnp.where(ing, probs, -1.0)
        m1 = jnp.max(pg, axis=-1, keepdims=True)
        a1 = _first_lane(pg == m1, lane)
        pg2 = jnp.where(lane == a1, -1.0, pg)
        m2 = jnp.max(pg2, axis=-1, keepdims=True)
        a2 = _first_lane(pg2 == m2, lane)
        score = m1 + m2
        if gi == 0:
            best, sel, v1, v2, i1, i2 = score, jnp.zeros_like(a1), m1, m2, a1, a2
        else:
            better = score > best
            best = jnp.where(better, score, best)
            sel = jnp.where(better, gi, sel)
            v1 = jnp.where(better, m1, v1)
            v2 = jnp.where(better, m2, v2)
            i1 = jnp.where(better, a1, i1)
            i2 = jnp.where(better, a2, i2)
    tot = v1 + v2
    w1 = v1 / tot
    w2 = v2 / tot
    ext = jnp.where(lane == 0, w1, 0.0) + jnp.where(lane == 1, w2, 0.0)
    ext = ext + jnp.where(lane == 2, i1.astype(F32), 0.0) + jnp.where(lane == 3, i2.astype(F32), 0.0)
    h_ref[0, :, d:] = ext


def _outproj_router(attn_o, hgrn_o, w_out_all, x, mod, ffn_norm_all, wr_hi, wr_lo, br, layer, tm=512):
    b, s, d = x.shape
    da = attn_o.shape[-1]
    dmix = w_out_all.shape[1]
    tok = lambda bi, i: (bi, i, 0)
    return pl.pallas_call(
        functools.partial(_outproj_body, d=d),
        grid=(b, s // tm),
        in_specs=[pl.BlockSpec((1, tm, da), tok),
                  pl.BlockSpec((1, tm, dmix - da), tok),
                  pl.BlockSpec((1, dmix, d), lambda bi, i: (layer, 0, 0)),
                  pl.BlockSpec((1, tm, d), tok),
                  pl.BlockSpec((1, N_MOD, d), lambda bi, i: (bi, 0, 0)),
                  pl.BlockSpec((1, 1, d), lambda bi, i: (layer, 0, 0)),
                  pl.BlockSpec((d, LANES), lambda bi, i: (0, 0)),
                  pl.BlockSpec((d, LANES), lambda bi, i: (0, 0)),
                  pl.BlockSpec((1, LANES), lambda bi, i: (0, 0))],
        out_specs=[pl.BlockSpec((1, tm, d), tok),
                   pl.BlockSpec((1, tm, d + LANES), tok)],
        out_shape=[jax.ShapeDtypeStruct((b, s, d), F32),
                   jax.ShapeDtypeStruct((b, s, d + LANES), F32)],
        compiler_params=_cparams(2),
        name="outproj_router",
    )(attn_o, hgrn_o, w_out_all, x, mod, ffn_norm_all, wr_hi, wr_lo, br)


def _moe_body(te_ref, nv_ref, nu_ref, row_ref, h_hbm, wg_ref, wu_ref, wd_ref, y_hbm,
              xbuf, ybuf, wgb, wub, wdb, gsem, ssem, *, tm, d, t):
    i = pl.program_id(0)
    n_used = nu_ref[0]
    slot = lax.rem(i, 2)

    def gather_start(tile, sl):
        base = tile * tm

        def body(r, carry):
            tok = lax.shift_right_logical(row_ref[base + r], 1)
            pltpu.make_async_copy(h_hbm.at[pl.ds(tok, 1), pl.ds(0, d)], xbuf.at[sl, pl.ds(r, 1)], gsem.at[sl]).start()
            return carry

        lax.fori_loop(0, tm, body, 0, unroll=MOE_ISSUE_UNROLL)

    def gather_wait(sl):
        pltpu.make_async_copy(h_hbm.at[pl.ds(0, tm), pl.ds(0, d)], xbuf.at[sl], gsem.at[sl]).wait()

    def scatter_start(tile, sl, n):
        base = tile * tm

        def body(r, carry):
            code = row_ref[base + r]
            dst = lax.shift_right_logical(code, 1) + (code & 1) * t
            pltpu.make_async_copy(ybuf.at[sl, pl.ds(r, 1)], y_hbm.at[pl.ds(dst, 1)], ssem.at[sl]).start()
            return carry

        @pl.when(n == tm)
        def _():
            lax.fori_loop(0, tm, body, 0, unroll=MOE_ISSUE_UNROLL)

        @pl.when(n < tm)
        def _():
            lax.fori_loop(0, n, body, 0)

    def scatter_wait(sl, n):
        @pl.when(n == tm)
        def _():
            pltpu.make_async_copy(ybuf.at[sl], y_hbm.at[pl.ds(0, tm)], ssem.at[sl]).wait()

        @pl.when(n < tm)
        def _():
            def body(r, carry):
                pltpu.make_async_copy(ybuf.at[sl, pl.ds(r, 1)], y_hbm.at[pl.ds(r, 1)], ssem.at[sl]).wait()
                return carry

            lax.fori_loop(0, n, body, 0)

    @pl.when(i < n_used)
    def _():
        @pl.when(i == 0)
        def _():
            gather_start(0, 0)

        @pl.when(jnp.logical_or(i == 0, te_ref[i] != te_ref[jnp.maximum(i - 1, 0)]))
        def _():
            wgb[...] = wg_ref[0, 0].astype(BF16)
            wub[...] = wu_ref[0, 0].astype(BF16)
            wdb[...] = wd_ref[0, 0].astype(BF16)

        gather_wait(slot)

        @pl.when(i + 1 < n_used)
        def _():
            gather_start(i + 1, 1 - slot)

        x = xbuf[slot].astype(BF16)
        hg = jnp.dot(x, wgb[...], preferred_element_type=F32)
        hu = jnp.dot(x, wub[...], preferred_element_type=F32)
        act = (hg * jax.nn.sigmoid(hg)) * hu
        y = jnp.dot(act.astype(BF16), wdb[...], preferred_element_type=F32)

        @pl.when(i >= 2)
        def _():
            scatter_wait(slot, nv_ref[jnp.maximum(i - 2, 0)])

        ybuf[slot] = y
        scatter_start(i, slot, nv_ref[i])

        @pl.when(i == n_used - 1)
        def _():
            scatter_wait(slot, nv_ref[i])

            @pl.when(i >= 1)
            def _():
                scatter_wait(1 - slot, nv_ref[jnp.maximum(i - 1, 0)])


def _route(hext, d, tm):
    b, s, _ = hext.shape
    t = b * s
    n_rows = 2 * t
    n_tiles = n_rows // tm + N_EXPERTS
    eid = hext[:, :, d + 2:d + 4].reshape(n_rows).astype(jnp.int32)
    onehot = (eid[:, None] == jnp.arange(N_EXPERTS, dtype=jnp.int32)[None, :]).astype(jnp.int32)
    cnt = jnp.sum(onehot, axis=0)
    rank = jnp.sum(onehot * (jnp.cumsum(onehot, axis=0) - onehot), axis=1)
    ntile_e = (cnt + tm - 1) // tm
    tile_end_e = jnp.cumsum(ntile_e)
    tile_start_e = tile_end_e - ntile_e
    pos = (tile_start_e * tm)[eid] + rank
    code_of_row = jnp.zeros((n_tiles * tm,), jnp.int32).at[pos].set(jnp.arange(n_rows, dtype=jnp.int32))
    tile = jnp.arange(n_tiles, dtype=jnp.int32)
    tile_eid = jnp.minimum(jnp.sum((tile[:, None] >= tile_end_e[None, :]).astype(jnp.int32), axis=1), N_EXPERTS - 1)
    nvalid = jnp.clip(cnt[tile_eid] - (tile - tile_start_e[tile_eid]) * tm, 0, tm).astype(jnp.int32)
    n_used = tile_end_e[N_EXPERTS - 1:].astype(jnp.int32)
    return tile_eid.astype(jnp.int32), nvalid, n_used, code_of_row


def _moe(hext, wg_all, wu_all, wd_all, layer, tm=MOE_TM):
    b, s, dx = hext.shape
    d = dx - LANES
    t = b * s
    f = wg_all.shape[-1]
    tile_eid, nvalid, n_used, code_of_row = _route(hext, d, tm)
    n_tiles = tile_eid.shape[0]

    def wmap(i, te, nv, nu, row):
        return (layer, te[i], 0, 0)

    return pl.pallas_call(
        functools.partial(_moe_body, tm=tm, d=d, t=t),
        grid_spec=pltpu.PrefetchScalarGridSpec(
            num_scalar_prefetch=4,
            grid=(n_tiles,),
            in_specs=[pl.BlockSpec(memory_space=pl.ANY),
                      pl.BlockSpec((1, 1, d, f), wmap),
                      pl.BlockSpec((1, 1, d, f), wmap),
                      pl.BlockSpec((1, 1, f, d), wmap)],
            out_specs=pl.BlockSpec(memory_space=pl.ANY),
            scratch_shapes=[pltpu.VMEM((2, tm, d), F32),
                            pltpu.VMEM((2, tm, d), F32),
                            pltpu.VMEM((d, f), BF16),
                            pltpu.VMEM((d, f), BF16),
                            pltpu.VMEM((f, d), BF16),
                            pltpu.SemaphoreType.DMA((2,)),
                            pltpu.SemaphoreType.DMA((2,))],
        ),
        out_shape=jax.ShapeDtypeStruct((2 * t, d), F32),
        compiler_params=_cparams(1),
        name="moe_experts",
    )(tile_eid, nvalid, n_used, code_of_row, hext.reshape(t, dx), wg_all, wu_all, wd_all)


def _final_body(x_ref, y0_ref, y1_ref, gate_ref, mod_ref, g_ref, o_ref):
    x = x_ref[0] + mod_ref[0, 5:6, :] * _combine(gate_ref[0], y0_ref[...], y1_ref[...])
    ms = jnp.mean(x * x, axis=-1, keepdims=True)
    o_ref[0] = x * lax.rsqrt(ms + EPS) * g_ref[...]


def _final(x, y, hext, mod, g, tm=512):
    b, s, d = x.shape
    nt = s // tm
    return pl.pallas_call(
        _final_body,
        grid=(b, nt),
        in_specs=[pl.BlockSpec((1, tm, d), lambda bi, i: (bi, i, 0)),
                  pl.BlockSpec((tm, d), lambda bi, i: (bi * nt + i, 0)),
                  pl.BlockSpec((tm, d), lambda bi, i: (b * nt + bi * nt + i, 0)),
                  pl.BlockSpec((1, tm, LANES), lambda bi, i: (bi, i, d // LANES)),
                  pl.BlockSpec((1, N_MOD, d), lambda bi, i: (bi, 0, 0)),
                  pl.BlockSpec((1, d), lambda bi, i: (0, 0))],
        out_specs=pl.BlockSpec((1, tm, d), lambda bi, i: (bi, i, 0)),
        out_shape=jax.ShapeDtypeStruct(x.shape, F32),
        compiler_params=_cparams(2),
        name="final_norm",
    )(x, y, y, hext, mod, g.reshape(1, d))


def kernel(x, c, w_in, w_out, attn_norm, ffn_norm, w_ada, b_ada, lambda_q1, lambda_k1, lambda_q2, lambda_k2,
           diff_subln, hgrn_lb, hgrn_norm, rel_bias, w_router, b_router, w_gate, w_up, w_down, final_norm):
    b, s, d = x.shape
    depth = w_in.shape[0]

    w_in_b = w_in.astype(BF16)
    w_out_b = w_out.astype(BF16)
    wr = jnp.pad(w_router.astype(F32), ((0, 0), (0, LANES - N_EXPERTS)))
    wr_hi = wr.astype(BF16)
    wr_lo = (wr - wr_hi.astype(F32)).astype(BF16)
    br = jnp.pad(b_router.astype(F32), (0, LANES - N_EXPERTS)).reshape(1, LANES)
    lb_soft = jax.nn.softmax(hgrn_lb.astype(F32), axis=0)
    lower_bounds = jnp.maximum(jnp.cumsum(lb_soft, axis=0) - lb_soft[0:1], 0.0).reshape(depth, 1, -1)
    attn_norm3 = attn_norm.astype(F32).reshape(depth, 1, d)
    ffn_norm3 = ffn_norm.astype(F32).reshape(depth, 1, d)
    subln3 = diff_subln.astype(F32).reshape(depth, 1, HEAD)
    hnorm3 = hgrn_norm.astype(F32).reshape(depth, 1, HEAD)

    c8 = jnp.pad(c.astype(F32), ((0, 8 - b), (0, 0)))
    mods = _ada(c8, w_ada, b_ada)
    bias = _bias_tiles(rel_bias, ATTN_T)

    y = None
    hext = None
    mod_prev = None
    for layer in range(depth):
        mod = mods[layer, :b].reshape(b, N_MOD, d)
        lambda_init = 0.8 - 0.6 * math.exp(-0.3 * layer)
        lam = (jnp.exp(jnp.sum(lambda_q1[layer].astype(F32) * lambda_k1[layer].astype(F32)))
               - jnp.exp(jnp.sum(lambda_q2[layer].astype(F32) * lambda_k2[layer].astype(F32)))
               + lambda_init)
        scalars = jnp.stack([lam, jnp.asarray(1.0 - lambda_init, F32)]).astype(F32)

        x, proj = _inproj(x, y, hext, mod_prev, mod, attn_norm3, w_in_b, layer)
        attn_o = _attention(proj, bias, scalars, subln3, layer)
        hgrn_o = _hgrn(proj, lower_bounds, hnorm3, layer)
        x, hext = _outproj_router(attn_o, hgrn_o, w_out_b, x, mod, ffn_norm3, wr_hi, wr_lo, br, layer)
        y = _moe(hext, w_gate, w_up, w_down, layer)
        mod_prev = mod
    return _final(x, y, hext, mod_prev, final_norm.astype(F32))
```

```python
import functools
import math

import numpy as np
import jax
import jax.numpy as jnp
from jax import lax
from jax.experimental import pallas as pl
from jax.experimental.pallas import tpu as pltpu

F32 = jnp.float32
BF16 = jnp.bfloat16
EPS = 1e-6

LANES = 128
HEAD = 128
QK = 64
N_HEADS = 8
NUM_BUCKETS = 32
MAX_DISTANCE = 128
N_EXPERTS = 16
N_GROUPS = 4
GROUP_SIZE = N_EXPERTS // N_GROUPS
N_MOD = 6
NEG = -1e30

ATTN_T = 512
ATTN_CW = 512
ONES_ROWS = 16
LOG2E = math.log2(math.e)
HGRN_L = 512
HGRN_C = 64
HGRN_SUB = 8
HGRN_GROUP = 256
HGRN_SAFE_SPAN = 80.0
NORM_ROWS = 256
MOE_TM = 512
MOE_ISSUE_UNROLL = MOE_TM
VMEM_LIMIT = 56 * 1024 * 1024


def _cparams(n_axes):
    return pltpu.CompilerParams(dimension_semantics=("arbitrary",) * n_axes, vmem_limit_bytes=VMEM_LIMIT)


def _ada_body(c_ref, w_ref, b_ref, o_ref):
    c = c_ref[...]
    ca = (c * jax.nn.sigmoid(c)).astype(BF16)
    o_ref[0] = jnp.dot(ca, w_ref[0].astype(BF16), preferred_element_type=F32) + b_ref[0]


def _ada(c8, w_ada, b_ada, tn=1024):
    depth, d, n = w_ada.shape
    return pl.pallas_call(
        _ada_body,
        grid=(depth, n // tn),
        in_specs=[pl.BlockSpec((8, d), lambda l, j: (0, 0)),
                  pl.BlockSpec((1, d, tn), lambda l, j: (l, 0, j)),
                  pl.BlockSpec((1, 1, tn), lambda l, j: (l, 0, j))],
        out_specs=pl.BlockSpec((1, 8, tn), lambda l, j: (l, 0, j)),
        out_shape=jax.ShapeDtypeStruct((depth, 8, n), F32),
        compiler_params=_cparams(2),
        name="ada_mod",
    )(c8, w_ada, b_ada.reshape(depth, 1, n))


def _modulated_norm(x, g, scale, shift):
    ms = jnp.mean(x * x, axis=-1, keepdims=True)
    return (x * lax.rsqrt(ms + EPS) * g) * (1.0 + scale) + shift


def _inproj_body(*refs, has_res):
    if has_res:
        x_ref, y_ref, gp_ref, mod_ref, g_ref, w_ref, xo_ref, p_ref, h_scr = refs
    else:
        x_ref, mod_ref, g_ref, w_ref, p_ref, h_scr = refs

    @pl.when(pl.program_id(2) == 0)
    def _():
        tm = h_scr.shape[0]
        for r0 in range(0, tm, NORM_ROWS):
            rows = slice(r0, r0 + NORM_ROWS)
            x = x_ref[0, rows, :]
            if has_res:
                x = x + gp_ref[0, 5:6, :] * y_ref[rows, :]
                xo_ref[0, rows, :] = x
            h = _modulated_norm(x, g_ref[0], mod_ref[0, 1:2, :], mod_ref[0, 0:1, :])
            h_scr[rows, :] = h.astype(BF16)

    p_ref[0] = jnp.dot(h_scr[...], w_ref[0], preferred_element_type=F32).astype(BF16)


def _inproj(x, y, mod_prev, mod, g_all, w_all, layer, tm=1024, tn=1024):
    b, s, d = x.shape
    n = w_all.shape[-1]
    nt = s // tm
    has_res = y is not None
    xspec = pl.BlockSpec((1, tm, d), lambda bi, i, j: (bi, i, 0))
    modspec = pl.BlockSpec((1, N_MOD, d), lambda bi, i, j: (bi, 0, 0))
    once = dict(pipeline_mode=pl.Buffered(1))
    in_specs = [pl.BlockSpec((1, tm, d), lambda bi, i, j: (bi, i, 0), **once)]
    args = [x]
    if has_res:
        in_specs += [pl.BlockSpec((tm, d), lambda bi, i, j: (bi * nt + i, 0), **once), modspec]
        args += [y, mod_prev]
    in_specs += [modspec,
                 pl.BlockSpec((1, 1, d), lambda bi, i, j: (layer, 0, 0)),
                 pl.BlockSpec((1, d, tn), lambda bi, i, j: (layer, 0, j))]
    args += [mod, g_all, w_all]
    pspec = pl.BlockSpec((1, tm, tn), lambda bi, i, j: (bi, i, j))
    pshape = jax.ShapeDtypeStruct((b, s, n), BF16)
    if has_res:
        out_specs, out_shape = [xspec, pspec], [jax.ShapeDtypeStruct(x.shape, F32), pshape]
    else:
        out_specs, out_shape = pspec, pshape
    out = pl.pallas_call(
        functools.partial(_inproj_body, has_res=has_res),
        grid=(b, nt, n // tn),
        in_specs=in_specs, out_specs=out_specs, out_shape=out_shape,
        scratch_shapes=[pltpu.VMEM((tm, d), BF16)],
        compiler_params=_cparams(3),
        name="inproj",
    )(*args)
    return (out[0], out[1]) if has_res else (x, out)


def _t5_bucket_np(n):
    max_exact = NUM_BUCKETS // 2
    nf = np.maximum(n, 1).astype(np.float32)
    large = max_exact + (np.log(nf / np.float32(max_exact)) / np.float32(math.log(MAX_DISTANCE / max_exact))
                         * np.float32(NUM_BUCKETS - max_exact)).astype(np.int32)
    large = np.minimum(large, NUM_BUCKETS - 1)
    return np.where(n < max_exact, n, large).astype(np.int32)


def _bias_body(tab_ref, bd_ref, bu_ref, o_ref):
    m = pl.program_id(0) * 2 + pl.program_id(1)
    bd = bd_ref[...]
    bu = bu_ref[...]
    far = tab_ref[NUM_BUCKETS - 1, m]
    accd = jnp.zeros(bd.shape, F32)
    accu = jnp.zeros(bu.shape, F32)
    for bkt in range(NUM_BUCKETS - 1):
        val = (tab_ref[bkt, m] - far) * LOG2E
        accd = jnp.where(bd == bkt, val, accd)
        accu = jnp.where(bu == bkt, val, accu)
    o_ref[0, 0] = jnp.where(bd < 0, NEG, accd)
    o_ref[0, 1] = accu


def _bias_tiles(rel_bias, t):
    key = np.arange(t)[:, None]
    qry = np.arange(t)[None, :]
    bd = np.where(qry >= key, _t5_bucket_np(np.maximum(qry - key, 0)), -1).astype(np.int32)
    bu = _t5_bucket_np(t + qry - key)
    assert MAX_DISTANCE <= t + 1, "tiles two or more to the left must lie in the far bucket"
    return pl.pallas_call(
        _bias_body,
        grid=(N_HEADS, 2),
        in_specs=[pl.BlockSpec(memory_space=pltpu.SMEM),
                  pl.BlockSpec((t, t), lambda h, m: (0, 0)),
                  pl.BlockSpec((t, t), lambda h, m: (0, 0))],
        out_specs=pl.BlockSpec((1, 2, t, t), lambda h, m: (h, 0, 0, m)),
        out_shape=jax.ShapeDtypeStruct((N_HEADS, 2, t, 2 * t), F32),
        compiler_params=_cparams(2),
        name="bias_tiles",
    )(rel_bias.astype(F32), jnp.asarray(bd), jnp.asarray(bu))


def _attn_body(sc_ref, q_ref, k_ref, v_ref, bias_ref, g_ref, o_ref, qq_scr, vt_scr, m_scr, acc_scr, s0_scr, s1_scr,
               *, t, cw):
    qi = pl.program_id(2)
    s_len = k_ref.shape[1]

    @pl.when(qi == 0)
    def _():
        for ci in range(s_len // t):
            vt_scr[0:HEAD, ci * t:(ci + 1) * t] = v_ref[0, ci * t:(ci + 1) * t, :].T
        row = lax.broadcasted_iota(jnp.int32, (ONES_ROWS, s_len), 0)
        vt_scr[HEAD:HEAD + ONES_ROWS, :] = jnp.where(row == 0, 1.0, 0.0).astype(BF16)

    q = (q_ref[0].astype(F32) * (QK ** -0.5 * LOG2E)).astype(BF16)
    lane = lax.broadcasted_iota(jnp.int32, q.shape, 1)
    zero = jnp.zeros_like(q)
    qq_scr[0:t, :] = jnp.where(lane < QK, q, zero)
    qq_scr[t:2 * t, :] = jnp.where(lane >= QK, q, zero)
    m_scr[...] = jnp.full(m_scr.shape, NEG, F32)
    acc_scr[...] = jnp.zeros(acc_scr.shape, F32)

    tiles = [slice(ct * cw, (ct + 1) * cw) for ct in range(2 * t // cw)]

    bufs = (s0_scr, s1_scr)

    nt_dims = (((1,), (1,)), ((), ()))
    hk = t // 2
    diag_parts = [(k0, k1, slice(mi * t + k0, (mi + 1) * t)) for mi in range(2) for k0, k1 in ((0, hk), (hk, t))]

    def scores_into(j, buf):
        kb = k_ref[0, pl.ds(pl.multiple_of(j * t, t), t), :]
        for cs in tiles:
            buf[:, cs] = lax.dot_general(kb, qq_scr[cs, :], nt_dims, preferred_element_type=F32)

    def diag_scores_into(j, buf):
        for k0, k1, cs in diag_parts:
            kb = k_ref[0, pl.ds(pl.multiple_of(j * t + k0, hk), k1 - k0), :]
            buf[k0:k1, cs] = lax.dot_general(kb, qq_scr[cs, :], nt_dims, preferred_element_type=F32)

    def consume_part(j, buf, k0, k1, cs, bias_of):
        vt = vt_scr[:, pl.ds(pl.multiple_of(j * t + k0, hk), k1 - k0)]
        s = buf[k0:k1, cs]
        if bias_of is not None:
            s = s + bias_of(k0, k1, cs)
        m_prev = m_scr[:, cs]
        m_new = jnp.maximum(m_prev, jnp.max(s, axis=0, keepdims=True))
        alpha = jnp.exp2(m_prev - m_new)
        p = jnp.exp2(s - m_new)
        acc_scr[:, cs] = alpha * acc_scr[:, cs] + jnp.dot(vt, p.astype(BF16), preferred_element_type=F32)
        m_scr[:, cs] = m_new

    def consume(j, buf, bias_of):
        for cs in tiles:
            consume_part(j, buf, 0, t, cs, bias_of)

    def diag_consume(j, buf):
        for k0, k1, cs in diag_parts:
            consume_part(j, buf, k0, k1, cs, lambda a, b, c: bias_ref[0, 0, a:b, c])

    near = lambda a, b, c: bias_ref[0, 1, a:b, c]

    @pl.when(qi == 0)
    def _():
        diag_scores_into(0, bufs[0])

    @pl.when(qi > 0)
    def _():
        scores_into(0, bufs[0])

    n_far = jnp.maximum(qi - 1, 0)

    def far_pair(pi, carry):
        j = 2 * pi
        scores_into(j + 1, bufs[1])
        consume(j, bufs[0], None)
        scores_into(j + 2, bufs[0])
        consume(j + 1, bufs[1], None)
        return carry

    lax.fori_loop(0, n_far // 2, far_pair, 0)

    @pl.when(qi == 0)
    def _():
        diag_consume(0, bufs[0])

    @pl.when(jnp.logical_and(qi >= 1, lax.rem(qi, 2) == 1))
    def _():
        diag_scores_into(qi, bufs[1])
        consume(qi - 1, bufs[0], near)
        diag_consume(qi, bufs[1])

    @pl.when(jnp.logical_and(qi >= 2, lax.rem(qi, 2) == 0))
    def _():
        scores_into(qi - 1, bufs[1])
        consume(qi - 2, bufs[0], None)
        diag_scores_into(qi, bufs[0])
        consume(qi - 1, bufs[1], near)
        diag_consume(qi, bufs[0])

    inv = 1.0 / acc_scr[HEAD:HEAD + 1, :]
    o = acc_scr[0:HEAD, 0:t] * inv[:, 0:t] - acc_scr[0:HEAD, t:2 * t] * (sc_ref[0] * inv[:, t:2 * t])
    ms = jnp.mean(o * o, axis=0, keepdims=True)
    o = (o * lax.rsqrt(ms + EPS)).T
    o_ref[0] = ((o * g_ref[0]) * sc_ref[1]).astype(BF16)


def _attention(proj, bias, scalars, subln_all, layer, t=ATTN_T):
    b, s, _ = proj.shape
    nq = s // t
    return pl.pallas_call(
        functools.partial(_attn_body, t=t, cw=ATTN_CW),
        grid=(b, N_HEADS, nq),
        in_specs=[pl.BlockSpec(memory_space=pltpu.SMEM),
                  pl.BlockSpec((1, t, HEAD), lambda bi, h, i: (bi, i, h)),
                  pl.BlockSpec((1, s, HEAD), lambda bi, h, i: (bi, 0, N_HEADS + h)),
                  pl.BlockSpec((1, s, HEAD), lambda bi, h, i: (bi, 0, 2 * N_HEADS + h)),
                  pl.BlockSpec((1, 2, t, 2 * t), lambda bi, h, i: (h, 0, 0, 0)),
                  pl.BlockSpec((1, 1, HEAD), lambda bi, h, i: (layer, 0, 0))],
        out_specs=pl.BlockSpec((1, t, HEAD), lambda bi, h, i: (bi, i, h)),
        out_shape=jax.ShapeDtypeStruct((b, s, N_HEADS * HEAD), BF16),
        scratch_shapes=[pltpu.VMEM((2 * t, HEAD), BF16),
                        pltpu.VMEM((HEAD + ONES_ROWS, s), BF16),
                        pltpu.VMEM((1, 2 * t), F32),
                        pltpu.VMEM((HEAD + ONES_ROWS, 2 * t), F32),
                        pltpu.VMEM((t, 2 * t), F32),
                        pltpu.VMEM((t, 2 * t), F32)],
        compiler_params=_cparams(3),
        name="diff_attention",
    )(scalars, proj, proj, proj, bias, subln_all)


def _hgrn_levels(c, sub):
    levels = []
    g = c // 2
    while g >= sub:
        levels.append(g)
        g //= 2
    return levels


def _hgrn_consts(l, c, sub):
    r = np.arange(HGRN_GROUP)[:, None]
    s = np.arange(HGRN_GROUP)[None, :]
    lcum = ((r // c == s // c) & (s <= r)).astype(np.float32)
    rc = np.arange(c)[:, None]
    sc = np.arange(c)[None, :]
    masks = [((rc // (2 * g) == sc // (2 * g)) & (rc % (2 * g) >= g) & (sc % (2 * g) < g)).astype(np.float32)
             for g in _hgrn_levels(c, sub)]
    half = c // 2
    rg = np.arange(HGRN_GROUP)[:, None]
    sg = np.arange(HGRN_GROUP)[None, :]
    top = (rg // c == sg // c) & (rg % c >= half) & (sg % c < half)
    near = (rg // half == sg // half) & (sg <= rg)
    fmask = np.stack([top, near]).astype(np.float32)
    return jnp.asarray(lcum, BF16), jnp.asarray(np.stack(masks), F32), jnp.asarray(fmask, F32)


def _hgrn_body(q_ref, f_ref, i_ref, g_ref, lb_ref, ng_ref, lcum_ref, mask_ref, fmask_ref, o_ref,
               st_scr, stb_scr, a_scr, oi_scr, *, l, c, sub):
    nchunk = l // c
    levels = _hgrn_levels(c, sub)

    @pl.when(pl.program_id(2) == 0)
    def _():
        st_scr[...] = jnp.zeros(st_scr.shape, F32)

    q = q_ref[0].astype(F32)
    fr = f_ref[0].astype(F32)
    vb = i_ref[0]
    lb = lb_ref[0]
    f = lb + (1.0 - lb) * jax.nn.sigmoid(fr)
    logf = jnp.log(jnp.maximum(f, jnp.finfo(F32).tiny))
    k = (1.0 - lb) * jax.nn.sigmoid(-fr)

    hi = logf.astype(BF16)
    lo = (logf - hi.astype(F32)).astype(BF16)
    hilo = jnp.concatenate([hi, lo], axis=1)
    lcum = lcum_ref[...]
    grp = lcum.shape[0]
    parts = []
    for gi in range(l // grp):
        both = jnp.dot(lcum, hilo[gi * grp:(gi + 1) * grp], preferred_element_type=F32)
        parts.append(both[:, 0:HEAD] + both[:, HEAD:2 * HEAD])
    bcum = jnp.concatenate(parts, axis=0)

    def rows_of(arr, group, row):
        a3 = arr.reshape(l // group, group, HEAD)
        return jnp.broadcast_to(a3[:, row:row + 1, :], a3.shape).reshape(l, HEAD)

    nt = (((1,), (1,)), ((), ()))
    tn = (((0,), (0,)), ((), ()))

    def level_operands(g):
        ref = rows_of(bcum, 2 * g, g - 1)
        return ((q * jnp.exp(jnp.minimum(bcum - ref, 0.0))).astype(BF16),
                (k * jnp.exp(jnp.minimum(ref - bcum, 0.0))).astype(BF16))

    q_top, k_top = level_operands(levels[0])

    def top_scores(sl):
        return mask_ref[0] * lax.dot_general(q_top[sl], k_top[sl], nt, preferred_element_type=F32)

    half = c // 2
    b_start = rows_of(bcum, half, 0) - rows_of(logf, half, 0)
    span = b_start - bcum
    bounded = jnp.max(span) <= HGRN_SAFE_SPAN

    @pl.when(bounded)
    def _():
        q_f = (q * jnp.exp(-span)).astype(BF16)
        k_f = (k * jnp.exp(span)).astype(BF16)
        grp = fmask_ref.shape[-1]
        for gi in range(l // grp):
            sl = slice(gi * grp, (gi + 1) * grp)
            a = fmask_ref[0] * lax.dot_general(q_top[sl], k_top[sl], nt, preferred_element_type=F32)
            a = a + fmask_ref[1] * lax.dot_general(q_f[sl], k_f[sl], nt, preferred_element_type=F32)
            a_scr[sl, :] = jnp.dot(a.astype(BF16), vb[sl], preferred_element_type=F32)

    @pl.when(jnp.logical_not(bounded))
    def _():
        lower = [level_operands(g) for g in levels[1:]]
        nb = l // sub
        b3 = bcum.reshape(nb, sub, HEAD)
        q3 = q.reshape(nb, sub, HEAD)
        k3 = k.reshape(nb, sub, HEAD)
        v3 = vb.astype(F32).reshape(nb, sub, HEAD)
        tloc = lax.broadcasted_iota(jnp.int32, (nb, sub, 1), 1)
        od = jnp.zeros((nb, sub, HEAD), F32)
        for si in range(sub):
            e = jnp.exp(b3 - b3[:, si:si + 1, :])
            pr = q3 * (k3[:, si:si + 1, :] * e)
            w = jnp.sum(pr, axis=-1, keepdims=True)
            w = jnp.where(tloc >= si, w, 0.0)
            od = od + w * v3[:, si:si + 1, :]
        od = od.reshape(l, HEAD)
        for ci in range(nchunk):
            sl = slice(ci * c, (ci + 1) * c)
            a = top_scores(sl)
            for li, (q_g, k_g) in enumerate(lower):
                a = a + mask_ref[li + 1] * lax.dot_general(q_g[sl], k_g[sl], nt, preferred_element_type=F32)
            a_scr[sl, :] = jnp.dot(a.astype(BF16), vb[sl], preferred_element_type=F32) + od[sl]

    b_end = rows_of(bcum, c, c - 1)
    q_in = (q * jnp.exp(bcum)).astype(BF16)
    k_out = (k * jnp.exp(b_end - bcum)).astype(BF16)
    dec = jnp.exp(b_end)
    kv = [lax.dot_general(vb[ci * c:(ci + 1) * c], k_out[ci * c:(ci + 1) * c], tn, preferred_element_type=F32)
          for ci in range(nchunk)]
    st = st_scr[...]
    for ci in range(nchunk):
        stb_scr[ci] = st.astype(BF16)
        st = st * dec[ci * c:ci * c + 1, :] + kv[ci]
    st_scr[...] = st
    for ci in range(nchunk):
        sl = slice(ci * c, (ci + 1) * c)
        oi_scr[sl, :] = lax.dot_general(q_in[sl], stb_scr[ci], nt, preferred_element_type=F32)

    o = a_scr[...] + oi_scr[...]
    ms = jnp.mean(o * o, axis=-1, keepdims=True)
    o = o * lax.rsqrt(ms + EPS) * ng_ref[0]
    gate = g_ref[0].astype(F32)
    o_ref[0] = (o * (gate * jax.nn.sigmoid(gate))).astype(BF16)


def _hgrn(proj, lower_bounds, norm_all, layer, l=HGRN_L, c=HGRN_C, sub=HGRN_SUB):
    b, s, _ = proj.shape
    lcum, masks, fmask = _hgrn_consts(l, c, sub)
    base = 3 * N_HEADS

    def col(kind):
        return pl.BlockSpec((1, l, HEAD), lambda bi, h, i: (bi, i, base + kind * N_HEADS + h))

    return pl.pallas_call(
        functools.partial(_hgrn_body, l=l, c=c, sub=sub),
        grid=(b, N_HEADS, s // l),
        in_specs=[col(0), col(1), col(2), col(3),
                  pl.BlockSpec((1, 1, HEAD), lambda bi, h, i: (layer, 0, h)),
                  pl.BlockSpec((1, 1, HEAD), lambda bi, h, i: (layer, 0, 0)),
                  pl.BlockSpec(lcum.shape, lambda bi, h, i: (0, 0)),
                  pl.BlockSpec(masks.shape, lambda bi, h, i: (0, 0, 0)),
                  pl.BlockSpec(fmask.shape, lambda bi, h, i: (0, 0, 0))],
        out_specs=pl.BlockSpec((1, l, HEAD), lambda bi, h, i: (bi, i, h)),
        out_shape=jax.ShapeDtypeStruct((b, s, N_HEADS * HEAD), BF16),
        scratch_shapes=[pltpu.VMEM((HEAD, HEAD), F32),
                        pltpu.VMEM((l // c, HEAD, HEAD), BF16),
                        pltpu.VMEM((l, HEAD), F32),
                        pltpu.VMEM((l, HEAD), F32)],
        compiler_params=_cparams(3),
        name="hgrn2",
    )(proj, proj, proj, proj, lower_bounds, norm_all, lcum, masks, fmask)


def _first_lane(cond, lane):
    return jnp.min(jnp.where(cond, lane, LANES), axis=-1, keepdims=True)


def _outproj_body(a_ref, hg_ref, w_ref, x_ref, mod_ref, g_ref, wrh_ref, wrl_ref, br_ref, xo_ref, h_ref, *, d):
    da = a_ref.shape[-1]
    mixed = jnp.dot(a_ref[0], w_ref[0, 0:da, :], preferred_element_type=F32)
    mixed = mixed + jnp.dot(hg_ref[0], w_ref[0, da:, :], preferred_element_type=F32)
    x = x_ref[0] + mod_ref[0, 2:3, :] * mixed
    xo_ref[0] = x
    h = _modulated_norm(x, g_ref[0], mod_ref[0, 4:5, :], mod_ref[0, 3:4, :])
    h_ref[0, :, 0:d] = h

    h_hi = h.astype(BF16)
    h_lo = (h - h_hi.astype(F32)).astype(BF16)
    logits = (jnp.dot(h_hi, wrh_ref[...], preferred_element_type=F32)
              + jnp.dot(h_lo, wrh_ref[...], preferred_element_type=F32)
              + jnp.dot(h_hi, wrl_ref[...], preferred_element_type=F32)) + br_ref[...]
    lane = lax.broadcasted_iota(jnp.int32, logits.shape, 1)
    valid = lane < N_EXPERTS
    logits = jnp.where(valid, logits, NEG)
    mx = jnp.max(logits, axis=-1, keepdims=True)
    ex = jnp.where(valid, jnp.exp(logits - mx), 0.0)
    probs = ex / jnp.sum(ex, axis=-1, keepdims=True)

    best = sel = v1 = v2 = i1 = i2 = None
    for gi in range(N_GROUPS):
        ing = (lane >= gi * GROUP_SIZE) & (lane < (gi + 1) * GROUP_SIZE)
        pg = jnp.where(ing, probs, -1.0)
        m1 = jnp.max(pg, axis=-1, keepdims=True)
        a1 = _first_lane(pg == m1, lane)
        pg2 = jnp.where(lane == a1, -1.0, pg)
        m2 = jnp.max(pg2, axis=-1, keepdims=True)
        a2 = _first_lane(pg2 == m2, lane)
        score = m1 + m2
        if gi == 0:
            best, sel, v1, v2, i1, i2 = score, jnp.zeros_like(a1), m1, m2, a1, a2
        else:
            better = score > best
            best = jnp.where(better, score, best)
            sel = jnp.where(better, gi, sel)
            v1 = jnp.where(better, m1, v1)
            v2 = jnp.where(better, m2, v2)
            i1 = jnp.where(better, a1, i1)
            i2 = jnp.where(better, a2, i2)
    tot = v1 + v2
    w1 = v1 / tot
    w2 = v2 / tot
    loc1 = i1 - sel * GROUP_SIZE
    loc2 = i2 - sel * GROUP_SIZE
    ext = jnp.where(lane == loc1, w1, 0.0) + jnp.where(lane == loc2, w2, 0.0)
    ext = jnp.where(lane == GROUP_SIZE, sel.astype(F32), ext)
    h_ref[0, :, d:] = ext


def _outproj_router(attn_o, hgrn_o, w_out_all, x, mod, ffn_norm_all, wr_hi, wr_lo, br, layer, tm=512):
    b, s, d = x.shape
    da = attn_o.shape[-1]
    dmix = w_out_all.shape[1]
    tok = lambda bi, i: (bi, i, 0)
    return pl.pallas_call(
        functools.partial(_outproj_body, d=d),
        grid=(b, s // tm),
        in_specs=[pl.BlockSpec((1, tm, da), tok),
                  pl.BlockSpec((1, tm, dmix - da), tok),
                  pl.BlockSpec((1, dmix, d), lambda bi, i: (layer, 0, 0)),
                  pl.BlockSpec((1, tm, d), tok),
                  pl.BlockSpec((1, N_MOD, d), lambda bi, i: (bi, 0, 0)),
                  pl.BlockSpec((1, 1, d), lambda bi, i: (layer, 0, 0)),
                  pl.BlockSpec((d, LANES), lambda bi, i: (0, 0)),
                  pl.BlockSpec((d, LANES), lambda bi, i: (0, 0)),
                  pl.BlockSpec((1, LANES), lambda bi, i: (0, 0))],
        out_specs=[pl.BlockSpec((1, tm, d), tok),
                   pl.BlockSpec((1, tm, d + LANES), tok)],
        out_shape=[jax.ShapeDtypeStruct((b, s, d), F32),
                   jax.ShapeDtypeStruct((b, s, d + LANES), F32)],
        compiler_params=_cparams(2),
        name="outproj_router",
    )(attn_o, hgrn_o, w_out_all, x, mod, ffn_norm_all, wr_hi, wr_lo, br)


def _moe_body(tg_ref, nv_ref, nu_ref, tok_ref, h_hbm, wg_ref, wu_ref, wd_ref, y_hbm,
              xbuf, xbf, acc, gsem, ssem, *, tm, d):
    i = pl.program_id(0)
    e = pl.program_id(1)
    n_used = nu_ref[0]
    slot = lax.rem(i, 2)

    def gather_start(tile, sl):
        base = tile * tm

        def body(r, carry):
            pltpu.make_async_copy(h_hbm.at[pl.ds(tok_ref[base + r], 1)], xbuf.at[sl, pl.ds(r, 1)], gsem.at[sl]).start()
            return carry

        lax.fori_loop(0, tm, body, 0, unroll=MOE_ISSUE_UNROLL)

    def gather_wait(sl):
        pltpu.make_async_copy(h_hbm.at[pl.ds(0, tm)], xbuf.at[sl], gsem.at[sl]).wait()

    def scatter_start(tile, sl, n):
        base = tile * tm

        def body(r, carry):
            pltpu.make_async_copy(acc.at[sl, pl.ds(r, 1)], y_hbm.at[pl.ds(tok_ref[base + r], 1)], ssem.at[sl]).start()
            return carry

        @pl.when(n == tm)
        def _():
            lax.fori_loop(0, tm, body, 0, unroll=MOE_ISSUE_UNROLL)

        @pl.when(n < tm)
        def _():
            lax.fori_loop(0, n, body, 0)

    def scatter_wait(sl, n):
        @pl.when(n == tm)
        def _():
            pltpu.make_async_copy(acc.at[sl], y_hbm.at[pl.ds(0, tm)], ssem.at[sl]).wait()

        @pl.when(n < tm)
        def _():
            def body(r, carry):
                pltpu.make_async_copy(acc.at[sl, pl.ds(r, 1)], y_hbm.at[pl.ds(r, 1)], ssem.at[sl]).wait()
                return carry

            lax.fori_loop(0, n, body, 0)

    @pl.when(i < n_used)
    def _():
        @pl.when(e == 0)
        def _():
            @pl.when(i == 0)
            def _():
                gather_start(0, 0)

            gather_wait(slot)

            @pl.when(i + 1 < n_used)
            def _():
                gather_start(i + 1, 1 - slot)

            xbf[...] = xbuf[slot, :, 0:d].astype(BF16)

            @pl.when(i >= 2)
            def _():
                scatter_wait(slot, nv_ref[jnp.maximum(i - 2, 0)])

            acc[slot] = jnp.zeros((tm, d), F32)

        ext = xbuf[slot, :, d:d + LANES]
        gate = jnp.zeros((tm, 1), F32)
        for ei in range(GROUP_SIZE):
            gate = jnp.where(e == ei, ext[:, ei:ei + 1], gate)
        x = xbf[...]
        hg = jnp.dot(x, wg_ref[0, 0], preferred_element_type=F32)
        hu = jnp.dot(x, wu_ref[0, 0], preferred_element_type=F32)
        act = (hg * jax.nn.sigmoid(hg)) * hu * gate
        acc[slot] += jnp.dot(act.astype(BF16), wd_ref[0, 0], preferred_element_type=F32)

        @pl.when(e == GROUP_SIZE - 1)
        def _():
            scatter_start(i, slot, nv_ref[i])

            @pl.when(i == n_used - 1)
            def _():
                scatter_wait(slot, nv_ref[i])

                @pl.when(i >= 1)
                def _():
                    scatter_wait(1 - slot, nv_ref[jnp.maximum(i - 1, 0)])


def _route(hext, d, tm):
    b, s, _ = hext.shape
    t = b * s
    n_tiles = t // tm + N_GROUPS
    gid = hext[:, :, d + GROUP_SIZE].reshape(t).astype(jnp.int32)
    onehot = (gid[:, None] == jnp.arange(N_GROUPS, dtype=jnp.int32)[None, :]).astype(jnp.int32)
    cnt = jnp.sum(onehot, axis=0)
    rank = jnp.sum(onehot * (jnp.cumsum(onehot, axis=0) - onehot), axis=1)
    ntile_g = (cnt + tm - 1) // tm
    tile_end_g = jnp.cumsum(ntile_g)
    tile_start_g = tile_end_g - ntile_g
    pos = (tile_start_g * tm)[gid] + rank
    tok_of_row = jnp.zeros((n_tiles * tm,), jnp.int32).at[pos].set(jnp.arange(t, dtype=jnp.int32))
    tile = jnp.arange(n_tiles, dtype=jnp.int32)
    tile_gid = jnp.minimum(jnp.sum((tile[:, None] >= tile_end_g[None, :]).astype(jnp.int32), axis=1), N_GROUPS - 1)
    nvalid = jnp.clip(cnt[tile_gid] - (tile - tile_start_g[tile_gid]) * tm, 0, tm).astype(jnp.int32)
    n_used = tile_end_g[N_GROUPS - 1:].astype(jnp.int32)
    return tile_gid.astype(jnp.int32), nvalid, n_used, tok_of_row


def _moe(hext, wg_all, wu_all, wd_all, layer, tm=MOE_TM):
    b, s, dx = hext.shape
    d = dx - LANES
    t = b * s
    f = wg_all.shape[-1]
    tile_gid, nvalid, n_used, tok_of_row = _route(hext, d, tm)
    n_tiles = tile_gid.shape[0]

    def wmap(i, e, tg, nv, nu, tok):
        return (layer, tg[i] * GROUP_SIZE + e, 0, 0)

    return pl.pallas_call(
        functools.partial(_moe_body, tm=tm, d=d),
        grid_spec=pltpu.PrefetchScalarGridSpec(
            num_scalar_prefetch=4,
            grid=(n_tiles, GROUP_SIZE),
            in_specs=[pl.BlockSpec(memory_space=pl.ANY),
                      pl.BlockSpec((1, 1, d, f), wmap),
                      pl.BlockSpec((1, 1, d, f), wmap),
                      pl.BlockSpec((1, 1, f, d), wmap)],
            out_specs=pl.BlockSpec(memory_space=pl.ANY),
            scratch_shapes=[pltpu.VMEM((2, tm, dx), F32),
                            pltpu.VMEM((tm, d), BF16),
                            pltpu.VMEM((2, tm, d), F32),
                            pltpu.SemaphoreType.DMA((2,)),
                            pltpu.SemaphoreType.DMA((2,))],
        ),
        out_shape=jax.ShapeDtypeStruct((t, d), F32),
        compiler_params=_cparams(2),
        name="moe_experts",
    )(tile_gid, nvalid, n_used, tok_of_row, hext.reshape(t, dx), wg_all, wu_all, wd_all)


def _final_body(x_ref, y_ref, mod_ref, g_ref, o_ref):
    x = x_ref[0] + mod_ref[0, 5:6, :] * y_ref[...]
    ms = jnp.mean(x * x, axis=-1, keepdims=True)
    o_ref[0] = x * lax.rsqrt(ms + EPS) * g_ref[...]


def _final(x, y, mod, g, tm=512):
    b, s, d = x.shape
    nt = s // tm
    return pl.pallas_call(
        _final_body,
        grid=(b, nt),
        in_specs=[pl.BlockSpec((1, tm, d), lambda bi, i: (bi, i, 0)),
                  pl.BlockSpec((tm, d), lambda bi, i: (bi * nt + i, 0)),
                  pl.BlockSpec((1, N_MOD, d), lambda bi, i: (bi, 0, 0)),
                  pl.BlockSpec((1, d), lambda bi, i: (0, 0))],
        out_specs=pl.BlockSpec((1, tm, d), lambda bi, i: (bi, i, 0)),
        out_shape=jax.ShapeDtypeStruct(x.shape, F32),
        compiler_params=_cparams(2),
        name="final_norm",
    )(x, y, mod, g.reshape(1, d))


def kernel(x, c, w_in, w_out, attn_norm, ffn_norm, w_ada, b_ada, lambda_q1, lambda_k1, lambda_q2, lambda_k2,
           diff_subln, hgrn_lb, hgrn_norm, rel_bias, w_router, b_router, w_gate, w_up, w_down, final_norm):
    b, s, d = x.shape
    depth = w_in.shape[0]

    w_in_b = w_in.astype(BF16)
    w_out_b = w_out.astype(BF16)
    w_gate_b = w_gate.astype(BF16)
    w_up_b = w_up.astype(BF16)
    w_down_b = w_down.astype(BF16)
    wr = jnp.pad(w_router.astype(F32), ((0, 0), (0, LANES - N_EXPERTS)))
    wr_hi = wr.astype(BF16)
    wr_lo = (wr - wr_hi.astype(F32)).astype(BF16)
    br = jnp.pad(b_router.astype(F32), (0, LANES - N_EXPERTS)).reshape(1, LANES)
    lb_soft = jax.nn.softmax(hgrn_lb.astype(F32), axis=0)
    lower_bounds = jnp.maximum(jnp.cumsum(lb_soft, axis=0) - lb_soft[0:1], 0.0).reshape(depth, 1, -1)
    attn_norm3 = attn_norm.astype(F32).reshape(depth, 1, d)
    ffn_norm3 = ffn_norm.astype(F32).reshape(depth, 1, d)
    subln3 = diff_subln.astype(F32).reshape(depth, 1, HEAD)
    hnorm3 = hgrn_norm.astype(F32).reshape(depth, 1, HEAD)

    c8 = jnp.pad(c.astype(F32), ((0, 8 - b), (0, 0)))
    mods = _ada(c8, w_ada, b_ada)
    bias = _bias_tiles(rel_bias, ATTN_T)

    y = None
    mod_prev = None
    for layer in range(depth):
        mod = mods[layer, :b].reshape(b, N_MOD, d)
        lambda_init = 0.8 - 0.6 * math.exp(-0.3 * layer)
        lam = (jnp.exp(jnp.sum(lambda_q1[layer].astype(F32) * lambda_k1[layer].astype(F32)))
               - jnp.exp(jnp.sum(lambda_q2[layer].astype(F32) * lambda_k2[layer].astype(F32)))
               + lambda_init)
        scalars = jnp.stack([lam, jnp.asarray(1.0 - lambda_init, F32)]).astype(F32)

        x, proj = _inproj(x, y, mod_prev, mod, attn_norm3, w_in_b, layer)
        attn_o = _attention(proj, bias, scalars, subln3, layer)
        hgrn_o = _hgrn(proj, lower_bounds, hnorm3, layer)
        x, hext = _outproj_router(attn_o, hgrn_o, w_out_b, x, mod, ffn_norm3, wr_hi, wr_lo, br, layer)
        y = _moe(hext, w_gate_b, w_up_b, w_down_b, layer)
        mod_prev = mod
    return _final(x, y, mod_prev, final_norm.astype(F32))
```

```python
import functools
import math

import numpy as np
import jax
import jax.numpy as jnp
from jax import lax
from jax.experimental import pallas as pl
from jax.experimental.pallas import tpu as pltpu

F32 = jnp.float32
BF16 = jnp.bfloat16
EPS = 1e-6

LANES = 128
HEAD = 128
QK = 64
N_HEADS = 8
NUM_BUCKETS = 32
MAX_DISTANCE = 128
N_EXPERTS = 16
N_GROUPS = 4
GROUP_SIZE = N_EXPERTS // N_GROUPS
N_MOD = 6
NEG = -1e30

ATTN_T = 512
ATTN_CW = 512
ONES_ROWS = 16
LOG2E = math.log2(math.e)
HGRN_L = 2048
HGRN_C = 64
HGRN_SUB = 8
HGRN_GROUP = 256
HGRN_SAFE_SPAN = 80.0
NORM_ROWS = 256
MOE_TM = 512
MOE_ISSUE_UNROLL = MOE_TM
VMEM_LIMIT = 56 * 1024 * 1024


def _cparams(n_axes):
    return pltpu.CompilerParams(dimension_semantics=("arbitrary",) * n_axes, vmem_limit_bytes=VMEM_LIMIT)


def _ada_body(c_ref, w_ref, b_ref, o_ref):
    c = c_ref[...]
    ca = (c * jax.nn.sigmoid(c)).astype(BF16)
    o_ref[0] = jnp.dot(ca, w_ref[0].astype(BF16), preferred_element_type=F32) + b_ref[0]


def _ada(c8, w_ada, b_ada, tn=1024):
    depth, d, n = w_ada.shape
    return pl.pallas_call(
        _ada_body,
        grid=(depth, n // tn),
        in_specs=[pl.BlockSpec((8, d), lambda l, j: (0, 0)),
                  pl.BlockSpec((1, d, tn), lambda l, j: (l, 0, j)),
                  pl.BlockSpec((1, 1, tn), lambda l, j: (l, 0, j))],
        out_specs=pl.BlockSpec((1, 8, tn), lambda l, j: (l, 0, j)),
        out_shape=jax.ShapeDtypeStruct((depth, 8, n), F32),
        compiler_params=_cparams(2),
        name="ada_mod",
    )(c8, w_ada, b_ada.reshape(depth, 1, n))


def _modulated_norm(x, g, scale, shift):
    ms = jnp.mean(x * x, axis=-1, keepdims=True)
    return (x * lax.rsqrt(ms + EPS) * g) * (1.0 + scale) + shift


def _inproj_body(*refs, has_res):
    if has_res:
        x_ref, y_ref, gp_ref, mod_ref, g_ref, w_ref, xo_ref, p_ref, h_scr = refs
    else:
        x_ref, mod_ref, g_ref, w_ref, p_ref, h_scr = refs

    @pl.when(pl.program_id(2) == 0)
    def _():
        tm = h_scr.shape[0]
        for r0 in range(0, tm, NORM_ROWS):
            rows = slice(r0, r0 + NORM_ROWS)
            x = x_ref[0, rows, :]
            if has_res:
                x = x + gp_ref[0, 5:6, :] * y_ref[rows, :]
                xo_ref[0, rows, :] = x
            h = _modulated_norm(x, g_ref[0], mod_ref[0, 1:2, :], mod_ref[0, 0:1, :])
            h_scr[rows, :] = h.astype(BF16)

    p_ref[0] = jnp.dot(h_scr[...], w_ref[0], preferred_element_type=F32).astype(BF16)


def _inproj(x, y, mod_prev, mod, g_all, w_all, layer, tm=1024, tn=1024):
    b, s, d = x.shape
    n = w_all.shape[-1]
    nt = s // tm
    has_res = y is not None
    xspec = pl.BlockSpec((1, tm, d), lambda bi, i, j: (bi, i, 0))
    modspec = pl.BlockSpec((1, N_MOD, d), lambda bi, i, j: (bi, 0, 0))
    once = dict(pipeline_mode=pl.Buffered(1))
    in_specs = [pl.BlockSpec((1, tm, d), lambda bi, i, j: (bi, i, 0), **once)]
    args = [x]
    if has_res:
        in_specs += [pl.BlockSpec((tm, d), lambda bi, i, j: (bi * nt + i, 0), **once), modspec]
        args += [y, mod_prev]
    in_specs += [modspec,
                 pl.BlockSpec((1, 1, d), lambda bi, i, j: (layer, 0, 0)),
                 pl.BlockSpec((1, d, tn), lambda bi, i, j: (layer, 0, j))]
    args += [mod, g_all, w_all]
    pspec = pl.BlockSpec((1, tm, tn), lambda bi, i, j: (bi, i, j))
    pshape = jax.ShapeDtypeStruct((b, s, n), BF16)
    if has_res:
        out_specs, out_shape = [xspec, pspec], [jax.ShapeDtypeStruct(x.shape, F32), pshape]
    else:
        out_specs, out_shape = pspec, pshape
    out = pl.pallas_call(
        functools.partial(_inproj_body, has_res=has_res),
        grid=(b, nt, n // tn),
        in_specs=in_specs, out_specs=out_specs, out_shape=out_shape,
        scratch_shapes=[pltpu.VMEM((tm, d), BF16)],
        compiler_params=_cparams(3),
        name="inproj",
    )(*args)
    return (out[0], out[1]) if has_res else (x, out)


def _t5_bucket_np(n):
    max_exact = NUM_BUCKETS // 2
    nf = np.maximum(n, 1).astype(np.float32)
    large = max_exact + (np.log(nf / np.float32(max_exact)) / np.float32(math.log(MAX_DISTANCE / max_exact))
                         * np.float32(NUM_BUCKETS - max_exact)).astype(np.int32)
    large = np.minimum(large, NUM_BUCKETS - 1)
    return np.where(n < max_exact, n, large).astype(np.int32)


def _bias_body(tab_ref, bd_ref, bu_ref, o_ref):
    t = bd_ref.shape[0]
    tab = jnp.broadcast_to(tab_ref[0], (t, LANES))
    for ti, idx_ref in enumerate((bd_ref, bu_ref)):
        for cb in range(t // LANES):
            cols = slice(cb * LANES, (cb + 1) * LANES)
            idx = idx_ref[:, cols]
            val = jnp.take_along_axis(tab, jnp.maximum(idx, 0), axis=1)
            o_ref[0, ti, :, cols] = jnp.where(idx < 0, NEG, val)


def _bias_tiles(rel_bias, t):
    key = np.arange(t)[:, None]
    qry = np.arange(t)[None, :]
    bd = np.where(qry >= key, _t5_bucket_np(np.maximum(qry - key, 0)), -1).astype(np.int32)
    bu = _t5_bucket_np(t + qry - key)
    assert MAX_DISTANCE <= t + 1, "tiles two or more to the left must lie in the far bucket"
    tab = (rel_bias.astype(F32) - rel_bias[NUM_BUCKETS - 1:].astype(F32)) * LOG2E
    tab = jnp.pad(tab.T, ((0, 0), (0, LANES - NUM_BUCKETS))).reshape(2 * N_HEADS, 1, LANES)
    return pl.pallas_call(
        _bias_body,
        grid=(N_HEADS, 2),
        in_specs=[pl.BlockSpec((1, 1, LANES), lambda h, m: (2 * h + m, 0, 0)),
                  pl.BlockSpec((t, t), lambda h, m: (0, 0)),
                  pl.BlockSpec((t, t), lambda h, m: (0, 0))],
        out_specs=pl.BlockSpec((1, 2, t, t), lambda h, m: (h, 0, 0, m)),
        out_shape=jax.ShapeDtypeStruct((N_HEADS, 2, t, 2 * t), F32),
        compiler_params=_cparams(2),
        name="bias_tiles",
    )(tab, jnp.asarray(bd), jnp.asarray(bu))


def _attn_body(sc_ref, q_ref, k_ref, v_ref, bias_ref, g_ref, o_ref, qq_scr, vt_scr, m_scr, acc_scr, s0_scr, s1_scr,
               *, t, cw):
    qi = pl.program_id(2)
    s_len = k_ref.shape[1]

    @pl.when(qi == 0)
    def _():
        for ci in range(s_len // t):
            vt_scr[0:HEAD, ci * t:(ci + 1) * t] = v_ref[0, ci * t:(ci + 1) * t, :].T
        row = lax.broadcasted_iota(jnp.int32, (ONES_ROWS, s_len), 0)
        vt_scr[HEAD:HEAD + ONES_ROWS, :] = jnp.where(row == 0, 1.0, 0.0).astype(BF16)

    q = (q_ref[0].astype(F32) * (QK ** -0.5 * LOG2E)).astype(BF16)
    lane = lax.broadcasted_iota(jnp.int32, q.shape, 1)
    zero = jnp.zeros_like(q)
    qq_scr[0:t, :] = jnp.where(lane < QK, q, zero)
    qq_scr[t:2 * t, :] = jnp.where(lane >= QK, q, zero)
    m_scr[...] = jnp.full(m_scr.shape, NEG, F32)
    acc_scr[...] = jnp.zeros(acc_scr.shape, F32)

    tiles = [slice(ct * cw, (ct + 1) * cw) for ct in range(2 * t // cw)]

    bufs = (s0_scr, s1_scr)

    nt_dims = (((1,), (1,)), ((), ()))
    hk = t // 2
    diag_parts = [(k0, k1, slice(mi * t + k0, (mi + 1) * t)) for mi in range(2) for k0, k1 in ((0, hk), (hk, t))]

    def scores_into(j, buf):
        kb = k_ref[0, pl.ds(pl.multiple_of(j * t, t), t), :]
        for cs in tiles:
            buf[:, cs] = lax.dot_general(kb, qq_scr[cs, :], nt_dims, preferred_element_type=F32)

    def diag_scores_into(j, buf):
        for k0, k1, cs in diag_parts:
            kb = k_ref[0, pl.ds(pl.multiple_of(j * t + k0, hk), k1 - k0), :]
            buf[k0:k1, cs] = lax.dot_general(kb, qq_scr[cs, :], nt_dims, preferred_element_type=F32)

    def consume_part(j, buf, k0, k1, cs, bias_of):
        vt = vt_scr[:, pl.ds(pl.multiple_of(j * t + k0, hk), k1 - k0)]
        s = buf[k0:k1, cs]
        if bias_of is not None:
            s = s + bias_of(k0, k1, cs)
        m_prev = m_scr[:, cs]
        m_new = jnp.maximum(m_prev, jnp.max(s, axis=0, keepdims=True))
        alpha = jnp.exp2(m_prev - m_new)
        p = jnp.exp2(s - m_new)
        acc_scr[:, cs] = alpha * acc_scr[:, cs] + jnp.dot(vt, p.astype(BF16), preferred_element_type=F32)
        m_scr[:, cs] = m_new

    def consume(j, buf, bias_of):
        for cs in tiles:
            consume_part(j, buf, 0, t, cs, bias_of)

    def diag_consume(j, buf):
        for k0, k1, cs in diag_parts:
            consume_part(j, buf, k0, k1, cs, lambda a, b, c: bias_ref[0, 0, a:b, c])

    near = lambda a, b, c: bias_ref[0, 1, a:b, c]

    @pl.when(qi == 0)
    def _():
        diag_scores_into(0, bufs[0])

    @pl.when(qi > 0)
    def _():
        scores_into(0, bufs[0])

    n_far = jnp.maximum(qi - 1, 0)

    def far_pair(pi, carry):
        j = 2 * pi
        scores_into(j + 1, bufs[1])
        consume(j, bufs[0], None)
        scores_into(j + 2, bufs[0])
        consume(j + 1, bufs[1], None)
        return carry

    lax.fori_loop(0, n_far // 2, far_pair, 0)

    @pl.when(qi == 0)
    def _():
        diag_consume(0, bufs[0])

    @pl.when(jnp.logical_and(qi >= 1, lax.rem(qi, 2) == 1))
    def _():
        diag_scores_into(qi, bufs[1])
        consume(qi - 1, bufs[0], near)
        diag_consume(qi, bufs[1])

    @pl.when(jnp.logical_and(qi >= 2, lax.rem(qi, 2) == 0))
    def _():
        scores_into(qi - 1, bufs[1])
        consume(qi - 2, bufs[0], None)
        diag_scores_into(qi, bufs[0])
        consume(qi - 1, bufs[1], near)
        diag_consume(qi, bufs[0])

    inv = 1.0 / acc_scr[HEAD:HEAD + 1, :]
    o = acc_scr[0:HEAD, 0:t] * inv[:, 0:t] - acc_scr[0:HEAD, t:2 * t] * (sc_ref[0] * inv[:, t:2 * t])
    ms = jnp.mean(o * o, axis=0, keepdims=True)
    o = (o * lax.rsqrt(ms + EPS)).T
    o_ref[0] = ((o * g_ref[0]) * sc_ref[1]).astype(BF16)


def _attention(proj, bias, scalars, subln_all, layer, t=ATTN_T):
    b, s, _ = proj.shape
    nq = s // t
    return pl.pallas_call(
        functools.partial(_attn_body, t=t, cw=ATTN_CW),
        grid=(b, N_HEADS, nq),
        in_specs=[pl.BlockSpec(memory_space=pltpu.SMEM),
                  pl.BlockSpec((1, t, HEAD), lambda bi, h, i: (bi, i, h)),
                  pl.BlockSpec((1, s, HEAD), lambda bi, h, i: (bi, 0, N_HEADS + h)),
                  pl.BlockSpec((1, s, HEAD), lambda bi, h, i: (bi, 0, 2 * N_HEADS + h)),
                  pl.BlockSpec((1, 2, t, 2 * t), lambda bi, h, i: (h, 0, 0, 0)),
                  pl.BlockSpec((1, 1, HEAD), lambda bi, h, i: (layer, 0, 0))],
        out_specs=pl.BlockSpec((1, t, HEAD), lambda bi, h, i: (bi, i, h)),
        out_shape=jax.ShapeDtypeStruct((b, s, N_HEADS * HEAD), BF16),
        scratch_shapes=[pltpu.VMEM((2 * t, HEAD), BF16),
                        pltpu.VMEM((HEAD + ONES_ROWS, s), BF16),
                        pltpu.VMEM((1, 2 * t), F32),
                        pltpu.VMEM((HEAD + ONES_ROWS, 2 * t), F32),
                        pltpu.VMEM((t, 2 * t), F32),
                        pltpu.VMEM((t, 2 * t), F32)],
        compiler_params=_cparams(3),
        name="diff_attention",
    )(scalars, proj, proj, proj, bias, subln_all)


def _hgrn_levels(c, sub):
    levels = []
    g = c // 2
    while g >= sub:
        levels.append(g)
        g //= 2
    return levels


def _hgrn_consts(l, c, sub):
    r = np.arange(HGRN_GROUP)[:, None]
    s = np.arange(HGRN_GROUP)[None, :]
    lcum = ((r // c == s // c) & (s <= r)).astype(np.float32)
    rc = np.arange(c)[:, None]
    sc = np.arange(c)[None, :]
    masks = [((rc // (2 * g) == sc // (2 * g)) & (rc % (2 * g) >= g) & (sc % (2 * g) < g)).astype(np.float32)
             for g in _hgrn_levels(c, sub)]
    half = c // 2
    rg = np.arange(HGRN_GROUP)[:, None]
    sg = np.arange(HGRN_GROUP)[None, :]
    top = (rg // c == sg // c) & (rg % c >= half) & (sg % c < half)
    near = (rg // half == sg // half) & (sg <= rg)
    fmask = np.stack([top, near]).astype(np.float32)
    return jnp.asarray(lcum, BF16), jnp.asarray(np.stack(masks), F32), jnp.asarray(fmask, F32)


def _hgrn_body(q_ref, f_ref, i_ref, g_ref, lb_ref, ng_ref, lcum_ref, mask_ref, fmask_ref, o_ref,
               st_scr, stb_scr, a_scr, oi_scr, *, l, c, sub):
    nchunk = l // c
    levels = _hgrn_levels(c, sub)

    @pl.when(pl.program_id(2) == 0)
    def _():
        st_scr[...] = jnp.zeros(st_scr.shape, F32)

    q = q_ref[0].astype(F32)
    fr = f_ref[0].astype(F32)
    vb = i_ref[0]
    lb = lb_ref[0]
    f = lb + (1.0 - lb) * jax.nn.sigmoid(fr)
    logf = jnp.log(jnp.maximum(f, jnp.finfo(F32).tiny))
    k = (1.0 - lb) * jax.nn.sigmoid(-fr)

    hi = logf.astype(BF16)
    lo = (logf - hi.astype(F32)).astype(BF16)
    hilo = jnp.concatenate([hi, lo], axis=1)
    lcum = lcum_ref[...]
    grp = lcum.shape[0]
    parts = []
    for gi in range(l // grp):
        both = jnp.dot(lcum, hilo[gi * grp:(gi + 1) * grp], preferred_element_type=F32)
        parts.append(both[:, 0:HEAD] + both[:, HEAD:2 * HEAD])
    bcum = jnp.concatenate(parts, axis=0)

    def rows_of(arr, group, row):
        a3 = arr.reshape(l // group, group, HEAD)
        return jnp.broadcast_to(a3[:, row:row + 1, :], a3.shape).reshape(l, HEAD)

    nt = (((1,), (1,)), ((), ()))
    tn = (((0,), (0,)), ((), ()))

    def level_operands(g):
        ref = rows_of(bcum, 2 * g, g - 1)
        return ((q * jnp.exp(jnp.minimum(bcum - ref, 0.0))).astype(BF16),
                (k * jnp.exp(jnp.minimum(ref - bcum, 0.0))).astype(BF16))

    q_top, k_top = level_operands(levels[0])

    def top_scores(sl):
        return mask_ref[0] * lax.dot_general(q_top[sl], k_top[sl], nt, preferred_element_type=F32)

    half = c // 2
    b_start = rows_of(bcum, half, 0) - rows_of(logf, half, 0)
    span = b_start - bcum
    bounded = jnp.max(span) <= HGRN_SAFE_SPAN

    @pl.when(bounded)
    def _():
        q_f = (q * jnp.exp(-span)).astype(BF16)
        k_f = (k * jnp.exp(span)).astype(BF16)
        grp = fmask_ref.shape[-1]
        for gi in range(l // grp):
            sl = slice(gi * grp, (gi + 1) * grp)
            a = fmask_ref[0] * lax.dot_general(q_top[sl], k_top[sl], nt, preferred_element_type=F32)
            a = a + fmask_ref[1] * lax.dot_general(q_f[sl], k_f[sl], nt, preferred_element_type=F32)
            a_scr[sl, :] = jnp.dot(a.astype(BF16), vb[sl], preferred_element_type=F32)

    @pl.when(jnp.logical_not(bounded))
    def _():
        lower = [level_operands(g) for g in levels[1:]]
        nb = l // sub
        b3 = bcum.reshape(nb, sub, HEAD)
        q3 = q.reshape(nb, sub, HEAD)
        k3 = k.reshape(nb, sub, HEAD)
        v3 = vb.astype(F32).reshape(nb, sub, HEAD)
        tloc = lax.broadcasted_iota(jnp.int32, (nb, sub, 1), 1)
        od = jnp.zeros((nb, sub, HEAD), F32)
        for si in range(sub):
            e = jnp.exp(b3 - b3[:, si:si + 1, :])
            pr = q3 * (k3[:, si:si + 1, :] * e)
            w = jnp.sum(pr, axis=-1, keepdims=True)
            w = jnp.where(tloc >= si, w, 0.0)
            od = od + w * v3[:, si:si + 1, :]
        od = od.reshape(l, HEAD)
        for ci in range(nchunk):
            sl = slice(ci * c, (ci + 1) * c)
            a = top_scores(sl)
            for li, (q_g, k_g) in enumerate(lower):
                a = a + mask_ref[li + 1] * lax.dot_general(q_g[sl], k_g[sl], nt, preferred_element_type=F32)
            a_scr[sl, :] = jnp.dot(a.astype(BF16), vb[sl], preferred_element_type=F32) + od[sl]

    b_end = rows_of(bcum, c, c - 1)
    q_in = (q * jnp.exp(bcum)).astype(BF16)
    k_out = (k * jnp.exp(b_end - bcum)).astype(BF16)
    dec = jnp.exp(b_end)
    kv = [lax.dot_general(vb[ci * c:(ci + 1) * c], k_out[ci * c:(ci + 1) * c], tn, preferred_element_type=F32)
          for ci in range(nchunk)]
    st = st_scr[...]
    for ci in range(nchunk):
        stb_scr[ci] = st.astype(BF16)
        st = st * dec[ci * c:ci * c + 1, :] + kv[ci]
    st_scr[...] = st
    for ci in range(nchunk):
        sl = slice(ci * c, (ci + 1) * c)
        oi_scr[sl, :] = lax.dot_general(q_in[sl], stb_scr[ci], nt, preferred_element_type=F32)

    o = a_scr[...] + oi_scr[...]
    ms = jnp.mean(o * o, axis=-1, keepdims=True)
    o = o * lax.rsqrt(ms + EPS) * ng_ref[0]
    gate = g_ref[0].astype(F32)
    o_ref[0] = (o * (gate * jax.nn.sigmoid(gate))).astype(BF16)


def _hgrn(proj, lower_bounds, norm_all, layer, l=HGRN_L, c=HGRN_C, sub=HGRN_SUB):
    b, s, _ = proj.shape
    lcum, masks, fmask = _hgrn_consts(l, c, sub)
    base = 3 * N_HEADS

    def col(kind):
        return pl.BlockSpec((1, l, HEAD), lambda bi, h, i: (bi, i, base + kind * N_HEADS + h))

    return pl.pallas_call(
        functools.partial(_hgrn_body, l=l, c=c, sub=sub),
        grid=(b, N_HEADS, s // l),
        in_specs=[col(0), col(1), col(2), col(3),
                  pl.BlockSpec((1, 1, HEAD), lambda bi, h, i: (layer, 0, h)),
                  pl.BlockSpec((1, 1, HEAD), lambda bi, h, i: (layer, 0, 0)),
                  pl.BlockSpec(lcum.shape, lambda bi, h, i: (0, 0)),
                  pl.BlockSpec(masks.shape, lambda bi, h, i: (0, 0, 0)),
                  pl.BlockSpec(fmask.shape, lambda bi, h, i: (0, 0, 0))],
        out_specs=pl.BlockSpec((1, l, HEAD), lambda bi, h, i: (bi, i, h)),
        out_shape=jax.ShapeDtypeStruct((b, s, N_HEADS * HEAD), BF16),
        scratch_shapes=[pltpu.VMEM((HEAD, HEAD), F32),
                        pltpu.VMEM((l // c, HEAD, HEAD), BF16),
                        pltpu.VMEM((l, HEAD), F32),
                        pltpu.VMEM((l, HEAD), F32)],
        compiler_params=_cparams(3),
        name="hgrn2",
    )(proj, proj, proj, proj, lower_bounds, norm_all, lcum, masks, fmask)


def _first_row(cond, row):
    return jnp.min(jnp.where(cond, row, N_EXPERTS), axis=0, keepdims=True)


def _outproj_body(a_ref, hg_ref, w_ref, x_ref, mod_ref, g_ref, wrh_ref, wrl_ref, br_ref, xo_ref, h_ref, *, d):
    da = a_ref.shape[-1]
    mixed = jnp.dot(a_ref[0], w_ref[0, 0:da, :], preferred_element_type=F32)
    mixed = mixed + jnp.dot(hg_ref[0], w_ref[0, da:, :], preferred_element_type=F32)
    x = x_ref[0] + mod_ref[0, 2:3, :] * mixed
    xo_ref[0] = x
    h = _modulated_norm(x, g_ref[0], mod_ref[0, 4:5, :], mod_ref[0, 3:4, :])
    h_ref[0, :, 0:d] = h

    h_hi = h.astype(BF16)
    h_lo = (h - h_hi.astype(F32)).astype(BF16)
    logits = (jnp.dot(h_hi, wrh_ref[...], preferred_element_type=F32)
              + jnp.dot(h_lo, wrh_ref[...], preferred_element_type=F32)
              + jnp.dot(h_hi, wrl_ref[...], preferred_element_type=F32)) + br_ref[...]
    lt = logits.T[0:N_EXPERTS, :]
    erow = lax.broadcasted_iota(jnp.int32, lt.shape, 0)
    mx = jnp.max(lt, axis=0, keepdims=True)
    ex = jnp.exp(lt - mx)
    probs = ex / jnp.sum(ex, axis=0, keepdims=True)

    best = sel = v1 = v2 = i1 = i2 = None
    for gi in range(N_GROUPS):
        ing = (erow >= gi * GROUP_SIZE) & (erow < (gi + 1) * GROUP_SIZE)
        pg = jnp.where(ing, probs, -1.0)
        m1 = jnp.max(pg, axis=0, keepdims=True)
        a1 = _first_row(pg == m1, erow)
        pg2 = jnp.where(erow == a1, -1.0, pg)
        m2 = jnp.max(pg2, axis=0, keepdims=True)
        a2 = _first_row(pg2 == m2, erow)
        score = m1 + m2
        if gi == 0:
            best, sel, v1, v2, i1, i2 = score, jnp.zeros_like(a1), m1, m2, a1, a2
        else:
            better = score > best
            best = jnp.where(better, score, best)
            sel = jnp.where(better, gi, sel)
            v1 = jnp.where(better, m1, v1)
            v2 = jnp.where(better, m2, v2)
            i1 = jnp.where(better, a1, i1)
            i2 = jnp.where(better, a2, i2)
    tot = v1 + v2
    w1 = v1 / tot
    w2 = v2 / tot
    loc1 = i1 - sel * GROUP_SIZE
    loc2 = i2 - sel * GROUP_SIZE
    xrow = lax.broadcasted_iota(jnp.int32, (LANES, lt.shape[1]), 0)
    ext = jnp.where(xrow == loc1, w1, 0.0) + jnp.where(xrow == loc2, w2, 0.0)
    ext = jnp.where(xrow == GROUP_SIZE, sel.astype(F32), ext)
    h_ref[0, :, d:] = ext.T


def _outproj_router(attn_o, hgrn_o, w_out_all, x, mod, ffn_norm_all, wr_hi, wr_lo, br, layer, tm=512):
    b, s, d = x.shape
    da = attn_o.shape[-1]
    dmix = w_out_all.shape[1]
    tok = lambda bi, i: (bi, i, 0)
    return pl.pallas_call(
        functools.partial(_outproj_body, d=d),
        grid=(b, s // tm),
        in_specs=[pl.BlockSpec((1, tm, da), tok),
                  pl.BlockSpec((1, tm, dmix - da), tok),
                  pl.BlockSpec((1, dmix, d), lambda bi, i: (layer, 0, 0)),
                  pl.BlockSpec((1, tm, d), tok),
                  pl.BlockSpec((1, N_MOD, d), lambda bi, i: (bi, 0, 0)),
                  pl.BlockSpec((1, 1, d), lambda bi, i: (layer, 0, 0)),
                  pl.BlockSpec((d, LANES), lambda bi, i: (0, 0)),
                  pl.BlockSpec((d, LANES), lambda bi, i: (0, 0)),
                  pl.BlockSpec((1, LANES), lambda bi, i: (0, 0))],
        out_specs=[pl.BlockSpec((1, tm, d), tok),
                   pl.BlockSpec((1, tm, d + LANES), tok)],
        out_shape=[jax.ShapeDtypeStruct((b, s, d), F32),
                   jax.ShapeDtypeStruct((b, s, d + LANES), F32)],
        compiler_params=_cparams(2),
        name="outproj_router",
    )(attn_o, hgrn_o, w_out_all, x, mod, ffn_norm_all, wr_hi, wr_lo, br)


def _moe_body(tg_ref, nv_ref, nu_ref, tok_ref, h_hbm, wg_ref, wu_ref, wd_ref, y_hbm,
              xbuf, xbf, acc, gsem, ssem, *, tm, d):
    i = pl.program_id(0)
    e = pl.program_id(1)
    n_used = nu_ref[0]
    slot = lax.rem(i, 2)

    def gather_start(tile, sl):
        base = tile * tm

        def body(r, carry):
            pltpu.make_async_copy(h_hbm.at[pl.ds(tok_ref[base + r], 1)], xbuf.at[sl, pl.ds(r, 1)], gsem.at[sl]).start()
            return carry

        lax.fori_loop(0, tm, body, 0, unroll=MOE_ISSUE_UNROLL)

    def gather_wait(sl):
        pltpu.make_async_copy(h_hbm.at[pl.ds(0, tm)], xbuf.at[sl], gsem.at[sl]).wait()

    def scatter_start(tile, sl, n):
        base = tile * tm

        def body(r, carry):
            pltpu.make_async_copy(acc.at[sl, pl.ds(r, 1)], y_hbm.at[pl.ds(tok_ref[base + r], 1)], ssem.at[sl]).start()
            return carry

        @pl.when(n == tm)
        def _():
            lax.fori_loop(0, tm, body, 0, unroll=MOE_ISSUE_UNROLL)

        @pl.when(n < tm)
        def _():
            lax.fori_loop(0, n, body, 0)

    def scatter_wait(sl, n):
        @pl.when(n == tm)
        def _():
            pltpu.make_async_copy(acc.at[sl], y_hbm.at[pl.ds(0, tm)], ssem.at[sl]).wait()

        @pl.when(n < tm)
        def _():
            def body(r, carry):
                pltpu.make_async_copy(acc.at[sl, pl.ds(r, 1)], y_hbm.at[pl.ds(r, 1)], ssem.at[sl]).wait()
                return carry

            lax.fori_loop(0, n, body, 0)

    @pl.when(i < n_used)
    def _():
        @pl.when(e == 0)
        def _():
            @pl.when(i == 0)
            def _():
                gather_start(0, 0)

            gather_wait(slot)

            @pl.when(i + 1 < n_used)
            def _():
                gather_start(i + 1, 1 - slot)

            xbf[...] = xbuf[slot, :, 0:d].astype(BF16)

            @pl.when(i >= 2)
            def _():
                scatter_wait(slot, nv_ref[jnp.maximum(i - 2, 0)])

            acc[slot] = jnp.zeros((tm, d), F32)

        ext = xbuf[slot, :, d:d + LANES]
        gate = jnp.zeros((tm, 1), F32)
        for ei in range(GROUP_SIZE):
            gate = jnp.where(e == ei, ext[:, ei:ei + 1], gate)
        x = xbf[...]
        hg = jnp.dot(x, wg_ref[0, 0], preferred_element_type=F32)
        hu = jnp.dot(x, wu_ref[0, 0], preferred_element_type=F32)
        act = (hg * jax.nn.sigmoid(hg)) * hu * gate
        acc[slot] += jnp.dot(act.astype(BF16), wd_ref[0, 0], preferred_element_type=F32)

        @pl.when(e == GROUP_SIZE - 1)
        def _():
            scatter_start(i, slot, nv_ref[i])

            @pl.when(i == n_used - 1)
            def _():
                scatter_wait(slot, nv_ref[i])

                @pl.when(i >= 1)
                def _():
                    scatter_wait(1 - slot, nv_ref[jnp.maximum(i - 1, 0)])


def _route(hext, d, tm):
    b, s, _ = hext.shape
    t = b * s
    n_tiles = t // tm + N_GROUPS
    gid = hext[:, :, d + GROUP_SIZE].reshape(t).astype(jnp.int32)
    onehot = (gid[:, None] == jnp.arange(N_GROUPS, dtype=jnp.int32)[None, :]).astype(jnp.int32)
    cnt = jnp.sum(onehot, axis=0)
    rank = jnp.sum(onehot * (jnp.cumsum(onehot, axis=0) - onehot), axis=1)
    ntile_g = (cnt + tm - 1) // tm
    tile_end_g = jnp.cumsum(ntile_g)
    tile_start_g = tile_end_g - ntile_g
    pos = (tile_start_g * tm)[gid] + rank
    tok_of_row = jnp.zeros((n_tiles * tm,), jnp.int32).at[pos].set(jnp.arange(t, dtype=jnp.int32))
    tile = jnp.arange(n_tiles, dtype=jnp.int32)
    tile_gid = jnp.minimum(jnp.sum((tile[:, None] >= tile_end_g[None, :]).astype(jnp.int32), axis=1), N_GROUPS - 1)
    nvalid = jnp.clip(cnt[tile_gid] - (tile - tile_start_g[tile_gid]) * tm, 0, tm).astype(jnp.int32)
    n_used = tile_end_g[N_GROUPS - 1:].astype(jnp.int32)
    return tile_gid.astype(jnp.int32), nvalid, n_used, tok_of_row


def _moe(hext, wg_all, wu_all, wd_all, layer, tm=MOE_TM):
    b, s, dx = hext.shape
    d = dx - LANES
    t = b * s
    f = wg_all.shape[-1]
    tile_gid, nvalid, n_used, tok_of_row = _route(hext, d, tm)
    n_tiles = tile_gid.shape[0]

    def wmap(i, e, tg, nv, nu, tok):
        return (layer, tg[i] * GROUP_SIZE + e, 0, 0)

    return pl.pallas_call(
        functools.partial(_moe_body, tm=tm, d=d),
        grid_spec=pltpu.PrefetchScalarGridSpec(
            num_scalar_prefetch=4,
            grid=(n_tiles, GROUP_SIZE),
            in_specs=[pl.BlockSpec(memory_space=pl.ANY),
                      pl.BlockSpec((1, 1, d, f), wmap),
                      pl.BlockSpec((1, 1, d, f), wmap),
                      pl.BlockSpec((1, 1, f, d), wmap)],
            out_specs=pl.BlockSpec(memory_space=pl.ANY),
            scratch_shapes=[pltpu.VMEM((2, tm, dx), F32),
                            pltpu.VMEM((tm, d), BF16),
                            pltpu.VMEM((2, tm, d), F32),
                            pltpu.SemaphoreType.DMA((2,)),
                            pltpu.SemaphoreType.DMA((2,))],
        ),
        out_shape=jax.ShapeDtypeStruct((t, d), F32),
        compiler_params=_cparams(2),
        name="moe_experts",
    )(tile_gid, nvalid, n_used, tok_of_row, hext.reshape(t, dx), wg_all, wu_all, wd_all)


def _final_body(x_ref, y_ref, mod_ref, g_ref, o_ref):
    x = x_ref[0] + mod_ref[0, 5:6, :] * y_ref[...]
    ms = jnp.mean(x * x, axis=-1, keepdims=True)
    o_ref[0] = x * lax.rsqrt(ms + EPS) * g_ref[...]


def _final(x, y, mod, g, tm=512):
    b, s, d = x.shape
    nt = s // tm
    return pl.pallas_call(
        _final_body,
        grid=(b, nt),
        in_specs=[pl.BlockSpec((1, tm, d), lambda bi, i: (bi, i, 0)),
                  pl.BlockSpec((tm, d), lambda bi, i: (bi * nt + i, 0)),
                  pl.BlockSpec((1, N_MOD, d), lambda bi, i: (bi, 0, 0)),
                  pl.BlockSpec((1, d), lambda bi, i: (0, 0))],
        out_specs=pl.BlockSpec((1, tm, d), lambda bi, i: (bi, i, 0)),
        out_shape=jax.ShapeDtypeStruct(x.shape, F32),
        compiler_params=_cparams(2),
        name="final_norm",
    )(x, y, mod, g.reshape(1, d))


def kernel(x, c, w_in, w_out, attn_norm, ffn_norm, w_ada, b_ada, lambda_q1, lambda_k1, lambda_q2, lambda_k2,
           diff_subln, hgrn_lb, hgrn_norm, rel_bias, w_router, b_router, w_gate, w_up, w_down, final_norm):
    b, s, d = x.shape
    depth = w_in.shape[0]

    w_in_b = w_in.astype(BF16)
    w_out_b = w_out.astype(BF16)
    w_gate_b = w_gate.astype(BF16)
    w_up_b = w_up.astype(BF16)
    w_down_b = w_down.astype(BF16)
    wr = jnp.pad(w_router.astype(F32), ((0, 0), (0, LANES - N_EXPERTS)))
    wr_hi = wr.astype(BF16)
    wr_lo = (wr - wr_hi.astype(F32)).astype(BF16)
    br = jnp.pad(b_router.astype(F32), (0, LANES - N_EXPERTS)).reshape(1, LANES)
    lb_soft = jax.nn.softmax(hgrn_lb.astype(F32), axis=0)
    lower_bounds = jnp.maximum(jnp.cumsum(lb_soft, axis=0) - lb_soft[0:1], 0.0).reshape(depth, 1, -1)
    attn_norm3 = attn_norm.astype(F32).reshape(depth, 1, d)
    ffn_norm3 = ffn_norm.astype(F32).reshape(depth, 1, d)
    subln3 = diff_subln.astype(F32).reshape(depth, 1, HEAD)
    hnorm3 = hgrn_norm.astype(F32).reshape(depth, 1, HEAD)

    c8 = jnp.pad(c.astype(F32), ((0, 8 - b), (0, 0)))
    mods = _ada(c8, w_ada, b_ada)
    bias = _bias_tiles(rel_bias, ATTN_T)

    y = None
    mod_prev = None
    for layer in range(depth):
        mod = mods[layer, :b].reshape(b, N_MOD, d)
        lambda_init = 0.8 - 0.6 * math.exp(-0.3 * layer)
        lam = (jnp.exp(jnp.sum(lambda_q1[layer].astype(F32) * lambda_k1[layer].astype(F32)))
               - jnp.exp(jnp.sum(lambda_q2[layer].astype(F32) * lambda_k2[layer].astype(F32)))
               + lambda_init)
        scalars = jnp.stack([lam, jnp.asarray(1.0 - lambda_init, F32)]).astype(F32)

        x, proj = _inproj(x, y, mod_prev, mod, attn_norm3, w_in_b, layer)
        attn_o = _attention(proj, bias, scalars, subln3, layer)
        hgrn_o = _hgrn(proj, lower_bounds, hnorm3, layer)
        x, hext = _outproj_router(attn_o, hgrn_o, w_out_b, x, mod, ffn_norm3, wr_hi, wr_lo, br, layer)
        y = _moe(hext, w_gate_b, w_up_b, w_down_b, layer)
        mod_prev = mod
    return _final(x, y, mod_prev, final_norm.astype(F32))
```

```python
import functools
import math

import numpy as np
import jax
import jax.numpy as jnp
from jax import lax
from jax.experimental import pallas as pl
from jax.experimental.pallas import tpu as pltpu

F32 = jnp.float32
BF16 = jnp.bfloat16
EPS = 1e-6

LANES = 128
HEAD = 128
QK = 64
N_HEADS = 8
NUM_BUCKETS = 32
MAX_DISTANCE = 128
N_EXPERTS = 16
N_GROUPS = 4
GROUP_SIZE = N_EXPERTS // N_GROUPS
N_MOD = 6
NEG = -1e30

ATTN_T = 512
ATTN_CW = 512
ONES_ROWS = 16
LOG2E = math.log2(math.e)
HGRN_L = 2048
HGRN_C = 64
HGRN_SUB = 8
HGRN_GROUP = 256
HGRN_SAFE_SPAN = 80.0
NORM_ROWS = 256
MOE_TM = 512
MOE_ISSUE_UNROLL = MOE_TM
VMEM_LIMIT = 56 * 1024 * 1024


def _cparams(n_axes):
    return pltpu.CompilerParams(dimension_semantics=("arbitrary",) * n_axes, vmem_limit_bytes=VMEM_LIMIT)


def _ada_body(c_ref, w_ref, b_ref, o_ref):
    c = c_ref[...]
    ca = (c * jax.nn.sigmoid(c)).astype(BF16)
    o_ref[0] = jnp.dot(ca, w_ref[0].astype(BF16), preferred_element_type=F32) + b_ref[0]


def _ada(c8, w_ada, b_ada, tn=1024):
    depth, d, n = w_ada.shape
    return pl.pallas_call(
        _ada_body,
        grid=(depth, n // tn),
        in_specs=[pl.BlockSpec((8, d), lambda l, j: (0, 0)),
                  pl.BlockSpec((1, d, tn), lambda l, j: (l, 0, j)),
                  pl.BlockSpec((1, 1, tn), lambda l, j: (l, 0, j))],
        out_specs=pl.BlockSpec((1, 8, tn), lambda l, j: (l, 0, j)),
        out_shape=jax.ShapeDtypeStruct((depth, 8, n), F32),
        compiler_params=_cparams(2),
        name="ada_mod",
    )(c8, w_ada, b_ada.reshape(depth, 1, n))


def _modulated_norm(x, g, scale, shift):
    ms = jnp.mean(x * x, axis=-1, keepdims=True)
    return (x * lax.rsqrt(ms + EPS) * g) * (1.0 + scale) + shift


def _inproj_body(*refs, has_res):
    if has_res:
        x_ref, y_ref, gp_ref, mod_ref, g_ref, w_ref, xo_ref, p_ref, h_scr = refs
    else:
        x_ref, mod_ref, g_ref, w_ref, p_ref, h_scr = refs

    @pl.when(pl.program_id(2) == 0)
    def _():
        tm = h_scr.shape[0]
        for r0 in range(0, tm, NORM_ROWS):
            rows = slice(r0, r0 + NORM_ROWS)
            x = x_ref[0, rows, :]
            if has_res:
                x = x + gp_ref[0, 5:6, :] * y_ref[rows, :]
                xo_ref[0, rows, :] = x
            h = _modulated_norm(x, g_ref[0], mod_ref[0, 1:2, :], mod_ref[0, 0:1, :])
            h_scr[rows, :] = h.astype(BF16)

    p_ref[0] = jnp.dot(h_scr[...], w_ref[0], preferred_element_type=F32).astype(BF16)


def _inproj(x, y, mod_prev, mod, g_all, w, layer, tm=1024, tn=1024):
    w_all = w[None]
    b, s, d = x.shape
    n = w_all.shape[-1]
    nt = s // tm
    has_res = y is not None
    xspec = pl.BlockSpec((1, tm, d), lambda bi, i, j: (bi, i, 0))
    modspec = pl.BlockSpec((1, N_MOD, d), lambda bi, i, j: (bi, 0, 0))
    once = dict(pipeline_mode=pl.Buffered(1))
    in_specs = [pl.BlockSpec((1, tm, d), lambda bi, i, j: (bi, i, 0), **once)]
    args = [x]
    if has_res:
        in_specs += [pl.BlockSpec((tm, d), lambda bi, i, j: (bi * nt + i, 0), **once), modspec]
        args += [y, mod_prev]
    in_specs += [modspec,
                 pl.BlockSpec((1, 1, d), lambda bi, i, j: (layer, 0, 0)),
                 pl.BlockSpec((1, d, tn), lambda bi, i, j: (0, 0, j))]
    args += [mod, g_all, w_all]
    pspec = pl.BlockSpec((1, tm, tn), lambda bi, i, j: (bi, i, j))
    pshape = jax.ShapeDtypeStruct((b, s, n), BF16)
    if has_res:
        out_specs, out_shape = [xspec, pspec], [jax.ShapeDtypeStruct(x.shape, F32), pshape]
    else:
        out_specs, out_shape = pspec, pshape
    out = pl.pallas_call(
        functools.partial(_inproj_body, has_res=has_res),
        grid=(b, nt, n // tn),
        in_specs=in_specs, out_specs=out_specs, out_shape=out_shape,
        scratch_shapes=[pltpu.VMEM((tm, d), BF16)],
        compiler_params=_cparams(3),
        name="inproj",
    )(*args)
    return (out[0], out[1]) if has_res else (x, out)


def _t5_bucket_np(n):
    max_exact = NUM_BUCKETS // 2
    nf = np.maximum(n, 1).astype(np.float32)
    large = max_exact + (np.log(nf / np.float32(max_exact)) / np.float32(math.log(MAX_DISTANCE / max_exact))
                         * np.float32(NUM_BUCKETS - max_exact)).astype(np.int32)
    large = np.minimum(large, NUM_BUCKETS - 1)
    return np.where(n < max_exact, n, large).astype(np.int32)


def _bias_body(tab_ref, bd_ref, bu_ref, o_ref):
    t = bd_ref.shape[0]
    tab = jnp.broadcast_to(tab_ref[0], (t, LANES))
    for ti, idx_ref in enumerate((bd_ref, bu_ref)):
        for cb in range(t // LANES):
            cols = slice(cb * LANES, (cb + 1) * LANES)
            idx = idx_ref[:, cols]
            val = jnp.take_along_axis(tab, jnp.maximum(idx, 0), axis=1)
            o_ref[0, ti, :, cols] = jnp.where(idx < 0, NEG, val)


def _bias_tiles(rel_bias, t):
    key = np.arange(t)[:, None]
    qry = np.arange(t)[None, :]
    bd = np.where(qry >= key, _t5_bucket_np(np.maximum(qry - key, 0)), -1).astype(np.int32)
    bu = _t5_bucket_np(t + qry - key)
    assert MAX_DISTANCE <= t + 1, "tiles two or more to the left must lie in the far bucket"
    tab = (rel_bias.astype(F32) - rel_bias[NUM_BUCKETS - 1:].astype(F32)) * LOG2E
    tab = jnp.pad(tab.T, ((0, 0), (0, LANES - NUM_BUCKETS))).reshape(2 * N_HEADS, 1, LANES)
    return pl.pallas_call(
        _bias_body,
        grid=(N_HEADS, 2),
        in_specs=[pl.BlockSpec((1, 1, LANES), lambda h, m: (2 * h + m, 0, 0)),
                  pl.BlockSpec((t, t), lambda h, m: (0, 0)),
                  pl.BlockSpec((t, t), lambda h, m: (0, 0))],
        out_specs=pl.BlockSpec((1, 2, t, t), lambda h, m: (h, 0, 0, m)),
        out_shape=jax.ShapeDtypeStruct((N_HEADS, 2, t, 2 * t), F32),
        compiler_params=_cparams(2),
        name="bias_tiles",
    )(tab, jnp.asarray(bd), jnp.asarray(bu))


def _attn_body(*refs, t, cw, n_cast):
    sc_ref, q_ref, k_ref, v_ref, bias_ref, g_ref = refs[:6]
    w_refs = refs[6:6 + n_cast]
    o_ref = refs[6 + n_cast]
    wb_refs = refs[7 + n_cast:7 + 2 * n_cast]
    qq_scr, vt_scr, m_scr, acc_scr, s0_scr, s1_scr = refs[7 + 2 * n_cast:]
    for w_ref, wb_ref in zip(w_refs, wb_refs):
        wb_ref[0] = w_ref[0, 0].astype(BF16)

    qi = pl.program_id(2)
    s_len = k_ref.shape[1]

    @pl.when(qi == 0)
    def _():
        for ci in range(s_len // t):
            vt_scr[0:HEAD, ci * t:(ci + 1) * t] = v_ref[0, ci * t:(ci + 1) * t, :].T
        row = lax.broadcasted_iota(jnp.int32, (ONES_ROWS, s_len), 0)
        vt_scr[HEAD:HEAD + ONES_ROWS, :] = jnp.where(row == 0, 1.0, 0.0).astype(BF16)

    q = (q_ref[0].astype(F32) * (QK ** -0.5 * LOG2E)).astype(BF16)
    lane = lax.broadcasted_iota(jnp.int32, q.shape, 1)
    zero = jnp.zeros_like(q)
    qq_scr[0:t, :] = jnp.where(lane < QK, q, zero)
    qq_scr[t:2 * t, :] = jnp.where(lane >= QK, q, zero)
    m_scr[...] = jnp.full(m_scr.shape, NEG, F32)
    acc_scr[...] = jnp.zeros(acc_scr.shape, F32)

    tiles = [slice(ct * cw, (ct + 1) * cw) for ct in range(2 * t // cw)]

    bufs = (s0_scr, s1_scr)

    nt_dims = (((1,), (1,)), ((), ()))
    hk = t // 2
    diag_parts = [(k0, k1, slice(mi * t + k0, (mi + 1) * t)) for mi in range(2) for k0, k1 in ((0, hk), (hk, t))]

    def scores_into(j, buf):
        kb = k_ref[0, pl.ds(pl.multiple_of(j * t, t), t), :]
        for cs in tiles:
            buf[:, cs] = lax.dot_general(kb, qq_scr[cs, :], nt_dims, preferred_element_type=F32)

    def diag_scores_into(j, buf):
        for k0, k1, cs in diag_parts:
            kb = k_ref[0, pl.ds(pl.multiple_of(j * t + k0, hk), k1 - k0), :]
            buf[k0:k1, cs] = lax.dot_general(kb, qq_scr[cs, :], nt_dims, preferred_element_type=F32)

    def consume_part(j, buf, k0, k1, cs, bias_of):
        vt = vt_scr[:, pl.ds(pl.multiple_of(j * t + k0, hk), k1 - k0)]
        s = buf[k0:k1, cs]
        if bias_of is not None:
            s = s + bias_of(k0, k1, cs)
        m_prev = m_scr[:, cs]
        m_new = jnp.maximum(m_prev, jnp.max(s, axis=0, keepdims=True))
        alpha = jnp.exp2(m_prev - m_new)
        p = jnp.exp2(s - m_new)
        acc_scr[:, cs] = alpha * acc_scr[:, cs] + jnp.dot(vt, p.astype(BF16), preferred_element_type=F32)
        m_scr[:, cs] = m_new

    def consume(j, buf, bias_of):
        for cs in tiles:
            consume_part(j, buf, 0, t, cs, bias_of)

    def diag_consume(j, buf):
        for k0, k1, cs in diag_parts:
            consume_part(j, buf, k0, k1, cs, lambda a, b, c: bias_ref[0, 0, a:b, c])

    near = lambda a, b, c: bias_ref[0, 1, a:b, c]

    @pl.when(qi == 0)
    def _():
        diag_scores_into(0, bufs[0])

    @pl.when(qi > 0)
    def _():
        scores_into(0, bufs[0])

    n_far = jnp.maximum(qi - 1, 0)

    def far_pair(pi, carry):
        j = 2 * pi
        scores_into(j + 1, bufs[1])
        consume(j, bufs[0], None)
        scores_into(j + 2, bufs[0])
        consume(j + 1, bufs[1], None)
        return carry

    lax.fori_loop(0, n_far // 2, far_pair, 0)

    @pl.when(qi == 0)
    def _():
        diag_consume(0, bufs[0])

    @pl.when(jnp.logical_and(qi >= 1, lax.rem(qi, 2) == 1))
    def _():
        diag_scores_into(qi, bufs[1])
        consume(qi - 1, bufs[0], near)
        diag_consume(qi, bufs[1])

    @pl.when(jnp.logical_and(qi >= 2, lax.rem(qi, 2) == 0))
    def _():
        scores_into(qi - 1, bufs[1])
        consume(qi - 2, bufs[0], None)
        diag_scores_into(qi, bufs[0])
        consume(qi - 1, bufs[1], near)
        diag_consume(qi, bufs[0])

    inv = 1.0 / acc_scr[HEAD:HEAD + 1, :]
    o = acc_scr[0:HEAD, 0:t] * inv[:, 0:t] - acc_scr[0:HEAD, t:2 * t] * (sc_ref[0] * inv[:, t:2 * t])
    ms = jnp.mean(o * o, axis=0, keepdims=True)
    o = (o * lax.rsqrt(ms + EPS)).T
    o_ref[0] = ((o * g_ref[0]) * sc_ref[1]).astype(BF16)


def _attention(proj, bias, scalars, subln_all, layer, casts, t=ATTN_T):
    b, s, _ = proj.shape
    nq = s // t
    nsteps = b * N_HEADS * nq
    step = lambda bi, h, i: (bi * N_HEADS + h) * nq + i
    cast_in, cast_specs, cast_out_specs, cast_out_shapes = [], [], [], []
    for w_all, li in casts:
        cols = w_all.shape[-1]
        rows = math.prod(w_all.shape[1:-1]) // nsteps
        cast_in.append(w_all.reshape(w_all.shape[0], nsteps, rows, cols))
        cast_specs.append(pl.BlockSpec((1, 1, rows, cols), lambda bi, h, i, li=li: (li, step(bi, h, i), 0, 0)))
        cast_out_specs.append(pl.BlockSpec((1, rows, cols), lambda bi, h, i: (step(bi, h, i), 0, 0)))
        cast_out_shapes.append(jax.ShapeDtypeStruct((nsteps, rows, cols), BF16))
    out = pl.pallas_call(
        functools.partial(_attn_body, t=t, cw=ATTN_CW, n_cast=len(casts)),
        grid=(b, N_HEADS, nq),
        in_specs=[pl.BlockSpec(memory_space=pltpu.SMEM),
                  pl.BlockSpec((1, t, HEAD), lambda bi, h, i: (bi, i, h)),
                  pl.BlockSpec((1, s, HEAD), lambda bi, h, i: (bi, 0, N_HEADS + h)),
                  pl.BlockSpec((1, s, HEAD), lambda bi, h, i: (bi, 0, 2 * N_HEADS + h)),
                  pl.BlockSpec((1, 2, t, 2 * t), lambda bi, h, i: (h, 0, 0, 0)),
                  pl.BlockSpec((1, 1, HEAD), lambda bi, h, i: (layer, 0, 0))] + cast_specs,
        out_specs=[pl.BlockSpec((1, t, HEAD), lambda bi, h, i: (bi, i, h))] + cast_out_specs,
        out_shape=[jax.ShapeDtypeStruct((b, s, N_HEADS * HEAD), BF16)] + cast_out_shapes,
        scratch_shapes=[pltpu.VMEM((2 * t, HEAD), BF16),
                        pltpu.VMEM((HEAD + ONES_ROWS, s), BF16),
                        pltpu.VMEM((1, 2 * t), F32),
                        pltpu.VMEM((HEAD + ONES_ROWS, 2 * t), F32),
                        pltpu.VMEM((t, 2 * t), F32),
                        pltpu.VMEM((t, 2 * t), F32)],
        compiler_params=_cparams(3),
        name="diff_attention",
    )(scalars, proj, proj, proj, bias, subln_all, *cast_in)
    return out[0], [o.reshape(w_all.shape[1:]) for o, (w_all, _) in zip(out[1:], casts)]


def _hgrn_levels(c, sub):
    levels = []
    g = c // 2
    while g >= sub:
        levels.append(g)
        g //= 2
    return levels


def _hgrn_consts(l, c, sub):
    r = np.arange(HGRN_GROUP)[:, None]
    s = np.arange(HGRN_GROUP)[None, :]
    lcum = ((r // c == s // c) & (s <= r)).astype(np.float32)
    rc = np.arange(c)[:, None]
    sc = np.arange(c)[None, :]
    masks = [((rc // (2 * g) == sc // (2 * g)) & (rc % (2 * g) >= g) & (sc % (2 * g) < g)).astype(np.float32)
             for g in _hgrn_levels(c, sub)]
    half = c // 2
    rg = np.arange(HGRN_GROUP)[:, None]
    sg = np.arange(HGRN_GROUP)[None, :]
    top = (rg // c == sg // c) & (rg % c >= half) & (sg % c < half)
    near = (rg // half == sg // half) & (sg <= rg)
    fmask = np.stack([top, near]).astype(np.float32)
    return jnp.asarray(lcum, BF16), jnp.asarray(np.stack(masks), F32), jnp.asarray(fmask, F32)


def _hgrn_body(q_ref, f_ref, i_ref, g_ref, lb_ref, ng_ref, lcum_ref, mask_ref, fmask_ref, o_ref,
               st_scr, stb_scr, a_scr, oi_scr, *, l, c, sub):
    nchunk = l // c
    levels = _hgrn_levels(c, sub)

    @pl.when(pl.program_id(2) == 0)
    def _():
        st_scr[...] = jnp.zeros(st_scr.shape, F32)

    q = q_ref[0].astype(F32)
    fr = f_ref[0].astype(F32)
    vb = i_ref[0]
    lb = lb_ref[0]
    f = lb + (1.0 - lb) * jax.nn.sigmoid(fr)
    logf = jnp.log(jnp.maximum(f, jnp.finfo(F32).tiny))
    k = (1.0 - lb) * jax.nn.sigmoid(-fr)

    hi = logf.astype(BF16)
    lo = (logf - hi.astype(F32)).astype(BF16)
    hilo = jnp.concatenate([hi, lo], axis=1)
    lcum = lcum_ref[...]
    grp = lcum.shape[0]
    parts = []
    for gi in range(l // grp):
        both = jnp.dot(lcum, hilo[gi * grp:(gi + 1) * grp], preferred_element_type=F32)
        parts.append(both[:, 0:HEAD] + both[:, HEAD:2 * HEAD])
    bcum = jnp.concatenate(parts, axis=0)

    def rows_of(arr, group, row):
        a3 = arr.reshape(l // group, group, HEAD)
        return jnp.broadcast_to(a3[:, row:row + 1, :], a3.shape).reshape(l, HEAD)

    nt = (((1,), (1,)), ((), ()))
    tn = (((0,), (0,)), ((), ()))

    def level_operands(g):
        ref = rows_of(bcum, 2 * g, g - 1)
        return ((q * jnp.exp(jnp.minimum(bcum - ref, 0.0))).astype(BF16),
                (k * jnp.exp(jnp.minimum(ref - bcum, 0.0))).astype(BF16))

    q_top, k_top = level_operands(levels[0])

    def top_scores(sl):
        return mask_ref[0] * lax.dot_general(q_top[sl], k_top[sl], nt, preferred_element_type=F32)

    half = c // 2
    b_start = rows_of(bcum, half, 0) - rows_of(logf, half, 0)
    span = b_start - bcum
    bounded = jnp.max(span) <= HGRN_SAFE_SPAN

    @pl.when(bounded)
    def _():
        q_f = (q * jnp.exp(-span)).astype(BF16)
        k_f = (k * jnp.exp(span)).astype(BF16)
        grp = fmask_ref.shape[-1]
        for gi in range(l // grp):
            sl = slice(gi * grp, (gi + 1) * grp)
            a = fmask_ref[0] * lax.dot_general(q_top[sl], k_top[sl], nt, preferred_element_type=F32)
            a = a + fmask_ref[1] * lax.dot_general(q_f[sl], k_f[sl], nt, preferred_element_type=F32)
            a_scr[sl, :] = jnp.dot(a.astype(BF16), vb[sl], preferred_element_type=F32)

    @pl.when(jnp.logical_not(bounded))
    def _():
        lower = [level_operands(g) for g in levels[1:]]
        nb = l // sub
        b3 = bcum.reshape(nb, sub, HEAD)
        q3 = q.reshape(nb, sub, HEAD)
        k3 = k.reshape(nb, sub, HEAD)
        v3 = vb.astype(F32).reshape(nb, sub, HEAD)
        tloc = lax.broadcasted_iota(jnp.int32, (nb, sub, 1), 1)
        od = jnp.zeros((nb, sub, HEAD), F32)
        for si in range(sub):
            e = jnp.exp(b3 - b3[:, si:si + 1, :])
            pr = q3 * (k3[:, si:si + 1, :] * e)
            w = jnp.sum(pr, axis=-1, keepdims=True)
            w = jnp.where(tloc >= si, w, 0.0)
            od = od + w * v3[:, si:si + 1, :]
        od = od.reshape(l, HEAD)
        for ci in range(nchunk):
            sl = slice(ci * c, (ci + 1) * c)
            a = top_scores(sl)
            for li, (q_g, k_g) in enumerate(lower):
                a = a + mask_ref[li + 1] * lax.dot_general(q_g[sl], k_g[sl], nt, preferred_element_type=F32)
            a_scr[sl, :] = jnp.dot(a.astype(BF16), vb[sl], preferred_element_type=F32) + od[sl]

    b_end = rows_of(bcum, c, c - 1)
    q_in = (q * jnp.exp(bcum)).astype(BF16)
    k_out = (k * jnp.exp(b_end - bcum)).astype(BF16)
    dec = jnp.exp(b_end)
    kv = [lax.dot_general(vb[ci * c:(ci + 1) * c], k_out[ci * c:(ci + 1) * c], tn, preferred_element_type=F32)
          for ci in range(nchunk)]
    st = st_scr[...]
    for ci in range(nchunk):
        stb_scr[ci] = st.astype(BF16)
        st = st * dec[ci * c:ci * c + 1, :] + kv[ci]
    st_scr[...] = st
    for ci in range(nchunk):
        sl = slice(ci * c, (ci + 1) * c)
        oi_scr[sl, :] = lax.dot_general(q_in[sl], stb_scr[ci], nt, preferred_element_type=F32)

    o = a_scr[...] + oi_scr[...]
    ms = jnp.mean(o * o, axis=-1, keepdims=True)
    o = o * lax.rsqrt(ms + EPS) * ng_ref[0]
    gate = g_ref[0].astype(F32)
    o_ref[0] = (o * (gate * jax.nn.sigmoid(gate))).astype(BF16)


def _hgrn(proj, lower_bounds, norm_all, layer, l=HGRN_L, c=HGRN_C, sub=HGRN_SUB):
    b, s, _ = proj.shape
    lcum, masks, fmask = _hgrn_consts(l, c, sub)
    base = 3 * N_HEADS

    def col(kind):
        return pl.BlockSpec((1, l, HEAD), lambda bi, h, i: (bi, i, base + kind * N_HEADS + h))

    return pl.pallas_call(
        functools.partial(_hgrn_body, l=l, c=c, sub=sub),
        grid=(b, N_HEADS, s // l),
        in_specs=[col(0), col(1), col(2), col(3),
                  pl.BlockSpec((1, 1, HEAD), lambda bi, h, i: (layer, 0, h)),
                  pl.BlockSpec((1, 1, HEAD), lambda bi, h, i: (layer, 0, 0)),
                  pl.BlockSpec(lcum.shape, lambda bi, h, i: (0, 0)),
                  pl.BlockSpec(masks.shape, lambda bi, h, i: (0, 0, 0)),
                  pl.BlockSpec(fmask.shape, lambda bi, h, i: (0, 0, 0))],
        out_specs=pl.BlockSpec((1, l, HEAD), lambda bi, h, i: (bi, i, h)),
        out_shape=jax.ShapeDtypeStruct((b, s, N_HEADS * HEAD), BF16),
        scratch_shapes=[pltpu.VMEM((HEAD, HEAD), F32),
                        pltpu.VMEM((l // c, HEAD, HEAD), BF16),
                        pltpu.VMEM((l, HEAD), F32),
                        pltpu.VMEM((l, HEAD), F32)],
        compiler_params=_cparams(3),
        name="hgrn2",
    )(proj, proj, proj, proj, lower_bounds, norm_all, lcum, masks, fmask)


def _first_row(cond, row):
    return jnp.min(jnp.where(cond, row, N_EXPERTS), axis=0, keepdims=True)


def _outproj_body(a_ref, hg_ref, w_ref, x_ref, mod_ref, g_ref, wrh_ref, wrl_ref, br_ref, xo_ref, h_ref, *, d):
    da = a_ref.shape[-1]
    mixed = jnp.dot(a_ref[0], w_ref[0, 0:da, :], preferred_element_type=F32)
    mixed = mixed + jnp.dot(hg_ref[0], w_ref[0, da:, :], preferred_element_type=F32)
    x = x_ref[0] + mod_ref[0, 2:3, :] * mixed
    xo_ref[0] = x
    h = _modulated_norm(x, g_ref[0], mod_ref[0, 4:5, :], mod_ref[0, 3:4, :])
    h_ref[0, :, 0:d] = h

    h_hi = h.astype(BF16)
    h_lo = (h - h_hi.astype(F32)).astype(BF16)
    logits = (jnp.dot(h_hi, wrh_ref[...], preferred_element_type=F32)
              + jnp.dot(h_lo, wrh_ref[...], preferred_element_type=F32)
              + jnp.dot(h_hi, wrl_ref[...], preferred_element_type=F32)) + br_ref[...]
    lt = logits.T[0:N_EXPERTS, :]
    erow = lax.broadcasted_iota(jnp.int32, lt.shape, 0)
    mx = jnp.max(lt, axis=0, keepdims=True)
    ex = jnp.exp(lt - mx)
    probs = ex / jnp.sum(ex, axis=0, keepdims=True)

    best = sel = v1 = v2 = i1 = i2 = None
    for gi in range(N_GROUPS):
        ing = (erow >= gi * GROUP_SIZE) & (erow < (gi + 1) * GROUP_SIZE)
        pg = jnp.where(ing, probs, -1.0)
        m1 = jnp.max(pg, axis=0, keepdims=True)
        a1 = _first_row(pg == m1, erow)
        pg2 = jnp.where(erow == a1, -1.0, pg)
        m2 = jnp.max(pg2, axis=0, keepdims=True)
        a2 = _first_row(pg2 == m2, erow)
        score = m1 + m2
        if gi == 0:
            best, sel, v1, v2, i1, i2 = score, jnp.zeros_like(a1), m1, m2, a1, a2
        else:
            better = score > best
            best = jnp.where(better, score, best)
            sel = jnp.where(better, gi, sel)
            v1 = jnp.where(better, m1, v1)
            v2 = jnp.where(better, m2, v2)
            i1 = jnp.where(better, a1, i1)
            i2 = jnp.where(better, a2, i2)
    tot = v1 + v2
    w1 = v1 / tot
    w2 = v2 / tot
    loc1 = i1 - sel * GROUP_SIZE
    loc2 = i2 - sel * GROUP_SIZE
    xrow = lax.broadcasted_iota(jnp.int32, (LANES, lt.shape[1]), 0)
    ext = jnp.where(xrow == loc1, w1, 0.0) + jnp.where(xrow == loc2, w2, 0.0)
    ext = jnp.where(xrow == GROUP_SIZE, sel.astype(F32), ext)
    h_ref[0, :, d:] = ext.T


def _outproj_router(attn_o, hgrn_o, w_out, x, mod, ffn_norm_all, wr_hi, wr_lo, br, layer, tm=512):
    w_out_all = w_out[None]
    b, s, d = x.shape
    da = attn_o.shape[-1]
    dmix = w_out_all.shape[1]
    tok = lambda bi, i: (bi, i, 0)
    return pl.pallas_call(
        functools.partial(_outproj_body, d=d),
        grid=(b, s // tm),
        in_specs=[pl.BlockSpec((1, tm, da), tok),
                  pl.BlockSpec((1, tm, dmix - da), tok),
                  pl.BlockSpec((1, dmix, d), lambda bi, i: (0, 0, 0)),
                  pl.BlockSpec((1, tm, d), tok),
                  pl.BlockSpec((1, N_MOD, d), lambda bi, i: (bi, 0, 0)),
                  pl.BlockSpec((1, 1, d), lambda bi, i: (layer, 0, 0)),
                  pl.BlockSpec((d, LANES), lambda bi, i: (0, 0)),
                  pl.BlockSpec((d, LANES), lambda bi, i: (0, 0)),
                  pl.BlockSpec((1, LANES), lambda bi, i: (0, 0))],
        out_specs=[pl.BlockSpec((1, tm, d), tok),
                   pl.BlockSpec((1, tm, d + LANES), tok)],
        out_shape=[jax.ShapeDtypeStruct((b, s, d), F32),
                   jax.ShapeDtypeStruct((b, s, d + LANES), F32)],
        compiler_params=_cparams(2),
        name="outproj_router",
    )(attn_o, hgrn_o, w_out_all, x, mod, ffn_norm_all, wr_hi, wr_lo, br)


def _moe_body(tg_ref, nv_ref, nu_ref, tok_ref, h_hbm, wg_ref, wu_ref, wd_ref, y_hbm,
              xbuf, xbf, acc, gsem, ssem, *, tm, d):
    i = pl.program_id(0)
    e = pl.program_id(1)
    n_used = nu_ref[0]
    slot = lax.rem(i, 2)

    def gather_start(tile, sl):
        base = tile * tm

        def body(r, carry):
            pltpu.make_async_copy(h_hbm.at[pl.ds(tok_ref[base + r], 1)], xbuf.at[sl, pl.ds(r, 1)], gsem.at[sl]).start()
            return carry

        lax.fori_loop(0, tm, body, 0, unroll=MOE_ISSUE_UNROLL)

    def gather_wait(sl):
        pltpu.make_async_copy(h_hbm.at[pl.ds(0, tm)], xbuf.at[sl], gsem.at[sl]).wait()

    def scatter_start(tile, sl, n):
        base = tile * tm

        def body(r, carry):
            pltpu.make_async_copy(acc.at[sl, pl.ds(r, 1)], y_hbm.at[pl.ds(tok_ref[base + r], 1)], ssem.at[sl]).start()
            return carry

        @pl.when(n == tm)
        def _():
            lax.fori_loop(0, tm, body, 0, unroll=MOE_ISSUE_UNROLL)

        @pl.when(n < tm)
        def _():
            lax.fori_loop(0, n, body, 0)

    def scatter_wait(sl, n):
        @pl.when(n == tm)
        def _():
            pltpu.make_async_copy(acc.at[sl], y_hbm.at[pl.ds(0, tm)], ssem.at[sl]).wait()

        @pl.when(n < tm)
        def _():
            def body(r, carry):
                pltpu.make_async_copy(acc.at[sl, pl.ds(r, 1)], y_hbm.at[pl.ds(r, 1)], ssem.at[sl]).wait()
                return carry

            lax.fori_loop(0, n, body, 0)

    @pl.when(i < n_used)
    def _():
        @pl.when(e == 0)
        def _():
            @pl.when(i == 0)
            def _():
                gather_start(0, 0)

            gather_wait(slot)

            @pl.when(i + 1 < n_used)
            def _():
                gather_start(i + 1, 1 - slot)

            xbf[...] = xbuf[slot, :, 0:d].astype(BF16)

            @pl.when(i >= 2)
            def _():
                scatter_wait(slot, nv_ref[jnp.maximum(i - 2, 0)])

            acc[slot] = jnp.zeros((tm, d), F32)

        ext = xbuf[slot, :, d:d + LANES]
        gate = jnp.zeros((tm, 1), F32)
        for ei in range(GROUP_SIZE):
            gate = jnp.where(e == ei, ext[:, ei:ei + 1], gate)
        x = xbf[...]
        hg = jnp.dot(x, wg_ref[0, 0], preferred_element_type=F32)
        hu = jnp.dot(x, wu_ref[0, 0], preferred_element_type=F32)
        act = (hg * jax.nn.sigmoid(hg)) * hu * gate
        acc[slot] += jnp.dot(act.astype(BF16), wd_ref[0, 0], preferred_element_type=F32)

        @pl.when(e == GROUP_SIZE - 1)
        def _():
            scatter_start(i, slot, nv_ref[i])

            @pl.when(i == n_used - 1)
            def _():
                scatter_wait(slot, nv_ref[i])

                @pl.when(i >= 1)
                def _():
                    scatter_wait(1 - slot, nv_ref[jnp.maximum(i - 1, 0)])


def _route(hext, d, tm):
    b, s, _ = hext.shape
    t = b * s
    n_tiles = t // tm + N_GROUPS
    gid = hext[:, :, d + GROUP_SIZE].reshape(t).astype(jnp.int32)
    onehot = (gid[:, None] == jnp.arange(N_GROUPS, dtype=jnp.int32)[None, :]).astype(jnp.int32)
    cnt = jnp.sum(onehot, axis=0)
    rank = jnp.sum(onehot * (jnp.cumsum(onehot, axis=0) - onehot), axis=1)
    ntile_g = (cnt + tm - 1) // tm
    tile_end_g = jnp.cumsum(ntile_g)
    tile_start_g = tile_end_g - ntile_g
    pos = (tile_start_g * tm)[gid] + rank
    tok_of_row = jnp.zeros((n_tiles * tm,), jnp.int32).at[pos].set(jnp.arange(t, dtype=jnp.int32))
    tile = jnp.arange(n_tiles, dtype=jnp.int32)
    tile_gid = jnp.minimum(jnp.sum((tile[:, None] >= tile_end_g[None, :]).astype(jnp.int32), axis=1), N_GROUPS - 1)
    nvalid = jnp.clip(cnt[tile_gid] - (tile - tile_start_g[tile_gid]) * tm, 0, tm).astype(jnp.int32)
    n_used = tile_end_g[N_GROUPS - 1:].astype(jnp.int32)
    return tile_gid.astype(jnp.int32), nvalid, n_used, tok_of_row


def _moe(hext, wg, wu, wd, tm=MOE_TM):
    wg_all, wu_all, wd_all, layer = wg[None], wu[None], wd[None], 0
    b, s, dx = hext.shape
    d = dx - LANES
    t = b * s
    f = wg_all.shape[-1]
    tile_gid, nvalid, n_used, tok_of_row = _route(hext, d, tm)
    n_tiles = tile_gid.shape[0]

    def wmap(i, e, tg, nv, nu, tok):
        return (layer, tg[i] * GROUP_SIZE + e, 0, 0)

    return pl.pallas_call(
        functools.partial(_moe_body, tm=tm, d=d),
        grid_spec=pltpu.PrefetchScalarGridSpec(
            num_scalar_prefetch=4,
            grid=(n_tiles, GROUP_SIZE),
            in_specs=[pl.BlockSpec(memory_space=pl.ANY),
                      pl.BlockSpec((1, 1, d, f), wmap),
                      pl.BlockSpec((1, 1, d, f), wmap),
                      pl.BlockSpec((1, 1, f, d), wmap)],
            out_specs=pl.BlockSpec(memory_space=pl.ANY),
            scratch_shapes=[pltpu.VMEM((2, tm, dx), F32),
                            pltpu.VMEM((tm, d), BF16),
                            pltpu.VMEM((2, tm, d), F32),
                            pltpu.SemaphoreType.DMA((2,)),
                            pltpu.SemaphoreType.DMA((2,))],
        ),
        out_shape=jax.ShapeDtypeStruct((t, d), F32),
        compiler_params=_cparams(2),
        name="moe_experts",
    )(tile_gid, nvalid, n_used, tok_of_row, hext.reshape(t, dx), wg_all, wu_all, wd_all)


def _final_body(x_ref, y_ref, mod_ref, g_ref, o_ref):
    x = x_ref[0] + mod_ref[0, 5:6, :] * y_ref[...]
    ms = jnp.mean(x * x, axis=-1, keepdims=True)
    o_ref[0] = x * lax.rsqrt(ms + EPS) * g_ref[...]


def _final(x, y, mod, g, tm=512):
    b, s, d = x.shape
    nt = s // tm
    return pl.pallas_call(
        _final_body,
        grid=(b, nt),
        in_specs=[pl.BlockSpec((1, tm, d), lambda bi, i: (bi, i, 0)),
                  pl.BlockSpec((tm, d), lambda bi, i: (bi * nt + i, 0)),
                  pl.BlockSpec((1, N_MOD, d), lambda bi, i: (bi, 0, 0)),
                  pl.BlockSpec((1, d), lambda bi, i: (0, 0))],
        out_specs=pl.BlockSpec((1, tm, d), lambda bi, i: (bi, i, 0)),
        out_shape=jax.ShapeDtypeStruct(x.shape, F32),
        compiler_params=_cparams(2),
        name="final_norm",
    )(x, y, mod, g.reshape(1, d))


def kernel(x, c, w_in, w_out, attn_norm, ffn_norm, w_ada, b_ada, lambda_q1, lambda_k1, lambda_q2, lambda_k2,
           diff_subln, hgrn_lb, hgrn_norm, rel_bias, w_router, b_router, w_gate, w_up, w_down, final_norm):
    b, s, d = x.shape
    depth = w_in.shape[0]

    w_in_cur = w_in[0].astype(BF16)
    wr =jnp.pad(w_router.astype(F32), ((0, 0), (0, LANES - N_EXPERTS)))
    wr_hi = wr.astype(BF16)
    wr_lo = (wr - wr_hi.astype(F32)).astype(BF16)
    br = jnp.pad(b_router.astype(F32), (0, LANES - N_EXPERTS)).reshape(1, LANES)
    lb_soft = jax.nn.softmax(hgrn_lb.astype(F32), axis=0)
    lower_bounds = jnp.maximum(jnp.cumsum(lb_soft, axis=0) - lb_soft[0:1], 0.0).reshape(depth, 1, -1)
    attn_norm3 = attn_norm.astype(F32).reshape(depth, 1, d)
    ffn_norm3 = ffn_norm.astype(F32).reshape(depth, 1, d)
    subln3 = diff_subln.astype(F32).reshape(depth, 1, HEAD)
    hnorm3 = hgrn_norm.astype(F32).reshape(depth, 1, HEAD)

    c8 = jnp.pad(c.astype(F32), ((0, 8 - b), (0, 0)))
    mods = _ada(c8, w_ada, b_ada)
    bias = _bias_tiles(rel_bias, ATTN_T)

    y = None
    mod_prev = None
    for layer in range(depth):
        mod = mods[layer, :b].reshape(b, N_MOD, d)
        lambda_init = 0.8 - 0.6 * math.exp(-0.3 * layer)
        lam = (jnp.exp(jnp.sum(lambda_q1[layer].astype(F32) * lambda_k1[layer].astype(F32)))
               - jnp.exp(jnp.sum(lambda_q2[layer].astype(F32) * lambda_k2[layer].astype(F32)))
               + lambda_init)
        scalars = jnp.stack([lam, jnp.asarray(1.0 - lambda_init, F32)]).astype(F32)

        x, proj = _inproj(x, y, mod_prev, mod, attn_norm3, w_in_cur, layer)
        casts = [(w_out, layer), (w_gate, layer), (w_up, layer), (w_down, layer)]
        if layer + 1 < depth:
            casts.append((w_in, layer + 1))
        attn_o, converted = _attention(proj, bias, scalars, subln3, layer, casts)
        w_out_b, w_gate_b, w_up_b, w_down_b = converted[:4]
        if layer + 1 < depth:
            w_in_cur = converted[4]
        hgrn_o = _hgrn(proj, lower_bounds, hnorm3, layer)
        x, hext = _outproj_router(attn_o, hgrn_o, w_out_b, x, mod, ffn_norm3, wr_hi, wr_lo, br, layer)
        y = _moe(hext, w_gate_b, w_up_b, w_down_b)
        mod_prev = mod
    return _final(x, y, mod_prev, final_norm.astype(F32))
```

```python
import functools
import math

import numpy as np
import jax
import jax.numpy as jnp
from jax import lax
from jax.experimental import pallas as pl
from jax.experimental.pallas import tpu as pltpu

F32 = jnp.float32
BF16 = jnp.bfloat16
EPS = 1e-6

LANES = 128
HEAD = 128
QK = 64
N_HEADS = 8
NUM_BUCKETS = 32
MAX_DISTANCE = 128
N_EXPERTS = 16
N_GROUPS = 4
GROUP_SIZE = N_EXPERTS // N_GROUPS
N_MOD = 6
NEG = -1e30

ATTN_T = 512
ATTN_CW = 512
ONES_ROWS = 16
LOG2E = math.log2(math.e)
HGRN_L = 4096
HGRN_C = 64
HGRN_SUB = 8
HGRN_GROUP = 256
HGRN_SAFE_SPAN = 80.0
NORM_ROWS = 256
MOE_TM = 512
MOE_ISSUE_UNROLL = MOE_TM
VMEM_LIMIT = 56 * 1024 * 1024


def _cparams(n_axes):
    return pltpu.CompilerParams(dimension_semantics=("arbitrary",) * n_axes, vmem_limit_bytes=VMEM_LIMIT)


def _ada_body(c_ref, w_ref, b_ref, o_ref):
    c = c_ref[...]
    ca = (c * jax.nn.sigmoid(c)).astype(BF16)
    o_ref[0] = jnp.dot(ca, w_ref[0].astype(BF16), preferred_element_type=F32) + b_ref[0]


def _ada(c8, w_ada, b_ada3, n_layers, tn=1024):
    _, d, n = w_ada.shape
    return pl.pallas_call(
        _ada_body,
        grid=(n_layers, n // tn),
        in_specs=[pl.BlockSpec((8, d), lambda l, j: (0, 0)),
                  pl.BlockSpec((1, d, tn), lambda l, j: (l, 0, j)),
                  pl.BlockSpec((1, 1, tn), lambda l, j: (l, 0, j))],
        out_specs=pl.BlockSpec((1, 8, tn), lambda l, j: (l, 0, j)),
        out_shape=jax.ShapeDtypeStruct((n_layers, 8, n), F32),
        compiler_params=_cparams(2),
        name="ada_mod",
    )(c8, w_ada, b_ada3)


def _modulated_norm(x, g, scale, shift):
    ms = jnp.mean(x * x, axis=-1, keepdims=True)
    return (x * lax.rsqrt(ms + EPS) * g) * (1.0 + scale) + shift


def _inproj_body(*refs, has_res):
    if has_res:
        x_ref, y_ref, gp_ref, mod_ref, g_ref, w_ref, xo_ref, p_ref, h_scr = refs
    else:
        x_ref, mod_ref, g_ref, w_ref, p_ref, h_scr = refs

    @pl.when(pl.program_id(2) == 0)
    def _():
        tm = h_scr.shape[0]
        for r0 in range(0, tm, NORM_ROWS):
            rows = slice(r0, r0 + NORM_ROWS)
            x = x_ref[0, rows, :]
            if has_res:
                x = x + gp_ref[0, 5:6, :] * y_ref[rows, :]
                xo_ref[0, rows, :] = x
            h = _modulated_norm(x, g_ref[0], mod_ref[0, 1:2, :], mod_ref[0, 0:1, :])
            h_scr[rows, :] = h.astype(BF16)

    p_ref[0] = jnp.dot(h_scr[...], w_ref[0], preferred_element_type=F32).astype(BF16)


def _inproj(x, y, mod_prev, mod, g_all, w, layer, tm=1024, tn=1024):
    w_all = w[None]
    b, s, d = x.shape
    n = w_all.shape[-1]
    nt = s // tm
    has_res = y is not None
    xspec = pl.BlockSpec((1, tm, d), lambda bi, i, j: (bi, i, 0))
    modspec = pl.BlockSpec((1, N_MOD, d), lambda bi, i, j: (bi, 0, 0))
    once = dict(pipeline_mode=pl.Buffered(1))
    in_specs = [pl.BlockSpec((1, tm, d), lambda bi, i, j: (bi, i, 0), **once)]
    args = [x]
    if has_res:
        in_specs += [pl.BlockSpec((tm, d), lambda bi, i, j: (bi * nt + i, 0), **once), modspec]
        args += [y, mod_prev]
    in_specs += [modspec,
                 pl.BlockSpec((1, 1, d), lambda bi, i, j: (layer, 0, 0)),
                 pl.BlockSpec((1, d, tn), lambda bi, i, j: (0, 0, j))]
    args += [mod, g_all, w_all]
    pspec = pl.BlockSpec((1, tm, tn), lambda bi, i, j: (bi, i, j))
    pshape = jax.ShapeDtypeStruct((b, s, n), BF16)
    if has_res:
        out_specs, out_shape = [xspec, pspec], [jax.ShapeDtypeStruct(x.shape, F32), pshape]
    else:
        out_specs, out_shape = pspec, pshape
    out = pl.pallas_call(
        functools.partial(_inproj_body, has_res=has_res),
        grid=(b, nt, n // tn),
        in_specs=in_specs, out_specs=out_specs, out_shape=out_shape,
        scratch_shapes=[pltpu.VMEM((tm, d), BF16)],
        compiler_params=_cparams(3),
        name="inproj",
    )(*args)
    return (out[0], out[1]) if has_res else (x, out)


def _t5_bucket_np(n):
    max_exact = NUM_BUCKETS // 2
    nf = np.maximum(n, 1).astype(np.float32)
    large = max_exact + (np.log(nf / np.float32(max_exact)) / np.float32(math.log(MAX_DISTANCE / max_exact))
                         * np.float32(NUM_BUCKETS - max_exact)).astype(np.int32)
    large = np.minimum(large, NUM_BUCKETS - 1)
    return np.where(n < max_exact, n, large).astype(np.int32)


def _bias_body(tab_ref, bd_ref, bu_ref, o_ref):
    t = bd_ref.shape[0]
    tab = jnp.broadcast_to(tab_ref[0], (t, LANES))
    for ti, idx_ref in enumerate((bd_ref, bu_ref)):
        for cb in range(t // LANES):
            cols = slice(cb * LANES, (cb + 1) * LANES)
            idx = idx_ref[:, cols]
            val = jnp.take_along_axis(tab, jnp.maximum(idx, 0), axis=1)
            o_ref[0, ti, :, cols] = jnp.where(idx < 0, NEG, val)


def _bias_tiles(rel_bias, t):
    key = np.arange(t)[:, None]
    qry = np.arange(t)[None, :]
    bd = np.where(qry >= key, _t5_bucket_np(np.maximum(qry - key, 0)), -1).astype(np.int32)
    bu = _t5_bucket_np(t + qry - key)
    assert MAX_DISTANCE <= t + 1, "tiles two or more to the left must lie in the far bucket"
    tab = (rel_bias.astype(F32) - rel_bias[NUM_BUCKETS - 1:].astype(F32)) * LOG2E
    tab = jnp.pad(tab.T, ((0, 0), (0, LANES - NUM_BUCKETS))).reshape(2 * N_HEADS, 1, LANES)
    return pl.pallas_call(
        _bias_body,
        grid=(N_HEADS, 2),
        in_specs=[pl.BlockSpec((1, 1, LANES), lambda h, m: (2 * h + m, 0, 0)),
                  pl.BlockSpec((t, t), lambda h, m: (0, 0)),
                  pl.BlockSpec((t, t), lambda h, m: (0, 0))],
        out_specs=pl.BlockSpec((1, 2, t, t), lambda h, m: (h, 0, 0, m)),
        out_shape=jax.ShapeDtypeStruct((N_HEADS, 2, t, 2 * t), F32),
        compiler_params=_cparams(2),
        name="bias_tiles",
    )(tab, jnp.asarray(bd), jnp.asarray(bu))


def _attn_body(*refs, t, cw, n_cast, with_ada):
    n_in = 6 + n_cast + (3 if with_ada else 0)
    n_out = 1 + n_cast + (1 if with_ada else 0)
    sc_ref, q_ref, k_ref, v_ref, bias_ref, g_ref = refs[:6]
    w_refs = refs[6:6 + n_cast]
    o_ref = refs[n_in]
    wb_refs = refs[n_in + 1:n_in + 1 + n_cast]
    qq_scr, vt_scr, m_scr, acc_scr, s0_scr, s1_scr = refs[n_in + n_out:]
    for w_ref, wb_ref in zip(w_refs, wb_refs):
        wb_ref[0] = w_ref[0, 0].astype(BF16)
    if with_ada:
        _ada_body(refs[6 + n_cast], refs[7 + n_cast], refs[8 + n_cast], refs[n_in + n_out - 1])

    qi = pl.program_id(2)
    s_len = k_ref.shape[1]

    @pl.when(qi == 0)
    def _():
        for ci in range(s_len // t):
            vt_scr[0:HEAD, ci * t:(ci + 1) * t] = v_ref[0, ci * t:(ci + 1) * t, :].T
        row = lax.broadcasted_iota(jnp.int32, (ONES_ROWS, s_len), 0)
        vt_scr[HEAD:HEAD + ONES_ROWS, :] = jnp.where(row == 0, 1.0, 0.0).astype(BF16)

    q = (q_ref[0].astype(F32) * (QK ** -0.5 * LOG2E)).astype(BF16)
    lane = lax.broadcasted_iota(jnp.int32, q.shape, 1)
    zero = jnp.zeros_like(q)
    qq_scr[0:t, :] = jnp.where(lane < QK, q, zero)
    qq_scr[t:2 * t, :] = jnp.where(lane >= QK, q, zero)
    m_scr[...] = jnp.full(m_scr.shape, NEG, F32)
    acc_scr[...] = jnp.zeros(acc_scr.shape, F32)

    tiles = [slice(ct * cw, (ct + 1) * cw) for ct in range(2 * t // cw)]

    bufs = (s0_scr, s1_scr)

    nt_dims = (((1,), (1,)), ((), ()))
    hk = t // 2
    diag_parts = [(k0, k1, slice(mi * t + k0, (mi + 1) * t)) for mi in range(2) for k0, k1 in ((0, hk), (hk, t))]

    def scores_into(j, buf):
        kb = k_ref[0, pl.ds(pl.multiple_of(j * t, t), t), :]
        for cs in tiles:
            buf[:, cs] = lax.dot_general(kb, qq_scr[cs, :], nt_dims, preferred_element_type=F32)

    def diag_scores_into(j, buf):
        for k0, k1, cs in diag_parts:
            kb = k_ref[0, pl.ds(pl.multiple_of(j * t + k0, hk), k1 - k0), :]
            buf[k0:k1, cs] = lax.dot_general(kb, qq_scr[cs, :], nt_dims, preferred_element_type=F32)

    def consume_part(j, buf, k0, k1, cs, bias_of):
        vt = vt_scr[:, pl.ds(pl.multiple_of(j * t + k0, hk), k1 - k0)]
        s = buf[k0:k1, cs]
        if bias_of is not None:
            s = s + bias_of(k0, k1, cs)
        m_prev = m_scr[:, cs]
        m_new = jnp.maximum(m_prev, jnp.max(s, axis=0, keepdims=True))
        alpha = jnp.exp2(m_prev - m_new)
        p = jnp.exp2(s - m_new)
        acc_scr[:, cs] = alpha * acc_scr[:, cs] + jnp.dot(vt, p.astype(BF16), preferred_element_type=F32)
        m_scr[:, cs] = m_new

    def consume(j, buf, bias_of):
        for cs in tiles:
            consume_part(j, buf, 0, t, cs, bias_of)

    def diag_consume(j, buf):
        for k0, k1, cs in diag_parts:
            consume_part(j, buf, k0, k1, cs, lambda a, b, c: bias_ref[0, 0, a:b, c])

    near = lambda a, b, c: bias_ref[0, 1, a:b, c]

    @pl.when(qi == 0)
    def _():
        diag_scores_into(0, bufs[0])

    @pl.when(qi > 0)
    def _():
        scores_into(0, bufs[0])

    n_far = jnp.maximum(qi - 1, 0)

    def far_pair(pi, carry):
        j = 2 * pi
        scores_into(j + 1, bufs[1])
        consume(j, bufs[0], None)
        scores_into(j + 2, bufs[0])
        consume(j + 1, bufs[1], None)
        return carry

    lax.fori_loop(0, n_far // 2, far_pair, 0)

    @pl.when(qi == 0)
    def _():
        diag_consume(0, bufs[0])

    @pl.when(jnp.logical_and(qi >= 1, lax.rem(qi, 2) == 1))
    def _():
        diag_scores_into(qi, bufs[1])
        consume(qi - 1, bufs[0], near)
        diag_consume(qi, bufs[1])

    @pl.when(jnp.logical_and(qi >= 2, lax.rem(qi, 2) == 0))
    def _():
        scores_into(qi - 1, bufs[1])
        consume(qi - 2, bufs[0], None)
        diag_scores_into(qi, bufs[0])
        consume(qi - 1, bufs[1], near)
        diag_consume(qi, bufs[0])

    inv = 1.0 / acc_scr[HEAD:HEAD + 1, :]
    o = acc_scr[0:HEAD, 0:t] * inv[:, 0:t] - acc_scr[0:HEAD, t:2 * t] * (sc_ref[0] * inv[:, t:2 * t])
    ms = jnp.mean(o * o, axis=0, keepdims=True)
    o = (o * lax.rsqrt(ms + EPS)).T
    o_ref[0] = ((o * g_ref[0]) * sc_ref[1]).astype(BF16)


def _attention(proj, bias, scalars, subln_all, layer, casts, ada=None, t=ATTN_T):
    b, s, _ = proj.shape
    nq = s // t
    nsteps = b * N_HEADS * nq
    step = lambda bi, h, i: (bi * N_HEADS + h) * nq + i
    ada_in, ada_specs, ada_out_specs, ada_out_shapes = [], [], [], []
    if ada is not None:
        c8, w_ada, b_ada3, la = ada
        d_model, n_mod = w_ada.shape[1:]
        assert n_mod // LANES <= nsteps
        col = lambda bi, h, i: jnp.minimum(step(bi, h, i), n_mod // LANES - 1)
        ada_in = [c8, w_ada, b_ada3]
        ada_specs = [pl.BlockSpec((8, d_model), lambda bi, h, i: (0, 0)),
                     pl.BlockSpec((1, d_model, LANES), lambda bi, h, i: (la, 0, col(bi, h, i))),
                     pl.BlockSpec((1, 1, LANES), lambda bi, h, i: (la, 0, col(bi, h, i)))]
        ada_out_specs = [pl.BlockSpec((1, 8, LANES), lambda bi, h, i: (0, 0, col(bi, h, i)))]
        ada_out_shapes = [jax.ShapeDtypeStruct((1, 8, n_mod), F32)]
    cast_in, cast_specs, cast_out_specs, cast_out_shapes = [], [], [], []
    for w_all, li in casts:
        cols = w_all.shape[-1]
        rows = math.prod(w_all.shape[1:-1]) // nsteps
        cast_in.append(w_all.reshape(w_all.shape[0], nsteps, rows, cols))
        cast_specs.append(pl.BlockSpec((1, 1, rows, cols), lambda bi, h, i, li=li: (li, step(bi, h, i), 0, 0)))
        cast_out_specs.append(pl.BlockSpec((1, rows, cols), lambda bi, h, i: (step(bi, h, i), 0, 0)))
        cast_out_shapes.append(jax.ShapeDtypeStruct((nsteps, rows, cols), BF16))
    out = pl.pallas_call(
        functools.partial(_attn_body, t=t, cw=ATTN_CW, n_cast=len(casts), with_ada=ada is not None),
        grid=(b, N_HEADS, nq),
        in_specs=[pl.BlockSpec(memory_space=pltpu.SMEM),
                  pl.BlockSpec((1, t, HEAD), lambda bi, h, i: (bi, i, h)),
                  pl.BlockSpec((1, s, HEAD), lambda bi, h, i: (bi, 0, N_HEADS + h)),
                  pl.BlockSpec((1, s, HEAD), lambda bi, h, i: (bi, 0, 2 * N_HEADS + h)),
                  pl.BlockSpec((1, 2, t, 2 * t), lambda bi, h, i: (h, 0, 0, 0)),
                  pl.BlockSpec((1, 1, HEAD), lambda bi, h, i: (layer, 0, 0))] + cast_specs + ada_specs,
        out_specs=[pl.BlockSpec((1, t, HEAD), lambda bi, h, i: (bi, i, h))] + cast_out_specs + ada_out_specs,
        out_shape=[jax.ShapeDtypeStruct((b, s, N_HEADS * HEAD), BF16)] + cast_out_shapes + ada_out_shapes,
        scratch_shapes=[pltpu.VMEM((2 * t, HEAD), BF16),
                        pltpu.VMEM((HEAD + ONES_ROWS, s), BF16),
                        pltpu.VMEM((1, 2 * t), F32),
                        pltpu.VMEM((HEAD + ONES_ROWS, 2 * t), F32),
                        pltpu.VMEM((t, 2 * t), F32),
                        pltpu.VMEM((t, 2 * t), F32)],
        compiler_params=_cparams(3),
        name="diff_attention",
    )(scalars, proj, proj, proj, bias, subln_all, *cast_in, *ada_in)
    converted = [o.reshape(w_all.shape[1:]) for o, (w_all, _) in zip(out[1:], casts)]
    return out[0], converted, (out[-1][0] if ada is not None else None)


def _hgrn_levels(c, sub):
    levels = []
    g = c // 2
    while g >= sub:
        levels.append(g)
        g //= 2
    return levels


def _hgrn_consts(l, c, sub):
    r = np.arange(HGRN_GROUP)[:, None]
    s = np.arange(HGRN_GROUP)[None, :]
    lcum = ((r // c == s // c) & (s <= r)).astype(np.float32)
    rc = np.arange(c)[:, None]
    sc = np.arange(c)[None, :]
    masks = [((rc // (2 * g) == sc // (2 * g)) & (rc % (2 * g) >= g) & (sc % (2 * g) < g)).astype(np.float32)
             for g in _hgrn_levels(c, sub)]
    half = c // 2
    rg = np.arange(HGRN_GROUP)[:, None]
    sg = np.arange(HGRN_GROUP)[None, :]
    top = (rg // c == sg // c) & (rg % c >= half) & (sg % c < half)
    near = (rg // half == sg // half) & (sg <= rg)
    fmask = np.stack([top, near]).astype(np.float32)
    return jnp.asarray(lcum, BF16), jnp.asarray(np.stack(masks), F32), jnp.asarray(fmask, F32)


def _hgrn_body(q_ref, f_ref, i_ref, g_ref, lb_ref, ng_ref, lcum_ref, mask_ref, fmask_ref, o_ref,
               st_scr, stb_scr, a_scr, oi_scr, *, l, c, sub):
    nchunk = l // c
    levels = _hgrn_levels(c, sub)

    @pl.when(pl.program_id(2) == 0)
    def _():
        st_scr[...] = jnp.zeros(st_scr.shape, F32)

    q = q_ref[0].astype(F32)
    fr = f_ref[0].astype(F32)
    vb = i_ref[0]
    lb = lb_ref[0]
    f = lb + (1.0 - lb) * jax.nn.sigmoid(fr)
    logf = jnp.log(jnp.maximum(f, jnp.finfo(F32).tiny))
    k = (1.0 - lb) * jax.nn.sigmoid(-fr)

    hi = logf.astype(BF16)
    lo = (logf - hi.astype(F32)).astype(BF16)
    hilo = jnp.concatenate([hi, lo], axis=1)
    lcum = lcum_ref[...]
    grp = lcum.shape[0]
    parts = []
    for gi in range(l // grp):
        both = jnp.dot(lcum, hilo[gi * grp:(gi + 1) * grp], preferred_element_type=F32)
        parts.append(both[:, 0:HEAD] + both[:, HEAD:2 * HEAD])
    bcum = jnp.concatenate(parts, axis=0)

    def rows_of(arr, group, row):
        a3 = arr.reshape(l // group, group, HEAD)
        return jnp.broadcast_to(a3[:, row:row + 1, :], a3.shape).reshape(l, HEAD)

    nt = (((1,), (1,)), ((), ()))
    tn = (((0,), (0,)), ((), ()))

    def level_operands(g):
        ref = rows_of(bcum, 2 * g, g - 1)
        return ((q * jnp.exp(jnp.minimum(bcum - ref, 0.0))).astype(BF16),
                (k * jnp.exp(jnp.minimum(ref - bcum, 0.0))).astype(BF16))

    q_top, k_top = level_operands(levels[0])

    def top_scores(sl):
        return mask_ref[0] * lax.dot_general(q_top[sl], k_top[sl], nt, preferred_element_type=F32)

    half = c // 2
    b_start = rows_of(bcum, half, 0) - rows_of(logf, half, 0)
    span = b_start - bcum
    bounded = jnp.max(span) <= HGRN_SAFE_SPAN

    @pl.when(bounded)
    def _():
        q_f = (q * jnp.exp(-span)).astype(BF16)
        k_f = (k * jnp.exp(span)).astype(BF16)
        grp = fmask_ref.shape[-1]
        for gi in range(l // grp):
            sl = slice(gi * grp, (gi + 1) * grp)
            a = fmask_ref[0] * lax.dot_general(q_top[sl], k_top[sl], nt, preferred_element_type=F32)
            a = a + fmask_ref[1] * lax.dot_general(q_f[sl], k_f[sl], nt, preferred_element_type=F32)
            a_scr[sl, :] = jnp.dot(a.astype(BF16), vb[sl], preferred_element_type=F32)

    @pl.when(jnp.logical_not(bounded))
    def _():
        lower = [level_operands(g) for g in levels[1:]]
        nb = l // sub
        b3 = bcum.reshape(nb, sub, HEAD)
        q3 = q.reshape(nb, sub, HEAD)
        k3 = k.reshape(nb, sub, HEAD)
        v3 = vb.astype(F32).reshape(nb, sub, HEAD)
        tloc = lax.broadcasted_iota(jnp.int32, (nb, sub, 1), 1)
        od = jnp.zeros((nb, sub, HEAD), F32)
        for si in range(sub):
            e = jnp.exp(b3 - b3[:, si:si + 1, :])
            pr = q3 * (k3[:, si:si + 1, :] * e)
            w = jnp.sum(pr, axis=-1, keepdims=True)
            w = jnp.where(tloc >= si, w, 0.0)
            od = od + w * v3[:, si:si + 1, :]
        od = od.reshape(l, HEAD)
        for ci in range(nchunk):
            sl = slice(ci * c, (ci + 1) * c)
            a = top_scores(sl)
            for li, (q_g, k_g) in enumerate(lower):
                a = a + mask_ref[li + 1] * lax.dot_general(q_g[sl], k_g[sl], nt, preferred_element_type=F32)
            a_scr[sl, :] = jnp.dot(a.astype(BF16), vb[sl], preferred_element_type=F32) + od[sl]

    b_end = rows_of(bcum, c, c - 1)
    q_in = (q * jnp.exp(bcum)).astype(BF16)
    k_out = (k * jnp.exp(b_end - bcum)).astype(BF16)
    dec = jnp.exp(b_end)
    kv = [lax.dot_general(vb[ci * c:(ci + 1) * c], k_out[ci * c:(ci + 1) * c], tn, preferred_element_type=F32)
          for ci in range(nchunk)]
    st = st_scr[...]
    for ci in range(nchunk):
        stb_scr[ci] = st.astype(BF16)
        st = st * dec[ci * c:ci * c + 1, :] + kv[ci]
    st_scr[...] = st
    for ci in range(nchunk):
        sl = slice(ci * c, (ci + 1) * c)
        oi_scr[sl, :] = lax.dot_general(q_in[sl], stb_scr[ci], nt, preferred_element_type=F32)

    o = a_scr[...] + oi_scr[...]
    ms = jnp.mean(o * o, axis=-1, keepdims=True)
    o = o * lax.rsqrt(ms + EPS) * ng_ref[0]
    gate = g_ref[0].astype(F32)
    o_ref[0] = (o * (gate * jax.nn.sigmoid(gate))).astype(BF16)


def _hgrn(proj, lower_bounds, norm_all, layer, l=HGRN_L, c=HGRN_C, sub=HGRN_SUB):
    b, s, _ = proj.shape
    lcum, masks, fmask = _hgrn_consts(l, c, sub)
    base = 3 * N_HEADS

    def col(kind):
        return pl.BlockSpec((1, l, HEAD), lambda bi, h, i: (bi, i, base + kind * N_HEADS + h))

    return pl.pallas_call(
        functools.partial(_hgrn_body, l=l, c=c, sub=sub),
        grid=(b, N_HEADS, s // l),
        in_specs=[col(0), col(1), col(2), col(3),
                  pl.BlockSpec((1, 1, HEAD), lambda bi, h, i: (layer, 0, h)),
                  pl.BlockSpec((1, 1, HEAD), lambda bi, h, i: (layer, 0, 0)),
                  pl.BlockSpec(lcum.shape, lambda bi, h, i: (0, 0)),
                  pl.BlockSpec(masks.shape, lambda bi, h, i: (0, 0, 0)),
                  pl.BlockSpec(fmask.shape, lambda bi, h, i: (0, 0, 0))],
        out_specs=pl.BlockSpec((1, l, HEAD), lambda bi, h, i: (bi, i, h)),
        out_shape=jax.ShapeDtypeStruct((b, s, N_HEADS * HEAD), BF16),
        scratch_shapes=[pltpu.VMEM((HEAD, HEAD), F32),
                        pltpu.VMEM((l // c, HEAD, HEAD), BF16),
                        pltpu.VMEM((l, HEAD), F32),
                        pltpu.VMEM((l, HEAD), F32)],
        compiler_params=_cparams(3),
        name="hgrn2",
    )(proj, proj, proj, proj, lower_bounds, norm_all, lcum, masks, fmask)


def _first_row(cond, row):
    return jnp.min(jnp.where(cond, row, N_EXPERTS), axis=0, keepdims=True)


def _outproj_body(a_ref, hg_ref, w_ref, x_ref, mod_ref, g_ref, wrh_ref, wrl_ref, br_ref, xo_ref, h_ref, *, d):
    da = a_ref.shape[-1]
    mixed = jnp.dot(a_ref[0], w_ref[0, 0:da, :], preferred_element_type=F32)
    mixed = mixed + jnp.dot(hg_ref[0], w_ref[0, da:, :], preferred_element_type=F32)
    x = x_ref[0] + mod_ref[0, 2:3, :] * mixed
    xo_ref[0] = x
    h = _modulated_norm(x, g_ref[0], mod_ref[0, 4:5, :], mod_ref[0, 3:4, :])
    h_ref[0, :, 0:d] = h

    h_hi = h.astype(BF16)
    h_lo = (h - h_hi.astype(F32)).astype(BF16)
    logits = (jnp.dot(h_hi, wrh_ref[...], preferred_element_type=F32)
              + jnp.dot(h_lo, wrh_ref[...], preferred_element_type=F32)
              + jnp.dot(h_hi, wrl_ref[...], preferred_element_type=F32)) + br_ref[...]
    lt = logits.T[0:N_EXPERTS, :]
    erow = lax.broadcasted_iota(jnp.int32, lt.shape, 0)
    mx = jnp.max(lt, axis=0, keepdims=True)
    ex = jnp.exp(lt - mx)
    probs = ex / jnp.sum(ex, axis=0, keepdims=True)

    best = sel = v1 = v2 = i1 = i2 = None
    for gi in range(N_GROUPS):
        ing = (erow >= gi * GROUP_SIZE) & (erow < (gi + 1) * GROUP_SIZE)
        pg = jnp.where(ing, probs, -1.0)
        m1 = jnp.max(pg, axis=0, keepdims=True)
        a1 = _first_row(pg == m1, erow)
        pg2 = jnp.where(erow == a1, -1.0, pg)
        m2 = jnp.max(pg2, axis=0, keepdims=True)
        a2 = _first_row(pg2 == m2, erow)
        score = m1 + m2
        if gi == 0:
            best, sel, v1, v2, i1, i2 = score, jnp.zeros_like(a1), m1, m2, a1, a2
        else:
            better = score > best
            best = jnp.where(better, score, best)
            sel = jnp.where(better, gi, sel)
            v1 = jnp.where(better, m1, v1)
            v2 = jnp.where(better, m2, v2)
            i1 = jnp.where(better, a1, i1)
            i2 = jnp.where(better, a2, i2)
    tot = v1 + v2
    w1 = v1 / tot
    w2 = v2 / tot
    loc1 = i1 - sel * GROUP_SIZE
    loc2 = i2 - sel * GROUP_SIZE
    xrow = lax.broadcasted_iota(jnp.int32, (LANES, lt.shape[1]), 0)
    ext = jnp.where(xrow == loc1, w1, 0.0) + jnp.where(xrow == loc2, w2, 0.0)
    ext = jnp.where(xrow == GROUP_SIZE, sel.astype(F32), ext)
    h_ref[0, :, d:] = ext.T


def _outproj_router(attn_o, hgrn_o, w_out, x, mod, ffn_norm_all, wr_hi, wr_lo, br, layer, tm=512):
    w_out_all = w_out[None]
    b, s, d = x.shape
    da = attn_o.shape[-1]
    dmix = w_out_all.shape[1]
    tok = lambda bi, i: (bi, i, 0)
    return pl.pallas_call(
        functools.partial(_outproj_body, d=d),
        grid=(b, s // tm),
        in_specs=[pl.BlockSpec((1, tm, da), tok),
                  pl.BlockSpec((1, tm, dmix - da), tok),
                  pl.BlockSpec((1, dmix, d), lambda bi, i: (0, 0, 0)),
                  pl.BlockSpec((1, tm, d), tok),
                  pl.BlockSpec((1, N_MOD, d), lambda bi, i: (bi, 0, 0)),
                  pl.BlockSpec((1, 1, d), lambda bi, i: (layer, 0, 0)),
                  pl.BlockSpec((d, LANES), lambda bi, i: (0, 0)),
                  pl.BlockSpec((d, LANES), lambda bi, i: (0, 0)),
                  pl.BlockSpec((1, LANES), lambda bi, i: (0, 0))],
        out_specs=[pl.BlockSpec((1, tm, d), tok),
                   pl.BlockSpec((1, tm, d + LANES), tok)],
        out_shape=[jax.ShapeDtypeStruct((b, s, d), F32),
                   jax.ShapeDtypeStruct((b, s, d + LANES), F32)],
        compiler_params=_cparams(2),
        name="outproj_router",
    )(attn_o, hgrn_o, w_out_all, x, mod, ffn_norm_all, wr_hi, wr_lo, br)


def _moe_body(tg_ref, nv_ref, nu_ref, tok_ref, h_hbm, wg_ref, wu_ref, wd_ref, y_hbm,
              xbuf, xbf, acc, gsem, ssem, *, tm, d):
    i = pl.program_id(0)
    e = pl.program_id(1)
    n_used = nu_ref[0]
    slot = lax.rem(i, 2)

    def gather_start(tile, sl):
        base = tile * tm

        def body(r, carry):
            pltpu.make_async_copy(h_hbm.at[pl.ds(tok_ref[base + r], 1)], xbuf.at[sl, pl.ds(r, 1)], gsem.at[sl]).start()
            return carry

        lax.fori_loop(0, tm, body, 0, unroll=MOE_ISSUE_UNROLL)

    def gather_wait(sl):
        pltpu.make_async_copy(h_hbm.at[pl.ds(0, tm)], xbuf.at[sl], gsem.at[sl]).wait()

    def scatter_start(tile, sl, n):
        base = tile * tm

        def body(r, carry):
            pltpu.make_async_copy(acc.at[sl, pl.ds(r, 1)], y_hbm.at[pl.ds(tok_ref[base + r], 1)], ssem.at[sl]).start()
            return carry

        @pl.when(n == tm)
        def _():
            lax.fori_loop(0, tm, body, 0, unroll=MOE_ISSUE_UNROLL)

        @pl.when(n < tm)
        def _():
            lax.fori_loop(0, n, body, 0)

    def scatter_wait(sl, n):
        @pl.when(n == tm)
        def _():
            pltpu.make_async_copy(acc.at[sl], y_hbm.at[pl.ds(0, tm)], ssem.at[sl]).wait()

        @pl.when(n < tm)
        def _():
            def body(r, carry):
                pltpu.make_async_copy(acc.at[sl, pl.ds(r, 1)], y_hbm.at[pl.ds(r, 1)], ssem.at[sl]).wait()
                return carry

            lax.fori_loop(0, n, body, 0)

    @pl.when(i < n_used)
    def _():
        @pl.when(e == 0)
        def _():
            @pl.when(i == 0)
            def _():
                gather_start(0, 0)

            gather_wait(slot)

            @pl.when(i + 1 < n_used)
            def _():
                gather_start(i + 1, 1 - slot)

            xbf[...] = xbuf[slot, :, 0:d].astype(BF16)

            @pl.when(i >= 2)
            def _():
                scatter_wait(slot, nv_ref[jnp.maximum(i - 2, 0)])

            acc[slot] = jnp.zeros((tm, d), F32)

        ext = xbuf[slot, :, d:d + LANES]
        gate = jnp.zeros((tm, 1), F32)
        for ei in range(GROUP_SIZE):
            gate = jnp.where(e == ei, ext[:, ei:ei + 1], gate)
        x = xbf[...]
        hg = jnp.dot(x, wg_ref[0, 0], preferred_element_type=F32)
        hu = jnp.dot(x, wu_ref[0, 0], preferred_element_type=F32)
        act = (hg * jax.nn.sigmoid(hg)) * hu * gate
        acc[slot] += jnp.dot(act.astype(BF16), wd_ref[0, 0], preferred_element_type=F32)

        @pl.when(e == GROUP_SIZE - 1)
        def _():
            scatter_start(i, slot, nv_ref[i])

            @pl.when(i == n_used - 1)
            def _():
                scatter_wait(slot, nv_ref[i])

                @pl.when(i >= 1)
                def _():
                    scatter_wait(1 - slot, nv_ref[jnp.maximum(i - 1, 0)])


def _route(hext, d, tm):
    b, s, _ = hext.shape
    t = b * s
    n_tiles = t // tm + N_GROUPS
    gid = hext[:, :, d + GROUP_SIZE].reshape(t).astype(jnp.int32)
    onehot = (gid[:, None] == jnp.arange(N_GROUPS, dtype=jnp.int32)[None, :]).astype(jnp.int32)
    cnt = jnp.sum(onehot, axis=0)
    rank = jnp.sum(onehot * (jnp.cumsum(onehot, axis=0) - onehot), axis=1)
    ntile_g = (cnt + tm - 1) // tm
    tile_end_g = jnp.cumsum(ntile_g)
    tile_start_g = tile_end_g - ntile_g
    pos = (tile_start_g * tm)[gid] + rank
    tok_of_row = jnp.zeros((n_tiles * tm,), jnp.int32).at[pos].set(jnp.arange(t, dtype=jnp.int32))
    tile = jnp.arange(n_tiles, dtype=jnp.int32)
    tile_gid = jnp.minimum(jnp.sum((tile[:, None] >= tile_end_g[None, :]).astype(jnp.int32), axis=1), N_GROUPS - 1)
    nvalid = jnp.clip(cnt[tile_gid] - (tile - tile_start_g[tile_gid]) * tm, 0, tm).astype(jnp.int32)
    n_used = tile_end_g[N_GROUPS - 1:].astype(jnp.int32)
    return tile_gid.astype(jnp.int32), nvalid, n_used, tok_of_row


def _moe(hext, wg, wu, wd, tm=MOE_TM):
    wg_all, wu_all, wd_all, layer = wg[None], wu[None], wd[None], 0
    b, s, dx = hext.shape
    d = dx - LANES
    t = b * s
    f = wg_all.shape[-1]
    tile_gid, nvalid, n_used, tok_of_row = _route(hext, d, tm)
    n_tiles = tile_gid.shape[0]

    def wmap(i, e, tg, nv, nu, tok):
        return (layer, tg[i] * GROUP_SIZE + e, 0, 0)

    return pl.pallas_call(
        functools.partial(_moe_body, tm=tm, d=d),
        grid_spec=pltpu.PrefetchScalarGridSpec(
            num_scalar_prefetch=4,
            grid=(n_tiles, GROUP_SIZE),
            in_specs=[pl.BlockSpec(memory_space=pl.ANY),
                      pl.BlockSpec((1, 1, d, f), wmap),
                      pl.BlockSpec((1, 1, d, f), wmap),
                      pl.BlockSpec((1, 1, f, d), wmap)],
            out_specs=pl.BlockSpec(memory_space=pl.ANY),
            scratch_shapes=[pltpu.VMEM((2, tm, dx), F32),
                            pltpu.VMEM((tm, d), BF16),
                            pltpu.VMEM((2, tm, d), F32),
                            pltpu.SemaphoreType.DMA((2,)),
                            pltpu.SemaphoreType.DMA((2,))],
        ),
        out_shape=jax.ShapeDtypeStruct((t, d), F32),
        compiler_params=_cparams(2),
        name="moe_experts",
    )(tile_gid, nvalid, n_used, tok_of_row, hext.reshape(t, dx), wg_all, wu_all, wd_all)


def _final_body(x_ref, y_ref, mod_ref, g_ref, o_ref):
    x = x_ref[0] + mod_ref[0, 5:6, :] * y_ref[...]
    ms = jnp.mean(x * x, axis=-1, keepdims=True)
    o_ref[0] = x * lax.rsqrt(ms + EPS) * g_ref[...]


def _final(x, y, mod, g, tm=512):
    b, s, d = x.shape
    nt = s // tm
    return pl.pallas_call(
        _final_body,
        grid=(b, nt),
        in_specs=[pl.BlockSpec((1, tm, d), lambda bi, i: (bi, i, 0)),
                  pl.BlockSpec((tm, d), lambda bi, i: (bi * nt + i, 0)),
                  pl.BlockSpec((1, N_MOD, d), lambda bi, i: (bi, 0, 0)),
                  pl.BlockSpec((1, d), lambda bi, i: (0, 0))],
        out_specs=pl.BlockSpec((1, tm, d), lambda bi, i: (bi, i, 0)),
        out_shape=jax.ShapeDtypeStruct(x.shape, F32),
        compiler_params=_cparams(2),
        name="final_norm",
    )(x, y, mod, g.reshape(1, d))


def kernel(x, c, w_in, w_out, attn_norm, ffn_norm, w_ada, b_ada, lambda_q1, lambda_k1, lambda_q2, lambda_k2,
           diff_subln, hgrn_lb, hgrn_norm, rel_bias, w_router, b_router, w_gate, w_up, w_down, final_norm):
    b, s, d = x.shape
    depth = w_in.shape[0]

    w_in_cur = w_in[0].astype(BF16)
    wr =jnp.pad(w_router.astype(F32), ((0, 0), (0, LANES - N_EXPERTS)))
    wr_hi = wr.astype(BF16)
    wr_lo = (wr - wr_hi.astype(F32)).astype(BF16)
    br = jnp.pad(b_router.astype(F32), (0, LANES - N_EXPERTS)).reshape(1, LANES)
    lb_soft = jax.nn.softmax(hgrn_lb.astype(F32), axis=0)
    lower_bounds = jnp.maximum(jnp.cumsum(lb_soft, axis=0) - lb_soft[0:1], 0.0).reshape(depth, 1, -1)
    attn_norm3 = attn_norm.astype(F32).reshape(depth, 1, d)
    ffn_norm3 = ffn_norm.astype(F32).reshape(depth, 1, d)
    subln3 = diff_subln.astype(F32).reshape(depth, 1, HEAD)
    hnorm3 = hgrn_norm.astype(F32).reshape(depth, 1, HEAD)

    c8 = jnp.pad(c.astype(F32), ((0, 8 - b), (0, 0)))
    b_ada3 = b_ada.astype(F32).reshape(depth, 1, -1)
    mod_rows = _ada(c8, w_ada, b_ada3, 1)[0]
    bias = _bias_tiles(rel_bias, ATTN_T)

    y = None
    mod_prev = None
    for layer in range(depth):
        mod = mod_rows[:b].reshape(b, N_MOD, d)
        lambda_init = 0.8 - 0.6 * math.exp(-0.3 * layer)
        lam = (jnp.exp(jnp.sum(lambda_q1[layer].astype(F32) * lambda_k1[layer].astype(F32)))
               - jnp.exp(jnp.sum(lambda_q2[layer].astype(F32) * lambda_k2[layer].astype(F32)))
               + lambda_init)
        scalars = jnp.stack([lam, jnp.asarray(1.0 - lambda_init, F32)]).astype(F32)

        x, proj = _inproj(x, y, mod_prev, mod, attn_norm3, w_in_cur, layer)
        casts = [(w_out, layer), (w_gate, layer), (w_up, layer), (w_down, layer)]
        if layer + 1 < depth:
            casts.append((w_in, layer + 1))
        ada_next = (c8, w_ada, b_ada3, layer + 1) if layer + 1 < depth else None
        attn_o, converted, mod_rows = _attention(proj, bias, scalars, subln3, layer, casts, ada_next)
        w_out_b, w_gate_b, w_up_b, w_down_b = converted[:4]
        if layer + 1 < depth:
            w_in_cur = converted[4]
        hgrn_o = _hgrn(proj, lower_bounds, hnorm3, layer)
        x, hext = _outproj_router(attn_o, hgrn_o, w_out_b, x, mod, ffn_norm3, wr_hi, wr_lo, br, layer)
        y = _moe(hext, w_gate_b, w_up_b, w_down_b)
        mod_prev = mod
    return _final(x, y, mod_prev, final_norm.astype(F32))
```

```python
import functools
import math

import numpy as np
import jax
import jax.numpy as jnp
from jax import lax
from jax.experimental import pallas as pl
from jax.experimental.pallas import tpu as pltpu

F32 = jnp.float32
BF16 = jnp.bfloat16
EPS = 1e-6

LANES = 128
HEAD = 128
QK = 64
N_HEADS = 8
NUM_BUCKETS = 32
MAX_DISTANCE = 128
N_EXPERTS = 16
N_GROUPS = 4
GROUP_SIZE = N_EXPERTS // N_GROUPS
N_MOD = 6
NEG = -1e30

ATTN_T = 512
ATTN_CW = 512
ONES_ROWS = 16
LOG2E = math.log2(math.e)
HGRN_L = 4096
HGRN_C = 64
HGRN_SUB = 8
HGRN_GROUP = 256
HGRN_SAFE_SPAN = 80.0
NORM_ROWS = 256
MOE_TM = 512
MOE_ISSUE_UNROLL = MOE_TM
VMEM_LIMIT = 56 * 1024 * 1024


def _cparams(n_axes):
    return pltpu.CompilerParams(dimension_semantics=("arbitrary",) * n_axes, vmem_limit_bytes=VMEM_LIMIT)


def _ada_body(c_ref, w_ref, b_ref, o_ref):
    c = c_ref[...]
    ca = (c * jax.nn.sigmoid(c)).astype(BF16)
    o_ref[0] = jnp.dot(ca, w_ref[0].astype(BF16), preferred_element_type=F32) + b_ref[0]


def _ada(c8, w_ada, b_ada, tn=1024):
    depth, d, n = w_ada.shape
    return pl.pallas_call(
        _ada_body,
        grid=(depth, n // tn),
        in_specs=[pl.BlockSpec((8, d), lambda l, j: (0, 0)),
                  pl.BlockSpec((1, d, tn), lambda l, j: (l, 0, j)),
                  pl.BlockSpec((1, 1, tn), lambda l, j: (l, 0, j))],
        out_specs=pl.BlockSpec((1, 8, tn), lambda l, j: (l, 0, j)),
        out_shape=jax.ShapeDtypeStruct((depth, 8, n), F32),
        compiler_params=_cparams(2),
        name="ada_mod",
    )(c8, w_ada, b_ada.reshape(depth, 1, n))


def _modulated_norm(x, g, scale, shift):
    ms = jnp.mean(x * x, axis=-1, keepdims=True)
    return (x * lax.rsqrt(ms + EPS) * g) * (1.0 + scale) + shift


def _inproj_body(*refs, has_res):
    if has_res:
        x_ref, y_ref, gp_ref, mod_ref, g_ref, w_ref, xo_ref, p_ref, h_scr = refs
    else:
        x_ref, mod_ref, g_ref, w_ref, p_ref, h_scr = refs

    @pl.when(pl.program_id(2) == 0)
    def _():
        tm = h_scr.shape[0]
        for r0 in range(0, tm, NORM_ROWS):
            rows = slice(r0, r0 + NORM_ROWS)
            x = x_ref[0, rows, :]
            if has_res:
                x = x + gp_ref[0, 5:6, :] * y_ref[rows, :]
                xo_ref[0, rows, :] = x
            h = _modulated_norm(x, g_ref[0], mod_ref[0, 1:2, :], mod_ref[0, 0:1, :])
            h_scr[rows, :] = h.astype(BF16)

    p_ref[0] = jnp.dot(h_scr[...], w_ref[0], preferred_element_type=F32).astype(BF16)


def _inproj(x, y, mod_prev, mod, g_all, w, layer, tm=1024, tn=1024):
    w_all = w[None]
    b, s, d = x.shape
    n = w_all.shape[-1]
    nt = s // tm
    has_res = y is not None
    xspec = pl.BlockSpec((1, tm, d), lambda bi, i, j: (bi, i, 0))
    modspec = pl.BlockSpec((1, N_MOD, d), lambda bi, i, j: (bi, 0, 0))
    once = dict(pipeline_mode=pl.Buffered(1))
    in_specs = [pl.BlockSpec((1, tm, d), lambda bi, i, j: (bi, i, 0), **once)]
    args = [x]
    if has_res:
        in_specs += [pl.BlockSpec((tm, d), lambda bi, i, j: (bi * nt + i, 0), **once), modspec]
        args += [y, mod_prev]
    in_specs += [modspec,
                 pl.BlockSpec((1, 1, d), lambda bi, i, j: (layer, 0, 0)),
                 pl.BlockSpec((1, d, tn), lambda bi, i, j: (0, 0, j))]
    args += [mod, g_all, w_all]
    pspec = pl.BlockSpec((1, tm, tn), lambda bi, i, j: (bi, i, j))
    pshape = jax.ShapeDtypeStruct((b, s, n), BF16)
    if has_res:
        out_specs, out_shape = [xspec, pspec], [jax.ShapeDtypeStruct(x.shape, F32), pshape]
    else:
        out_specs, out_shape = pspec, pshape
    out = pl.pallas_call(
        functools.partial(_inproj_body, has_res=has_res),
        grid=(b, nt, n // tn),
        in_specs=in_specs, out_specs=out_specs, out_shape=out_shape,
        scratch_shapes=[pltpu.VMEM((tm, d), BF16)],
        compiler_params=_cparams(3),
        name="inproj",
    )(*args)
    return (out[0], out[1]) if has_res else (x, out)


def _t5_bucket_np(n):
    max_exact = NUM_BUCKETS // 2
    nf = np.maximum(n, 1).astype(np.float32)
    large = max_exact + (np.log(nf / np.float32(max_exact)) / np.float32(math.log(MAX_DISTANCE / max_exact))
                         * np.float32(NUM_BUCKETS - max_exact)).astype(np.int32)
    large = np.minimum(large, NUM_BUCKETS - 1)
    return np.where(n < max_exact, n, large).astype(np.int32)


def _bias_body(tab_ref, bd_ref, bu_ref, o_ref):
    t = bd_ref.shape[0]
    tab = jnp.broadcast_to(tab_ref[0], (t, LANES))
    for ti, idx_ref in enumerate((bd_ref, bu_ref)):
        for cb in range(t // LANES):
            cols = slice(cb * LANES, (cb + 1) * LANES)
            idx = idx_ref[:, cols]
            val = jnp.take_along_axis(tab, jnp.maximum(idx, 0), axis=1)
            o_ref[0, ti, :, cols] = jnp.where(idx < 0, NEG, val)


def _bias_tiles(rel_bias, t):
    key = np.arange(t)[:, None]
    qry = np.arange(t)[None, :]
    bd = np.where(qry >= key, _t5_bucket_np(np.maximum(qry - key, 0)), -1).astype(np.int32)
    bu = _t5_bucket_np(t + qry - key)
    assert MAX_DISTANCE <= t + 1, "tiles two or more to the left must lie in the far bucket"
    tab = (rel_bias.astype(F32) - rel_bias[NUM_BUCKETS - 1:].astype(F32)) * LOG2E
    tab = jnp.pad(tab.T, ((0, 0), (0, LANES - NUM_BUCKETS))).reshape(2 * N_HEADS, 1, LANES)
    return pl.pallas_call(
        _bias_body,
        grid=(N_HEADS, 2),
        in_specs=[pl.BlockSpec((1, 1, LANES), lambda h, m: (2 * h + m, 0, 0)),
                  pl.BlockSpec((t, t), lambda h, m: (0, 0)),
                  pl.BlockSpec((t, t), lambda h, m: (0, 0))],
        out_specs=pl.BlockSpec((1, 2, t, t), lambda h, m: (h, 0, 0, m)),
        out_shape=jax.ShapeDtypeStruct((N_HEADS, 2, t, 2 * t), F32),
        compiler_params=_cparams(2),
        name="bias_tiles",
    )(tab, jnp.asarray(bd), jnp.asarray(bu))


def _attn_body(*refs, t, cw, n_cast):
    sc_ref, q_ref, k_ref, v_ref, bias_ref, g_ref = refs[:6]
    w_refs = refs[6:6 + n_cast]
    o_ref = refs[6 + n_cast]
    wb_refs = refs[7 + n_cast:7 + 2 * n_cast]
    qq_scr, vt_scr, m_scr, acc_scr, s0_scr, s1_scr = refs[7 + 2 * n_cast:]
    for w_ref, wb_ref in zip(w_refs, wb_refs):
        wb_ref[0] = w_ref[0, 0].astype(BF16)

    qi = pl.program_id(2)
    s_len = k_ref.shape[1]

    @pl.when(qi == 0)
    def _():
        for ci in range(s_len // t):
            vt_scr[0:HEAD, ci * t:(ci + 1) * t] = v_ref[0, ci * t:(ci + 1) * t, :].T
        row = lax.broadcasted_iota(jnp.int32, (ONES_ROWS, s_len), 0)
        vt_scr[HEAD:HEAD + ONES_ROWS, :] = jnp.where(row == 0, 1.0, 0.0).astype(BF16)

    q = (q_ref[0].astype(F32) * (QK ** -0.5 * LOG2E)).astype(BF16)
    lane = lax.broadcasted_iota(jnp.int32, q.shape, 1)
    zero = jnp.zeros_like(q)
    qq_scr[0:t, :] = jnp.where(lane < QK, q, zero)
    qq_scr[t:2 * t, :] = jnp.where(lane >= QK, q, zero)
    m_scr[...] = jnp.full(m_scr.shape, NEG, F32)
    acc_scr[...] = jnp.zeros(acc_scr.shape, F32)

    tiles = [slice(ct * cw, (ct + 1) * cw) for ct in range(2 * t // cw)]

    bufs = (s0_scr, s1_scr)

    nt_dims = (((1,), (1,)), ((), ()))
    hk = t // 2
    diag_parts = [(k0, k1, slice(mi * t + k0, (mi + 1) * t)) for mi in range(2) for k0, k1 in ((0, hk), (hk, t))]

    def scores_into(j, buf):
        kb = k_ref[0, pl.ds(pl.multiple_of(j * t, t), t), :]
        for cs in tiles:
            buf[:, cs] = lax.dot_general(kb, qq_scr[cs, :], nt_dims, preferred_element_type=F32)

    def diag_scores_into(j, buf):
        for k0, k1, cs in diag_parts:
            kb = k_ref[0, pl.ds(pl.multiple_of(j * t + k0, hk), k1 - k0), :]
            buf[k0:k1, cs] = lax.dot_general(kb, qq_scr[cs, :], nt_dims, preferred_element_type=F32)

    def consume_part(j, buf, k0, k1, cs, bias_of):
        vt = vt_scr[:, pl.ds(pl.multiple_of(j * t + k0, hk), k1 - k0)]
        s = buf[k0:k1, cs]
        if bias_of is not None:
            s = s + bias_of(k0, k1, cs)
        m_prev = m_scr[:, cs]
        m_new = jnp.maximum(m_prev, jnp.max(s, axis=0, keepdims=True))
        alpha = jnp.exp2(m_prev - m_new)
        p = jnp.exp2(s - m_new)
        acc_scr[:, cs] = alpha * acc_scr[:, cs] + jnp.dot(vt, p.astype(BF16), preferred_element_type=F32)
        m_scr[:, cs] = m_new

    def consume(j, buf, bias_of):
        for cs in tiles:
            consume_part(j, buf, 0, t, cs, bias_of)

    def diag_consume(j, buf):
        for k0, k1, cs in diag_parts:
            consume_part(j, buf, k0, k1, cs, lambda a, b, c: bias_ref[0, 0, a:b, c])

    near = lambda a, b, c: bias_ref[0, 1, a:b, c]

    @pl.when(qi == 0)
    def _():
        diag_scores_into(0, bufs[0])

    @pl.when(qi > 0)
    def _():
        scores_into(0, bufs[0])

    n_far = jnp.maximum(qi - 1, 0)

    def far_pair(pi, carry):
        j = 2 * pi
        scores_into(j + 1, bufs[1])
        consume(j, bufs[0], None)
        scores_into(j + 2, bufs[0])
        consume(j + 1, bufs[1], None)
        return carry

    lax.fori_loop(0, n_far // 2, far_pair, 0)

    @pl.when(qi == 0)
    def _():
        diag_consume(0, bufs[0])

    @pl.when(jnp.logical_and(qi >= 1, lax.rem(qi, 2) == 1))
    def _():
        diag_scores_into(qi, bufs[1])
        consume(qi - 1, bufs[0], near)
        diag_consume(qi, bufs[1])

    @pl.when(jnp.logical_and(qi >= 2, lax.rem(qi, 2) == 0))
    def _():
        scores_into(qi - 1, bufs[1])
        consume(qi - 2, bufs[0], None)
        diag_scores_into(qi, bufs[0])
        consume(qi - 1, bufs[1], near)
        diag_consume(qi, bufs[0])

    inv = 1.0 / acc_scr[HEAD:HEAD + 1, :]
    o = acc_scr[0:HEAD, 0:t] * inv[:, 0:t] - acc_scr[0:HEAD, t:2 * t] * (sc_ref[0] * inv[:, t:2 * t])
    ms = jnp.mean(o * o, axis=0, keepdims=True)
    o = (o * lax.rsqrt(ms + EPS)).T
    o_ref[0] = ((o * g_ref[0]) * sc_ref[1]).astype(BF16)


def _attention(proj, bias, scalars, subln_all, layer, casts, t=ATTN_T):
    b, s, _ = proj.shape
    nq = s // t
    nsteps = b * N_HEADS * nq
    step = lambda bi, h, i: (bi * N_HEADS + h) * nq + i
    cast_in, cast_specs, cast_out_specs, cast_out_shapes = [], [], [], []
    for w_all, li in casts:
        cols = w_all.shape[-1]
        rows = math.prod(w_all.shape[1:-1]) // nsteps
        cast_in.append(w_all.reshape(w_all.shape[0], nsteps, rows, cols))
        cast_specs.append(pl.BlockSpec((1, 1, rows, cols), lambda bi, h, i, li=li: (li, step(bi, h, i), 0, 0)))
        cast_out_specs.append(pl.BlockSpec((1, rows, cols), lambda bi, h, i: (step(bi, h, i), 0, 0)))
        cast_out_shapes.append(jax.ShapeDtypeStruct((nsteps, rows, cols), BF16))
    out = pl.pallas_call(
        functools.partial(_attn_body, t=t, cw=ATTN_CW, n_cast=len(casts)),
        grid=(b, N_HEADS, nq),
        in_specs=[pl.BlockSpec(memory_space=pltpu.SMEM),
                  pl.BlockSpec((1, t, HEAD), lambda bi, h, i: (bi, i, h)),
                  pl.BlockSpec((1, s, HEAD), lambda bi, h, i: (bi, 0, N_HEADS + h)),
                  pl.BlockSpec((1, s, HEAD), lambda bi, h, i: (bi, 0, 2 * N_HEADS + h)),
                  pl.BlockSpec((1, 2, t, 2 * t), lambda bi, h, i: (h, 0, 0, 0)),
                  pl.BlockSpec((1, 1, HEAD), lambda bi, h, i: (layer, 0, 0))] + cast_specs,
        out_specs=[pl.BlockSpec((1, t, HEAD), lambda bi, h, i: (bi, i, h))] + cast_out_specs,
        out_shape=[jax.ShapeDtypeStruct((b, s, N_HEADS * HEAD), BF16)] + cast_out_shapes,
        scratch_shapes=[pltpu.VMEM((2 * t, HEAD), BF16),
                        pltpu.VMEM((HEAD + ONES_ROWS, s), BF16),
                        pltpu.VMEM((1, 2 * t), F32),
                        pltpu.VMEM((HEAD + ONES_ROWS, 2 * t), F32),
                        pltpu.VMEM((t, 2 * t), F32),
                        pltpu.VMEM((t, 2 * t), F32)],
        compiler_params=_cparams(3),
        name="diff_attention",
    )(scalars, proj, proj, proj, bias, subln_all, *cast_in)
    return out[0], [o.reshape(w_all.shape[1:]) for o, (w_all, _) in zip(out[1:], casts)]


def _hgrn_levels(c, sub):
    levels = []
    g = c // 2
    while g >= sub:
        levels.append(g)
        g //= 2
    return levels


def _hgrn_consts(l, c, sub):
    r = np.arange(HGRN_GROUP)[:, None]
    s = np.arange(HGRN_GROUP)[None, :]
    lcum = ((r // c == s // c) & (s <= r)).astype(np.float32)
    rc = np.arange(c)[:, None]
    sc = np.arange(c)[None, :]
    masks = [((rc // (2 * g) == sc // (2 * g)) & (rc % (2 * g) >= g) & (sc % (2 * g) < g)).astype(np.float32)
             for g in _hgrn_levels(c, sub)]
    half = c // 2
    rg = np.arange(HGRN_GROUP)[:, None]
    sg = np.arange(HGRN_GROUP)[None, :]
    top = (rg // c == sg // c) & (rg % c >= half) & (sg % c < half)
    near = (rg // half == sg // half) & (sg <= rg)
    fmask = np.stack([top, near]).astype(np.float32)
    return jnp.asarray(lcum, BF16), jnp.asarray(np.stack(masks), F32), jnp.asarray(fmask, F32)


def _hgrn_body(q_ref, f_ref, i_ref, g_ref, lb_ref, ng_ref, lcum_ref, mask_ref, fmask_ref, o_ref,
               st_scr, stb_scr, a_scr, oi_scr, *, l, c, sub):
    nchunk = l // c
    levels = _hgrn_levels(c, sub)

    @pl.when(pl.program_id(2) == 0)
    def _():
        st_scr[...] = jnp.zeros(st_scr.shape, F32)

    q = q_ref[0].astype(F32)
    fr = f_ref[0].astype(F32)
    vb = i_ref[0]
    lb = lb_ref[0]
    f = lb + (1.0 - lb) * jax.nn.sigmoid(fr)
    logf = jnp.log(jnp.maximum(f, jnp.finfo(F32).tiny))
    k = (1.0 - lb) * jax.nn.sigmoid(-fr)

    hi = logf.astype(BF16)
    lo = (logf - hi.astype(F32)).astype(BF16)
    hilo = jnp.concatenate([hi, lo], axis=1)
    lcum = lcum_ref[...]
    grp = lcum.shape[0]
    parts = []
    for gi in range(l // grp):
        both = jnp.dot(lcum, hilo[gi * grp:(gi + 1) * grp], preferred_element_type=F32)
        parts.append(both[:, 0:HEAD] + both[:, HEAD:2 * HEAD])
    bcum = jnp.concatenate(parts, axis=0)

    def rows_of(arr, group, row):
        a3 = arr.reshape(l // group, group, HEAD)
        return jnp.broadcast_to(a3[:, row:row + 1, :], a3.shape).reshape(l, HEAD)

    nt = (((1,), (1,)), ((), ()))
    tn = (((0,), (0,)), ((), ()))

    def level_operands(g):
        ref = rows_of(bcum, 2 * g, g - 1)
        return ((q * jnp.exp(jnp.minimum(bcum - ref, 0.0))).astype(BF16),
                (k * jnp.exp(jnp.minimum(ref - bcum, 0.0))).astype(BF16))

    q_top, k_top = level_operands(levels[0])

    def top_scores(sl):
        return mask_ref[0] * lax.dot_general(q_top[sl], k_top[sl], nt, preferred_element_type=F32)

    half = c // 2
    b_start = rows_of(bcum, half, 0) - rows_of(logf, half, 0)
    span = b_start - bcum
    bounded = jnp.max(span) <= HGRN_SAFE_SPAN

    @pl.when(bounded)
    def _():
        q_f = (q * jnp.exp(-span)).astype(BF16)
        k_f = (k * jnp.exp(span)).astype(BF16)
        grp = fmask_ref.shape[-1]
        for gi in range(l // grp):
            sl = slice(gi * grp, (gi + 1) * grp)
            a = fmask_ref[0] * lax.dot_general(q_top[sl], k_top[sl], nt, preferred_element_type=F32)
            a = a + fmask_ref[1] * lax.dot_general(q_f[sl], k_f[sl], nt, preferred_element_type=F32)
            a_scr[sl, :] = jnp.dot(a.astype(BF16), vb[sl], preferred_element_type=F32)

    @pl.when(jnp.logical_not(bounded))
    def _():
        lower = [level_operands(g) for g in levels[1:]]
        nb = l // sub
        b3 = bcum.reshape(nb, sub, HEAD)
        q3 = q.reshape(nb, sub, HEAD)
        k3 = k.reshape(nb, sub, HEAD)
        v3 = vb.astype(F32).reshape(nb, sub, HEAD)
        tloc = lax.broadcasted_iota(jnp.int32, (nb, sub, 1), 1)
        od = jnp.zeros((nb, sub, HEAD), F32)
        for si in range(sub):
            e = jnp.exp(b3 - b3[:, si:si + 1, :])
            pr = q3 * (k3[:, si:si + 1, :] * e)
            w = jnp.sum(pr, axis=-1, keepdims=True)
            w = jnp.where(tloc >= si, w, 0.0)
            od = od + w * v3[:, si:si + 1, :]
        od = od.reshape(l, HEAD)
        for ci in range(nchunk):
            sl = slice(ci * c, (ci + 1) * c)
            a = top_scores(sl)
            for li, (q_g, k_g) in enumerate(lower):
                a = a + mask_ref[li + 1] * lax.dot_general(q_g[sl], k_g[sl], nt, preferred_element_type=F32)
            a_scr[sl, :] = jnp.dot(a.astype(BF16), vb[sl], preferred_element_type=F32) + od[sl]

    b_end = rows_of(bcum, c, c - 1)
    q_in = (q * jnp.exp(bcum)).astype(BF16)
    k_out = (k * jnp.exp(b_end - bcum)).astype(BF16)
    dec = jnp.exp(b_end)
    kv = [lax.dot_general(vb[ci * c:(ci + 1) * c], k_out[ci * c:(ci + 1) * c], tn, preferred_element_type=F32)
          for ci in range(nchunk)]
    st = st_scr[...]
    for ci in range(nchunk):
        stb_scr[ci] = st.astype(BF16)
        st = st * dec[ci * c:ci * c + 1, :] + kv[ci]
    st_scr[...] = st
    for ci in range(nchunk):
        sl = slice(ci * c, (ci + 1) * c)
        oi_scr[sl, :] = lax.dot_general(q_in[sl], stb_scr[ci], nt, preferred_element_type=F32)

    o = a_scr[...] + oi_scr[...]
    ms = jnp.mean(o * o, axis=-1, keepdims=True)
    o = o * lax.rsqrt(ms + EPS) * ng_ref[0]
    gate = g_ref[0].astype(F32)
    o_ref[0] = (o * (gate * jax.nn.sigmoid(gate))).astype(BF16)


def _hgrn(proj, lower_bounds, norm_all, layer, l=HGRN_L, c=HGRN_C, sub=HGRN_SUB):
    b, s, _ = proj.shape
    lcum, masks, fmask = _hgrn_consts(l, c, sub)
    base = 3 * N_HEADS

    def col(kind):
        return pl.BlockSpec((1, l, HEAD), lambda bi, h, i: (bi, i, base + kind * N_HEADS + h))

    return pl.pallas_call(
        functools.partial(_hgrn_body, l=l, c=c, sub=sub),
        grid=(b, N_HEADS, s // l),
        in_specs=[col(0), col(1), col(2), col(3),
                  pl.BlockSpec((1, 1, HEAD), lambda bi, h, i: (layer, 0, h)),
                  pl.BlockSpec((1, 1, HEAD), lambda bi, h, i: (layer, 0, 0)),
                  pl.BlockSpec(lcum.shape, lambda bi, h, i: (0, 0)),
                  pl.BlockSpec(masks.shape, lambda bi, h, i: (0, 0, 0)),
                  pl.BlockSpec(fmask.shape, lambda bi, h, i: (0, 0, 0))],
        out_specs=pl.BlockSpec((1, l, HEAD), lambda bi, h, i: (bi, i, h)),
        out_shape=jax.ShapeDtypeStruct((b, s, N_HEADS * HEAD), BF16),
        scratch_shapes=[pltpu.VMEM((HEAD, HEAD), F32),
                        pltpu.VMEM((l // c, HEAD, HEAD), BF16),
                        pltpu.VMEM((l, HEAD), F32),
                        pltpu.VMEM((l, HEAD), F32)],
        compiler_params=_cparams(3),
        name="hgrn2",
    )(proj, proj, proj, proj, lower_bounds, norm_all, lcum, masks, fmask)


def _first_row(cond, row):
    return jnp.min(jnp.where(cond, row, N_EXPERTS), axis=0, keepdims=True)


def _outproj_body(a_ref, hg_ref, w_ref, x_ref, mod_ref, g_ref, wrh_ref, wrl_ref, br_ref, xo_ref, h_ref, *, d):
    da = a_ref.shape[-1]
    mixed = jnp.dot(a_ref[0], w_ref[0, 0:da, :], preferred_element_type=F32)
    mixed = mixed + jnp.dot(hg_ref[0], w_ref[0, da:, :], preferred_element_type=F32)
    x = x_ref[0] + mod_ref[0, 2:3, :] * mixed
    xo_ref[0] = x
    h = _modulated_norm(x, g_ref[0], mod_ref[0, 4:5, :], mod_ref[0, 3:4, :])
    h_ref[0, :, 0:d] = h

    h_hi = h.astype(BF16)
    h_lo = (h - h_hi.astype(F32)).astype(BF16)
    logits = (jnp.dot(h_hi, wrh_ref[...], preferred_element_type=F32)
              + jnp.dot(h_lo, wrh_ref[...], preferred_element_type=F32)
              + jnp.dot(h_hi, wrl_ref[...], preferred_element_type=F32)) + br_ref[...]
    lt = logits.T[0:N_EXPERTS, :]
    erow = lax.broadcasted_iota(jnp.int32, lt.shape, 0)
    mx = jnp.max(lt, axis=0, keepdims=True)
    ex = jnp.exp(lt - mx)
    probs = ex / jnp.sum(ex, axis=0, keepdims=True)

    best = sel = v1 = v2 = i1 = i2 = None
    for gi in range(N_GROUPS):
        ing = (erow >= gi * GROUP_SIZE) & (erow < (gi + 1) * GROUP_SIZE)
        pg = jnp.where(ing, probs, -1.0)
        m1 = jnp.max(pg, axis=0, keepdims=True)
        a1 = _first_row(pg == m1, erow)
        pg2 = jnp.where(erow == a1, -1.0, pg)
        m2 = jnp.max(pg2, axis=0, keepdims=True)
        a2 = _first_row(pg2 == m2, erow)
        score = m1 + m2
        if gi == 0:
            best, sel, v1, v2, i1, i2 = score, jnp.zeros_like(a1), m1, m2, a1, a2
        else:
            better = score > best
            best = jnp.where(better, score, best)
            sel = jnp.where(better, gi, sel)
            v1 = jnp.where(better, m1, v1)
            v2 = jnp.where(better, m2, v2)
            i1 = jnp.where(better, a1, i1)
            i2 = jnp.where(better, a2, i2)
    tot = v1 + v2
    w1 = v1 / tot
    w2 = v2 / tot
    loc1 = i1 - sel * GROUP_SIZE
    loc2 = i2 - sel * GROUP_SIZE
    xrow = lax.broadcasted_iota(jnp.int32, (LANES, lt.shape[1]), 0)
    ext = jnp.where(xrow == loc1, w1, 0.0) + jnp.where(xrow == loc2, w2, 0.0)
    ext = jnp.where(xrow == GROUP_SIZE, sel.astype(F32), ext)
    h_ref[0, :, d:] = ext.T


def _outproj_router(attn_o, hgrn_o, w_out, x, mod, ffn_norm_all, wr_hi, wr_lo, br, layer, tm=512):
    w_out_all = w_out[None]
    b, s, d = x.shape
    da = attn_o.shape[-1]
    dmix = w_out_all.shape[1]
    tok = lambda bi, i: (bi, i, 0)
    return pl.pallas_call(
        functools.partial(_outproj_body, d=d),
        grid=(b, s // tm),
        in_specs=[pl.BlockSpec((1, tm, da), tok),
                  pl.BlockSpec((1, tm, dmix - da), tok),
                  pl.BlockSpec((1, dmix, d), lambda bi, i: (0, 0, 0)),
                  pl.BlockSpec((1, tm, d), tok),
                  pl.BlockSpec((1, N_MOD, d), lambda bi, i: (bi, 0, 0)),
                  pl.BlockSpec((1, 1, d), lambda bi, i: (layer, 0, 0)),
                  pl.BlockSpec((d, LANES), lambda bi, i: (0, 0)),
                  pl.BlockSpec((d, LANES), lambda bi, i: (0, 0)),
                  pl.BlockSpec((1, LANES), lambda bi, i: (0, 0))],
        out_specs=[pl.BlockSpec((1, tm, d), tok),
                   pl.BlockSpec((1, tm, d + LANES), tok)],
        out_shape=[jax.ShapeDtypeStruct((b, s, d), F32),
                   jax.ShapeDtypeStruct((b, s, d + LANES), F32)],
        compiler_params=_cparams(2),
        name="outproj_router",
    )(attn_o, hgrn_o, w_out_all, x, mod, ffn_norm_all, wr_hi, wr_lo, br)


def _moe_body(tg_ref, nv_ref, nu_ref, tok_ref, h_hbm, wg_ref, wu_ref, wd_ref, y_hbm,
              xbuf, xbf, acc, gsem, ssem, *, tm, d):
    i = pl.program_id(0)
    e = pl.program_id(1)
    n_used = nu_ref[0]
    slot = lax.rem(i, 2)

    def gather_start(tile, sl):
        base = tile * tm

        def body(r, carry):
            pltpu.make_async_copy(h_hbm.at[pl.ds(tok_ref[base + r], 1)], xbuf.at[sl, pl.ds(r, 1)], gsem.at[sl]).start()
            return carry

        lax.fori_loop(0, tm, body, 0, unroll=MOE_ISSUE_UNROLL)

    def gather_wait(sl):
        pltpu.make_async_copy(h_hbm.at[pl.ds(0, tm)], xbuf.at[sl], gsem.at[sl]).wait()

    def scatter_start(tile, sl, n):
        base = tile * tm

        def body(r, carry):
            pltpu.make_async_copy(acc.at[sl, pl.ds(r, 1)], y_hbm.at[pl.ds(tok_ref[base + r], 1)], ssem.at[sl]).start()
            return carry

        @pl.when(n == tm)
        def _():
            lax.fori_loop(0, tm, body, 0, unroll=MOE_ISSUE_UNROLL)

        @pl.when(n < tm)
        def _():
            lax.fori_loop(0, n, body, 0)

    def scatter_wait(sl, n):
        @pl.when(n == tm)
        def _():
            pltpu.make_async_copy(acc.at[sl], y_hbm.at[pl.ds(0, tm)], ssem.at[sl]).wait()

        @pl.when(n < tm)
        def _():
            def body(r, carry):
                pltpu.make_async_copy(acc.at[sl, pl.ds(r, 1)], y_hbm.at[pl.ds(r, 1)], ssem.at[sl]).wait()
                return carry

            lax.fori_loop(0, n, body, 0)

    @pl.when(i < n_used)
    def _():
        @pl.when(e == 0)
        def _():
            @pl.when(i == 0)
            def _():
                gather_start(0, 0)

            gather_wait(slot)

            @pl.when(i + 1 < n_used)
            def _():
                gather_start(i + 1, 1 - slot)

            xbf[...] = xbuf[slot, :, 0:d].astype(BF16)

            @pl.when(i >= 2)
            def _():
                scatter_wait(slot, nv_ref[jnp.maximum(i - 2, 0)])

            acc[slot] = jnp.zeros((tm, d), F32)

        ext = xbuf[slot, :, d:d + LANES]
        gate = jnp.zeros((tm, 1), F32)
        for ei in range(GROUP_SIZE):
            gate = jnp.where(e == ei, ext[:, ei:ei + 1], gate)
        x = xbf[...]
        hg = jnp.dot(x, wg_ref[0, 0], preferred_element_type=F32)
        hu = jnp.dot(x, wu_ref[0, 0], preferred_element_type=F32)
        act = (hg * jax.nn.sigmoid(hg)) * hu * gate
        acc[slot] += jnp.dot(act.astype(BF16), wd_ref[0, 0], preferred_element_type=F32)

        @pl.when(e == GROUP_SIZE - 1)
        def _():
            scatter_start(i, slot, nv_ref[i])

            @pl.when(i == n_used - 1)
            def _():
                scatter_wait(slot, nv_ref[i])

                @pl.when(i >= 1)
                def _():
                    scatter_wait(1 - slot, nv_ref[jnp.maximum(i - 1, 0)])


def _route(hext, d, tm):
    b, s, _ = hext.shape
    t = b * s
    n_tiles = t // tm + N_GROUPS
    gid = hext[:, :, d + GROUP_SIZE].reshape(t).astype(jnp.int32)
    onehot = (gid[:, None] == jnp.arange(N_GROUPS, dtype=jnp.int32)[None, :]).astype(jnp.int32)
    cnt = jnp.sum(onehot, axis=0)
    rank = jnp.sum(onehot * (jnp.cumsum(onehot, axis=0) - onehot), axis=1)
    ntile_g = (cnt + tm - 1) // tm
    tile_end_g = jnp.cumsum(ntile_g)
    tile_start_g = tile_end_g - ntile_g
    pos = (tile_start_g * tm)[gid] + rank
    tok_of_row = jnp.zeros((n_tiles * tm,), jnp.int32).at[pos].set(jnp.arange(t, dtype=jnp.int32))
    tile = jnp.arange(n_tiles, dtype=jnp.int32)
    tile_gid = jnp.minimum(jnp.sum((tile[:, None] >= tile_end_g[None, :]).astype(jnp.int32), axis=1), N_GROUPS - 1)
    nvalid = jnp.clip(cnt[tile_gid] - (tile - tile_start_g[tile_gid]) * tm, 0, tm).astype(jnp.int32)
    n_used = tile_end_g[N_GROUPS - 1:].astype(jnp.int32)
    return tile_gid.astype(jnp.int32), nvalid, n_used, tok_of_row


def _moe(hext, wg, wu, wd, tm=MOE_TM):
    wg_all, wu_all, wd_all, layer = wg[None], wu[None], wd[None], 0
    b, s, dx = hext.shape
    d = dx - LANES
    t = b * s
    f = wg_all.shape[-1]
    tile_gid, nvalid, n_used, tok_of_row = _route(hext, d, tm)
    n_tiles = tile_gid.shape[0]

    def wmap(i, e, tg, nv, nu, tok):
        return (layer, tg[i] * GROUP_SIZE + e, 0, 0)

    return pl.pallas_call(
        functools.partial(_moe_body, tm=tm, d=d),
        grid_spec=pltpu.PrefetchScalarGridSpec(
            num_scalar_prefetch=4,
            grid=(n_tiles, GROUP_SIZE),
            in_specs=[pl.BlockSpec(memory_space=pl.ANY),
                      pl.BlockSpec((1, 1, d, f), wmap),
                      pl.BlockSpec((1, 1, d, f), wmap),
                      pl.BlockSpec((1, 1, f, d), wmap)],
            out_specs=pl.BlockSpec(memory_space=pl.ANY),
            scratch_shapes=[pltpu.VMEM((2, tm, dx), F32),
                            pltpu.VMEM((tm, d), BF16),
                            pltpu.VMEM((2, tm, d), F32),
                            pltpu.SemaphoreType.DMA((2,)),
                            pltpu.SemaphoreType.DMA((2,))],
        ),
        out_shape=jax.ShapeDtypeStruct((t, d), F32),
        compiler_params=_cparams(2),
        name="moe_experts",
    )(tile_gid, nvalid, n_used, tok_of_row, hext.reshape(t, dx), wg_all, wu_all, wd_all)


def _final_body(x_ref, y_ref, mod_ref, g_ref, o_ref):
    x = x_ref[0] + mod_ref[0, 5:6, :] * y_ref[...]
    ms = jnp.mean(x * x, axis=-1, keepdims=True)
    o_ref[0] = x * lax.rsqrt(ms + EPS) * g_ref[...]


def _final(x, y, mod, g, tm=512):
    b, s, d = x.shape
    nt = s // tm
    return pl.pallas_call(
        _final_body,
        grid=(b, nt),
        in_specs=[pl.BlockSpec((1, tm, d), lambda bi, i: (bi, i, 0)),
                  pl.BlockSpec((tm, d), lambda bi, i: (bi * nt + i, 0)),
                  pl.BlockSpec((1, N_MOD, d), lambda bi, i: (bi, 0, 0)),
                  pl.BlockSpec((1, d), lambda bi, i: (0, 0))],
        out_specs=pl.BlockSpec((1, tm, d), lambda bi, i: (bi, i, 0)),
        out_shape=jax.ShapeDtypeStruct(x.shape, F32),
        compiler_params=_cparams(2),
        name="final_norm",
    )(x, y, mod, g.reshape(1, d))


def kernel(x, c, w_in, w_out, attn_norm, ffn_norm, w_ada, b_ada, lambda_q1, lambda_k1, lambda_q2, lambda_k2,
           diff_subln, hgrn_lb, hgrn_norm, rel_bias, w_router, b_router, w_gate, w_up, w_down, final_norm):
    b, s, d = x.shape
    depth = w_in.shape[0]

    w_in_cur = w_in[0].astype(BF16)
    wr =jnp.pad(w_router.astype(F32), ((0, 0), (0, LANES - N_EXPERTS)))
    wr_hi = wr.astype(BF16)
    wr_lo = (wr - wr_hi.astype(F32)).astype(BF16)
    br = jnp.pad(b_router.astype(F32), (0, LANES - N_EXPERTS)).reshape(1, LANES)
    lb_soft = jax.nn.softmax(hgrn_lb.astype(F32), axis=0)
    lower_bounds = jnp.maximum(jnp.cumsum(lb_soft, axis=0) - lb_soft[0:1], 0.0).reshape(depth, 1, -1)
    attn_norm3 = attn_norm.astype(F32).reshape(depth, 1, d)
    ffn_norm3 = ffn_norm.astype(F32).reshape(depth, 1, d)
    subln3 = diff_subln.astype(F32).reshape(depth, 1, HEAD)
    hnorm3 = hgrn_norm.astype(F32).reshape(depth, 1, HEAD)

    c8 = jnp.pad(c.astype(F32), ((0, 8 - b), (0, 0)))
    mods = _ada(c8, w_ada, b_ada)
    bias = _bias_tiles(rel_bias, ATTN_T)

    y = None
    mod_prev = None
    for layer in range(depth):
        mod = mods[layer, :b].reshape(b, N_MOD, d)
        lambda_init = 0.8 - 0.6 * math.exp(-0.3 * layer)
        lam = (jnp.exp(jnp.sum(lambda_q1[layer].astype(F32) * lambda_k1[layer].astype(F32)))
               - jnp.exp(jnp.sum(lambda_q2[layer].astype(F32) * lambda_k2[layer].astype(F32)))
               + lambda_init)
        scalars = jnp.stack([lam, jnp.asarray(1.0 - lambda_init, F32)]).astype(F32)

        x, proj = _inproj(x, y, mod_prev, mod, attn_norm3, w_in_cur, layer)
        casts = [(w_out, layer), (w_gate, layer), (w_up, layer), (w_down, layer)]
        if layer + 1 < depth:
            casts.append((w_in, layer + 1))
        attn_o, converted = _attention(proj, bias, scalars, subln3, layer, casts)
        w_out_b, w_gate_b, w_up_b, w_down_b = converted[:4]
        if layer + 1 < depth:
            w_in_cur = converted[4]
        hgrn_o = _hgrn(proj, lower_bounds, hnorm3, layer)
        x, hext = _outproj_router(attn_o, hgrn_o, w_out_b, x, mod, ffn_norm3, wr_hi, wr_lo, br, layer)
        y = _moe(hext, w_gate_b, w_up_b, w_down_b)
        mod_prev = mod
    return _final(x, y, mod_prev, final_norm.astype(F32))
```

```python
import functools
import math

import numpy as np
import jax
import jax.numpy as jnp
from jax import lax
from jax.experimental import pallas as pl
from jax.experimental.pallas import tpu as pltpu

F32 = jnp.float32
BF16 = jnp.bfloat16
EPS = 1e-6

LANES = 128
HEAD = 128
QK = 64
N_HEADS = 8
NUM_BUCKETS = 32
MAX_DISTANCE = 128
N_EXPERTS = 16
N_GROUPS = 4
GROUP_SIZE = N_EXPERTS // N_GROUPS
N_MOD = 6
NEG = -1e30

ATTN_T = 512
ATTN_CW = 512
ONES_ROWS = 16
LOG2E = math.log2(math.e)
HGRN_L = 4096
HGRN_C = 64
HGRN_SUB = 8
HGRN_GROUP = 256
HGRN_SAFE_SPAN = 80.0
NORM_ROWS = 256
MOE_TM = 512
VMEM_LIMIT = 56 * 1024 * 1024


def _cparams(n_axes):
    return pltpu.CompilerParams(dimension_semantics=("arbitrary",) * n_axes, vmem_limit_bytes=VMEM_LIMIT)


def _ada_body(c_ref, w_ref, b_ref, o_ref):
    c = c_ref[...]
    ca = (c * jax.nn.sigmoid(c)).astype(BF16)
    o_ref[0] = jnp.dot(ca, w_ref[0].astype(BF16), preferred_element_type=F32) + b_ref[0]


def _ada(c8, w_ada, b_ada, tn=1024):
    depth, d, n = w_ada.shape
    return pl.pallas_call(
        _ada_body,
        grid=(depth, n // tn),
        in_specs=[pl.BlockSpec((8, d), lambda l, j: (0, 0)),
                  pl.BlockSpec((1, d, tn), lambda l, j: (l, 0, j)),
                  pl.BlockSpec((1, 1, tn), lambda l, j: (l, 0, j))],
        out_specs=pl.BlockSpec((1, 8, tn), lambda l, j: (l, 0, j)),
        out_shape=jax.ShapeDtypeStruct((depth, 8, n), F32),
        compiler_params=_cparams(2),
        name="ada_mod",
    )(c8, w_ada, b_ada.reshape(depth, 1, n))


def _modulated_norm(x, g, scale, shift):
    ms = jnp.mean(x * x, axis=-1, keepdims=True)
    return (x * lax.rsqrt(ms + EPS) * g) * (1.0 + scale) + shift


def _inproj_body(*refs, has_res):
    if has_res:
        x_ref, y_ref, gp_ref, mod_ref, g_ref, w_ref, xo_ref, p_ref, h_scr = refs
    else:
        x_ref, mod_ref, g_ref, w_ref, p_ref, h_scr = refs

    @pl.when(pl.program_id(2) == 0)
    def _():
        tm = h_scr.shape[0]
        for r0 in range(0, tm, NORM_ROWS):
            rows = slice(r0, r0 + NORM_ROWS)
            x = x_ref[0, rows, :]
            if has_res:
                x = x + gp_ref[0, 5:6, :] * y_ref[rows, :]
                xo_ref[0, rows, :] = x
            h = _modulated_norm(x, g_ref[0], mod_ref[0, 1:2, :], mod_ref[0, 0:1, :])
            h_scr[rows, :] = h.astype(BF16)

    p_ref[0] = jnp.dot(h_scr[...], w_ref[0], preferred_element_type=F32).astype(BF16)


def _inproj(x, y, mod_prev, mod, g_all, w, layer, tm=1024, tn=1024):
    w_all = w[None]
    b, s, d = x.shape
    n = w_all.shape[-1]
    nt = s // tm
    has_res = y is not None
    xspec = pl.BlockSpec((1, tm, d), lambda bi, i, j: (bi, i, 0))
    modspec = pl.BlockSpec((1, N_MOD, d), lambda bi, i, j: (bi, 0, 0))
    once = dict(pipeline_mode=pl.Buffered(1))
    in_specs = [pl.BlockSpec((1, tm, d), lambda bi, i, j: (bi, i, 0), **once)]
    args = [x]
    if has_res:
        in_specs += [pl.BlockSpec((tm, d), lambda bi, i, j: (bi * nt + i, 0), **once), modspec]
        args += [y, mod_prev]
    in_specs += [modspec,
                 pl.BlockSpec((1, 1, d), lambda bi, i, j: (layer, 0, 0)),
                 pl.BlockSpec((1, d, tn), lambda bi, i, j: (0, 0, j))]
    args += [mod, g_all, w_all]
    pspec = pl.BlockSpec((1, tm, tn), lambda bi, i, j: (bi, i, j))
    pshape = jax.ShapeDtypeStruct((b, s, n), BF16)
    if has_res:
        out_specs, out_shape = [xspec, pspec], [jax.ShapeDtypeStruct(x.shape, F32), pshape]
    else:
        out_specs, out_shape = pspec, pshape
    out = pl.pallas_call(
        functools.partial(_inproj_body, has_res=has_res),
        grid=(b, nt, n // tn),
        in_specs=in_specs, out_specs=out_specs, out_shape=out_shape,
        scratch_shapes=[pltpu.VMEM((tm, d), BF16)],
        compiler_params=_cparams(3),
        name="inproj",
    )(*args)
    return (out[0], out[1]) if has_res else (x, out)


def _t5_bucket_np(n):
    max_exact = NUM_BUCKETS // 2
    nf = np.maximum(n, 1).astype(np.float32)
    large = max_exact + (np.log(nf / np.float32(max_exact)) / np.float32(math.log(MAX_DISTANCE / max_exact))
                         * np.float32(NUM_BUCKETS - max_exact)).astype(np.int32)
    large = np.minimum(large, NUM_BUCKETS - 1)
    return np.where(n < max_exact, n, large).astype(np.int32)


def _bias_body(tab_ref, bd_ref, bu_ref, o_ref):
    t = bd_ref.shape[0]
    tab = jnp.broadcast_to(tab_ref[0], (t, LANES))
    for ti, idx_ref in enumerate((bd_ref, bu_ref)):
        for cb in range(t // LANES):
            cols = slice(cb * LANES, (cb + 1) * LANES)
            idx = idx_ref[:, cols]
            val = jnp.take_along_axis(tab, jnp.maximum(idx, 0), axis=1)
            o_ref[0, ti, :, cols] = jnp.where(idx < 0, NEG, val)


def _bias_tiles(rel_bias, t):
    key = np.arange(t)[:, None]
    qry = np.arange(t)[None, :]
    bd = np.where(qry >= key, _t5_bucket_np(np.maximum(qry - key, 0)), -1).astype(np.int32)
    bu = _t5_bucket_np(t + qry - key)
    assert MAX_DISTANCE <= t + 1, "tiles two or more to the left must lie in the far bucket"
    tab = (rel_bias.astype(F32) - rel_bias[NUM_BUCKETS - 1:].astype(F32)) * LOG2E
    tab = jnp.pad(tab.T, ((0, 0), (0, LANES - NUM_BUCKETS))).reshape(2 * N_HEADS, 1, LANES)
    return pl.pallas_call(
        _bias_body,
        grid=(N_HEADS, 2),
        in_specs=[pl.BlockSpec((1, 1, LANES), lambda h, m: (2 * h + m, 0, 0)),
                  pl.BlockSpec((t, t), lambda h, m: (0, 0)),
                  pl.BlockSpec((t, t), lambda h, m: (0, 0))],
        out_specs=pl.BlockSpec((1, 2, t, t), lambda h, m: (h, 0, 0, m)),
        out_shape=jax.ShapeDtypeStruct((N_HEADS, 2, t, 2 * t), F32),
        compiler_params=_cparams(2),
        name="bias_tiles",
    )(tab, jnp.asarray(bd), jnp.asarray(bu))


def _attn_body(*refs, t, cw, n_cast):
    sc_ref, q_ref, k_ref, v_ref, bias_ref, g_ref = refs[:6]
    w_refs = refs[6:6 + n_cast]
    o_ref = refs[6 + n_cast]
    wb_refs = refs[7 + n_cast:7 + 2 * n_cast]
    qq_scr, vt_scr, m_scr, acc_scr, s0_scr, s1_scr = refs[7 + 2 * n_cast:]
    for w_ref, wb_ref in zip(w_refs, wb_refs):
        wb_ref[0] = w_ref[0, 0].astype(BF16)

    qi = pl.program_id(2)
    s_len = k_ref.shape[1]

    @pl.when(qi == 0)
    def _():
        for ci in range(s_len // t):
            vt_scr[0:HEAD, ci * t:(ci + 1) * t] = v_ref[0, ci * t:(ci + 1) * t, :].T
        row = lax.broadcasted_iota(jnp.int32, (ONES_ROWS, s_len), 0)
        vt_scr[HEAD:HEAD + ONES_ROWS, :] = jnp.where(row == 0, 1.0, 0.0).astype(BF16)

    q = (q_ref[0].astype(F32) * (QK ** -0.5 * LOG2E)).astype(BF16)
    lane = lax.broadcasted_iota(jnp.int32, q.shape, 1)
    zero = jnp.zeros_like(q)
    qq_scr[0:t, :] = jnp.where(lane < QK, q, zero)
    qq_scr[t:2 * t, :] = jnp.where(lane >= QK, q, zero)
    m_scr[...] = jnp.full(m_scr.shape, NEG, F32)
    acc_scr[...] = jnp.zeros(acc_scr.shape, F32)

    tiles = [slice(ct * cw, (ct + 1) * cw) for ct in range(2 * t // cw)]

    bufs = (s0_scr, s1_scr)

    nt_dims = (((1,), (1,)), ((), ()))
    hk = t // 2
    diag_parts = [(k0, k1, slice(mi * t + k0, (mi + 1) * t)) for mi in range(2) for k0, k1 in ((0, hk), (hk, t))]

    def scores_into(j, buf):
        kb = k_ref[0, pl.ds(pl.multiple_of(j * t, t), t), :]
        for cs in tiles:
            buf[:, cs] = lax.dot_general(kb, qq_scr[cs, :], nt_dims, preferred_element_type=F32)

    def diag_scores_into(j, buf):
        for k0, k1, cs in diag_parts:
            kb = k_ref[0, pl.ds(pl.multiple_of(j * t + k0, hk), k1 - k0), :]
            buf[k0:k1, cs] = lax.dot_general(kb, qq_scr[cs, :], nt_dims, preferred_element_type=F32)

    def consume_part(j, buf, k0, k1, cs, bias_of):
        vt = vt_scr[:, pl.ds(pl.multiple_of(j * t + k0, hk), k1 - k0)]
        s = buf[k0:k1, cs]
        if bias_of is not None:
            s = s + bias_of(k0, k1, cs)
        m_prev = m_scr[:, cs]
        m_new = jnp.maximum(m_prev, jnp.max(s, axis=0, keepdims=True))
        alpha = jnp.exp2(m_prev - m_new)
        p = jnp.exp2(s - m_new)
        acc_scr[:, cs] = alpha * acc_scr[:, cs] + jnp.dot(vt, p.astype(BF16), preferred_element_type=F32)
        m_scr[:, cs] = m_new

    def consume(j, buf, bias_of):
        for cs in tiles:
            consume_part(j, buf, 0, t, cs, bias_of)

    def diag_consume(j, buf):
        for k0, k1, cs in diag_parts:
            consume_part(j, buf, k0, k1, cs, lambda a, b, c: bias_ref[0, 0, a:b, c])

    near = lambda a, b, c: bias_ref[0, 1, a:b, c]

    @pl.when(qi == 0)
    def _():
        diag_scores_into(0, bufs[0])

    @pl.when(qi > 0)
    def _():
        scores_into(0, bufs[0])

    n_far = jnp.maximum(qi - 1, 0)

    def far_pair(pi, carry):
        j = 2 * pi
        scores_into(j + 1, bufs[1])
        consume(j, bufs[0], None)
        scores_into(j + 2, bufs[0])
        consume(j + 1, bufs[1], None)
        return carry

    lax.fori_loop(0, n_far // 2, far_pair, 0)

    @pl.when(qi == 0)
    def _():
        diag_consume(0, bufs[0])

    @pl.when(jnp.logical_and(qi >= 1, lax.rem(qi, 2) == 1))
    def _():
        diag_scores_into(qi, bufs[1])
        consume(qi - 1, bufs[0], near)
        diag_consume(qi, bufs[1])

    @pl.when(jnp.logical_and(qi >= 2, lax.rem(qi, 2) == 0))
    def _():
        scores_into(qi - 1, bufs[1])
        consume(qi - 2, bufs[0], None)
        diag_scores_into(qi, bufs[0])
        consume(qi - 1, bufs[1], near)
        diag_consume(qi, bufs[0])

    inv = 1.0 / acc_scr[HEAD:HEAD + 1, :]
    o = acc_scr[0:HEAD, 0:t] * inv[:, 0:t] - acc_scr[0:HEAD, t:2 * t] * (sc_ref[0] * inv[:, t:2 * t])
    ms = jnp.mean(o * o, axis=0, keepdims=True)
    o = (o * lax.rsqrt(ms + EPS)).T
    o_ref[0] = ((o * g_ref[0]) * sc_ref[1]).astype(BF16)


def _attention(proj, bias, scalars, subln_all, layer, casts, t=ATTN_T):
    b, s, _ = proj.shape
    nq = s // t
    nsteps = b * N_HEADS * nq
    step = lambda bi, h, i: (bi * N_HEADS + h) * nq + i
    cast_in, cast_specs, cast_out_specs, cast_out_shapes = [], [], [], []
    for w_all, li in casts:
        cols = w_all.shape[-1]
        rows = math.prod(w_all.shape[1:-1]) // nsteps
        cast_in.append(w_all.reshape(w_all.shape[0], nsteps, rows, cols))
        cast_specs.append(pl.BlockSpec((1, 1, rows, cols), lambda bi, h, i, li=li: (li, step(bi, h, i), 0, 0)))
        cast_out_specs.append(pl.BlockSpec((1, rows, cols), lambda bi, h, i: (step(bi, h, i), 0, 0)))
        cast_out_shapes.append(jax.ShapeDtypeStruct((nsteps, rows, cols), BF16))
    out = pl.pallas_call(
        functools.partial(_attn_body, t=t, cw=ATTN_CW, n_cast=len(casts)),
        grid=(b, N_HEADS, nq),
        in_specs=[pl.BlockSpec(memory_space=pltpu.SMEM),
                  pl.BlockSpec((1, t, HEAD), lambda bi, h, i: (bi, i, h)),
                  pl.BlockSpec((1, s, HEAD), lambda bi, h, i: (bi, 0, N_HEADS + h)),
                  pl.BlockSpec((1, s, HEAD), lambda bi, h, i: (bi, 0, 2 * N_HEADS + h)),
                  pl.BlockSpec((1, 2, t, 2 * t), lambda bi, h, i: (h, 0, 0, 0)),
                  pl.BlockSpec((1, 1, HEAD), lambda bi, h, i: (layer, 0, 0))] + cast_specs,
        out_specs=[pl.BlockSpec((1, t, HEAD), lambda bi, h, i: (bi, i, h))] + cast_out_specs,
        out_shape=[jax.ShapeDtypeStruct((b, s, N_HEADS * HEAD), BF16)] + cast_out_shapes,
        scratch_shapes=[pltpu.VMEM((2 * t, HEAD), BF16),
                        pltpu.VMEM((HEAD + ONES_ROWS, s), BF16),
                        pltpu.VMEM((1, 2 * t), F32),
                        pltpu.VMEM((HEAD + ONES_ROWS, 2 * t), F32),
                        pltpu.VMEM((t, 2 * t), F32),
                        pltpu.VMEM((t, 2 * t), F32)],
        compiler_params=_cparams(3),
        name="diff_attention",
    )(scalars, proj, proj, proj, bias, subln_all, *cast_in)
    return out[0], [o.reshape(w_all.shape[1:]) for o, (w_all, _) in zip(out[1:], casts)]


def _hgrn_levels(c, sub):
    levels = []
    g = c // 2
    while g >= sub:
        levels.append(g)
        g //= 2
    return levels


def _hgrn_consts(l, c, sub):
    r = np.arange(HGRN_GROUP)[:, None]
    s = np.arange(HGRN_GROUP)[None, :]
    lcum = ((r // c == s // c) & (s <= r)).astype(np.float32)
    rc = np.arange(c)[:, None]
    sc = np.arange(c)[None, :]
    masks = [((rc // (2 * g) == sc // (2 * g)) & (rc % (2 * g) >= g) & (sc % (2 * g) < g)).astype(np.float32)
             for g in _hgrn_levels(c, sub)]
    half = c // 2
    rg = np.arange(HGRN_GROUP)[:, None]
    sg = np.arange(HGRN_GROUP)[None, :]
    top = (rg // c == sg // c) & (rg % c >= half) & (sg % c < half)
    near = (rg // half == sg // half) & (sg <= rg)
    fmask = np.stack([top, near]).astype(np.float32)
    return jnp.asarray(lcum, BF16), jnp.asarray(np.stack(masks), F32), jnp.asarray(fmask, F32)


def _hgrn_body(q_ref, f_ref, i_ref, g_ref, lb_ref, ng_ref, lcum_ref, mask_ref, fmask_ref, o_ref,
               st_scr, stb_scr, a_scr, oi_scr, *, l, c, sub):
    nchunk = l // c
    levels = _hgrn_levels(c, sub)

    @pl.when(pl.program_id(2) == 0)
    def _():
        st_scr[...] = jnp.zeros(st_scr.shape, F32)

    q = q_ref[0].astype(F32)
    fr = f_ref[0].astype(F32)
    vb = i_ref[0]
    lb = lb_ref[0]
    f = lb + (1.0 - lb) * jax.nn.sigmoid(fr)
    logf = jnp.log(jnp.maximum(f, jnp.finfo(F32).tiny))
    k = (1.0 - lb) * jax.nn.sigmoid(-fr)

    hi = logf.astype(BF16)
    lo = (logf - hi.astype(F32)).astype(BF16)
    hilo = jnp.concatenate([hi, lo], axis=1)
    lcum = lcum_ref[...]
    grp = lcum.shape[0]
    parts = []
    for gi in range(l // grp):
        both = jnp.dot(lcum, hilo[gi * grp:(gi + 1) * grp], preferred_element_type=F32)
        parts.append(both[:, 0:HEAD] + both[:, HEAD:2 * HEAD])
    bcum = jnp.concatenate(parts, axis=0)

    def rows_of(arr, group, row):
        a3 = arr.reshape(l // group, group, HEAD)
        return jnp.broadcast_to(a3[:, row:row + 1, :], a3.shape).reshape(l, HEAD)

    nt = (((1,), (1,)), ((), ()))
    tn = (((0,), (0,)), ((), ()))

    def level_operands(g):
        ref = rows_of(bcum, 2 * g, g - 1)
        return ((q * jnp.exp(jnp.minimum(bcum - ref, 0.0))).astype(BF16),
                (k * jnp.exp(jnp.minimum(ref - bcum, 0.0))).astype(BF16))

    q_top, k_top = level_operands(levels[0])

    def top_scores(sl):
        return mask_ref[0] * lax.dot_general(q_top[sl], k_top[sl], nt, preferred_element_type=F32)

    half = c // 2
    b_start = rows_of(bcum, half, 0) - rows_of(logf, half, 0)
    span = b_start - bcum
    bounded = jnp.max(span) <= HGRN_SAFE_SPAN

    @pl.when(bounded)
    def _():
        q_f = (q * jnp.exp(-span)).astype(BF16)
        k_f = (k * jnp.exp(span)).astype(BF16)
        grp = fmask_ref.shape[-1]
        for gi in range(l // grp):
            sl = slice(gi * grp, (gi + 1) * grp)
            a = fmask_ref[0] * lax.dot_general(q_top[sl], k_top[sl], nt, preferred_element_type=F32)
            a = a + fmask_ref[1] * lax.dot_general(q_f[sl], k_f[sl], nt, preferred_element_type=F32)
            a_scr[sl, :] = jnp.dot(a.astype(BF16), vb[sl], preferred_element_type=F32)

    @pl.when(jnp.logical_not(bounded))
    def _():
        lower = [level_operands(g) for g in levels[1:]]
        nb = l // sub
        b3 = bcum.reshape(nb, sub, HEAD)
        q3 = q.reshape(nb, sub, HEAD)
        k3 = k.reshape(nb, sub, HEAD)
        v3 = vb.astype(F32).reshape(nb, sub, HEAD)
        tloc = lax.broadcasted_iota(jnp.int32, (nb, sub, 1), 1)
        od = jnp.zeros((nb, sub, HEAD), F32)
        for si in range(sub):
            e = jnp.exp(b3 - b3[:, si:si + 1, :])
            pr = q3 * (k3[:, si:si + 1, :] * e)
            w = jnp.sum(pr, axis=-1, keepdims=True)
            w = jnp.where(tloc >= si, w, 0.0)
            od = od + w * v3[:, si:si + 1, :]
        od = od.reshape(l, HEAD)
        for ci in range(nchunk):
            sl = slice(ci * c, (ci + 1) * c)
            a = top_scores(sl)
            for li, (q_g, k_g) in enumerate(lower):
                a = a + mask_ref[li + 1] * lax.dot_general(q_g[sl], k_g[sl], nt, preferred_element_type=F32)
            a_scr[sl, :] = jnp.dot(a.astype(BF16), vb[sl], preferred_element_type=F32) + od[sl]

    b_end = rows_of(bcum, c, c - 1)
    q_in = (q * jnp.exp(bcum)).astype(BF16)
    k_out = (k * jnp.exp(b_end - bcum)).astype(BF16)
    dec = jnp.exp(b_end)
    kv = [lax.dot_general(vb[ci * c:(ci + 1) * c], k_out[ci * c:(ci + 1) * c], tn, preferred_element_type=F32)
          for ci in range(nchunk)]
    st = st_scr[...]
    for ci in range(nchunk):
        stb_scr[ci] = st.astype(BF16)
        st = st * dec[ci * c:ci * c + 1, :] + kv[ci]
    st_scr[...] = st
    for ci in range(nchunk):
        sl = slice(ci * c, (ci + 1) * c)
        oi_scr[sl, :] = lax.dot_general(q_in[sl], stb_scr[ci], nt, preferred_element_type=F32)

    o = a_scr[...] + oi_scr[...]
    ms = jnp.mean(o * o, axis=-1, keepdims=True)
    o = o * lax.rsqrt(ms + EPS) * ng_ref[0]
    gate = g_ref[0].astype(F32)
    o_ref[0] = (o * (gate * jax.nn.sigmoid(gate))).astype(BF16)


def _hgrn(proj, lower_bounds, norm_all, layer, l=HGRN_L, c=HGRN_C, sub=HGRN_SUB):
    b, s, _ = proj.shape
    lcum, masks, fmask = _hgrn_consts(l, c, sub)
    base = 3 * N_HEADS

    def col(kind):
        return pl.BlockSpec((1, l, HEAD), lambda bi, h, i: (bi, i, base + kind * N_HEADS + h))

    return pl.pallas_call(
        functools.partial(_hgrn_body, l=l, c=c, sub=sub),
        grid=(b, N_HEADS, s // l),
        in_specs=[col(0), col(1), col(2), col(3),
                  pl.BlockSpec((1, 1, HEAD), lambda bi, h, i: (layer, 0, h)),
                  pl.BlockSpec((1, 1, HEAD), lambda bi, h, i: (layer, 0, 0)),
                  pl.BlockSpec(lcum.shape, lambda bi, h, i: (0, 0)),
                  pl.BlockSpec(masks.shape, lambda bi, h, i: (0, 0, 0)),
                  pl.BlockSpec(fmask.shape, lambda bi, h, i: (0, 0, 0))],
        out_specs=pl.BlockSpec((1, l, HEAD), lambda bi, h, i: (bi, i, h)),
        out_shape=jax.ShapeDtypeStruct((b, s, N_HEADS * HEAD), BF16),
        scratch_shapes=[pltpu.VMEM((HEAD, HEAD), F32),
                        pltpu.VMEM((l // c, HEAD, HEAD), BF16),
                        pltpu.VMEM((l, HEAD), F32),
                        pltpu.VMEM((l, HEAD), F32)],
        compiler_params=_cparams(3),
        name="hgrn2",
    )(proj, proj, proj, proj, lower_bounds, norm_all, lcum, masks, fmask)


def _first_row(cond, row):
    return jnp.min(jnp.where(cond, row, N_EXPERTS), axis=0, keepdims=True)


def _outproj_body(a_ref, hg_ref, w_ref, x_ref, mod_ref, g_ref, wrh_ref, wrl_ref, br_ref, xo_ref, h_ref, *, d):
    da = a_ref.shape[-1]
    mixed = jnp.dot(a_ref[0], w_ref[0, 0:da, :], preferred_element_type=F32)
    mixed = mixed + jnp.dot(hg_ref[0], w_ref[0, da:, :], preferred_element_type=F32)
    x = x_ref[0] + mod_ref[0, 2:3, :] * mixed
    xo_ref[0] = x
    h = _modulated_norm(x, g_ref[0], mod_ref[0, 4:5, :], mod_ref[0, 3:4, :])
    h_ref[0, :, 0:d] = h

    h_hi = h.astype(BF16)
    h_lo = (h - h_hi.astype(F32)).astype(BF16)
    logits = (jnp.dot(h_hi, wrh_ref[...], preferred_element_type=F32)
              + jnp.dot(h_lo, wrh_ref[...], preferred_element_type=F32)
              + jnp.dot(h_hi, wrl_ref[...], preferred_element_type=F32)) + br_ref[...]
    lt = logits.T[0:N_EXPERTS, :]
    erow = lax.broadcasted_iota(jnp.int32, lt.shape, 0)
    mx = jnp.max(lt, axis=0, keepdims=True)
    ex = jnp.exp(lt - mx)
    probs = ex / jnp.sum(ex, axis=0, keepdims=True)

    best = sel = v1 = v2 = i1 = i2 = None
    for gi in range(N_GROUPS):
        ing = (erow >= gi * GROUP_SIZE) & (erow < (gi + 1) * GROUP_SIZE)
        pg = jnp.where(ing, probs, -1.0)
        m1 = jnp.max(pg, axis=0, keepdims=True)
        a1 = _first_row(pg == m1, erow)
        pg2 = jnp.where(erow == a1, -1.0, pg)
        m2 = jnp.max(pg2, axis=0, keepdims=True)
        a2 = _first_row(pg2 == m2, erow)
        score = m1 + m2
        if gi == 0:
            best, sel, v1, v2, i1, i2 = score, jnp.zeros_like(a1), m1, m2, a1, a2
        else:
            better = score > best
            best = jnp.where(better, score, best)
            sel = jnp.where(better, gi, sel)
            v1 = jnp.where(better, m1, v1)
            v2 = jnp.where(better, m2, v2)
            i1 = jnp.where(better, a1, i1)
            i2 = jnp.where(better, a2, i2)
    tot = v1 + v2
    w1 = v1 / tot
    w2 = v2 / tot
    loc1 = i1 - sel * GROUP_SIZE
    loc2 = i2 - sel * GROUP_SIZE
    xrow = lax.broadcasted_iota(jnp.int32, (LANES, lt.shape[1]), 0)
    ext = jnp.where(xrow == loc1, w1, 0.0) + jnp.where(xrow == loc2, w2, 0.0)
    ext = jnp.where(xrow == GROUP_SIZE, sel.astype(F32), ext)
    h_ref[0, :, d:] = ext.T


def _outproj_router(attn_o, hgrn_o, w_out, x, mod, ffn_norm_all, wr_hi, wr_lo, br, layer, tm=512):
    w_out_all = w_out[None]
    b, s, d = x.shape
    da = attn_o.shape[-1]
    dmix = w_out_all.shape[1]
    tok = lambda bi, i: (bi, i, 0)
    return pl.pallas_call(
        functools.partial(_outproj_body, d=d),
        grid=(b, s // tm),
        in_specs=[pl.BlockSpec((1, tm, da), tok),
                  pl.BlockSpec((1, tm, dmix - da), tok),
                  pl.BlockSpec((1, dmix, d), lambda bi, i: (0, 0, 0)),
                  pl.BlockSpec((1, tm, d), tok),
                  pl.BlockSpec((1, N_MOD, d), lambda bi, i: (bi, 0, 0)),
                  pl.BlockSpec((1, 1, d), lambda bi, i: (layer, 0, 0)),
                  pl.BlockSpec((d, LANES), lambda bi, i: (0, 0)),
                  pl.BlockSpec((d, LANES), lambda bi, i: (0, 0)),
                  pl.BlockSpec((1, LANES), lambda bi, i: (0, 0))],
        out_specs=[pl.BlockSpec((1, tm, d), tok),
                   pl.BlockSpec((1, tm, d + LANES), tok)],
        out_shape=[jax.ShapeDtypeStruct((b, s, d), F32),
                   jax.ShapeDtypeStruct((b, s, d + LANES), F32)],
        compiler_params=_cparams(2),
        name="outproj_router",
    )(attn_o, hgrn_o, w_out_all, x, mod, ffn_norm_all, wr_hi, wr_lo, br)


def _moe_body(tg_ref, nv_ref, nu_ref, tok_ref, h_hbm, wg_ref, wu_ref, wd_ref, y_hbm,
              xbuf, xbf, acc, gsem, ssem, *, tm, d):
    i = pl.program_id(0)
    e = pl.program_id(1)
    n_used = nu_ref[0]
    slot = lax.rem(i, 2)

    def gather_start(tile, sl):
        base = tile * tm

        for r in range(tm):
            pltpu.make_async_copy(h_hbm.at[pl.ds(tok_ref[base + r], 1)], xbuf.at[sl, pl.ds(r, 1)],
                                  gsem.at[sl]).start(priority=r % 2)

    def gather_wait(sl):
        pltpu.make_async_copy(h_hbm.at[pl.ds(0, tm)], xbuf.at[sl], gsem.at[sl]).wait()

    def scatter_start(tile, sl, n):
        base = tile * tm

        def body(r, carry):
            pltpu.make_async_copy(acc.at[sl, pl.ds(r, 1)], y_hbm.at[pl.ds(tok_ref[base + r], 1)], ssem.at[sl]).start()
            return carry

        @pl.when(n == tm)
        def _():
            for r in range(tm):
                pltpu.make_async_copy(acc.at[sl, pl.ds(r, 1)], y_hbm.at[pl.ds(tok_ref[base + r], 1)],
                                      ssem.at[sl]).start(priority=r % 2)

        @pl.when(n < tm)
        def _():
            lax.fori_loop(0, n, body, 0)

    def scatter_wait(sl, n):
        @pl.when(n == tm)
        def _():
            pltpu.make_async_copy(acc.at[sl], y_hbm.at[pl.ds(0, tm)], ssem.at[sl]).wait()

        @pl.when(n < tm)
        def _():
            def body(r, carry):
                pltpu.make_async_copy(acc.at[sl, pl.ds(r, 1)], y_hbm.at[pl.ds(r, 1)], ssem.at[sl]).wait()
                return carry

            lax.fori_loop(0, n, body, 0)

    @pl.when(i < n_used)
    def _():
        @pl.when(e == 0)
        def _():
            @pl.when(i == 0)
            def _():
                gather_start(0, 0)

            gather_wait(slot)

            @pl.when(i + 1 < n_used)
            def _():
                gather_start(i + 1, 1 - slot)

            xbf[...] = xbuf[slot, :, 0:d].astype(BF16)

            @pl.when(i >= 2)
            def _():
                scatter_wait(slot, nv_ref[jnp.maximum(i - 2, 0)])

            acc[slot] = jnp.zeros((tm, d), F32)

        ext = xbuf[slot, :, d:d + LANES]
        gate = jnp.zeros((tm, 1), F32)
        for ei in range(GROUP_SIZE):
            gate = jnp.where(e == ei, ext[:, ei:ei + 1], gate)
        x = xbf[...]
        hg = jnp.dot(x, wg_ref[0, 0], preferred_element_type=F32)
        hu = jnp.dot(x, wu_ref[0, 0], preferred_element_type=F32)
        act = (hg * jax.nn.sigmoid(hg)) * hu * gate
        acc[slot] += jnp.dot(act.astype(BF16), wd_ref[0, 0], preferred_element_type=F32)

        @pl.when(e == GROUP_SIZE - 1)
        def _():
            scatter_start(i, slot, nv_ref[i])

            @pl.when(i == n_used - 1)
            def _():
                scatter_wait(slot, nv_ref[i])

                @pl.when(i >= 1)
                def _():
                    scatter_wait(1 - slot, nv_ref[jnp.maximum(i - 1, 0)])


def _route(hext, d, tm):
    b, s, _ = hext.shape
    t = b * s
    n_tiles = t // tm + N_GROUPS
    gid = hext[:, :, d + GROUP_SIZE].reshape(t).astype(jnp.int32)
    onehot = (gid[:, None] == jnp.arange(N_GROUPS, dtype=jnp.int32)[None, :]).astype(jnp.int32)
    cnt = jnp.sum(onehot, axis=0)
    rank = jnp.sum(onehot * (jnp.cumsum(onehot, axis=0) - onehot), axis=1)
    ntile_g = (cnt + tm - 1) // tm
    tile_end_g = jnp.cumsum(ntile_g)
    tile_start_g = tile_end_g - ntile_g
    pos = (tile_start_g * tm)[gid] + rank
    tok_of_row = jnp.zeros((n_tiles * tm,), jnp.int32).at[pos].set(jnp.arange(t, dtype=jnp.int32))
    tile = jnp.arange(n_tiles, dtype=jnp.int32)
    tile_gid = jnp.minimum(jnp.sum((tile[:, None] >= tile_end_g[None, :]).astype(jnp.int32), axis=1), N_GROUPS - 1)
    nvalid = jnp.clip(cnt[tile_gid] - (tile - tile_start_g[tile_gid]) * tm, 0, tm).astype(jnp.int32)
    n_used = tile_end_g[N_GROUPS - 1:].astype(jnp.int32)
    return tile_gid.astype(jnp.int32), nvalid, n_used, tok_of_row


def _moe(hext, wg, wu, wd, tm=MOE_TM):
    wg_all, wu_all, wd_all, layer = wg[None], wu[None], wd[None], 0
    b, s, dx = hext.shape
    d = dx - LANES
    t = b * s
    f = wg_all.shape[-1]
    tile_gid, nvalid, n_used, tok_of_row = _route(hext, d, tm)
    n_tiles = tile_gid.shape[0]

    def wmap(i, e, tg, nv, nu, tok):
        return (layer, tg[i] * GROUP_SIZE + e, 0, 0)

    return pl.pallas_call(
        functools.partial(_moe_body, tm=tm, d=d),
        grid_spec=pltpu.PrefetchScalarGridSpec(
            num_scalar_prefetch=4,
            grid=(n_tiles, GROUP_SIZE),
            in_specs=[pl.BlockSpec(memory_space=pl.ANY),
                      pl.BlockSpec((1, 1, d, f), wmap),
                      pl.BlockSpec((1, 1, d, f), wmap),
                      pl.BlockSpec((1, 1, f, d), wmap)],
            out_specs=pl.BlockSpec(memory_space=pl.ANY),
            scratch_shapes=[pltpu.VMEM((2, tm, dx), F32),
                            pltpu.VMEM((tm, d), BF16),
                            pltpu.VMEM((2, tm, d), F32),
                            pltpu.SemaphoreType.DMA((2,)),
                            pltpu.SemaphoreType.DMA((2,))],
        ),
        out_shape=jax.ShapeDtypeStruct((t, d), F32),
        compiler_params=_cparams(2),
        name="moe_experts",
    )(tile_gid, nvalid, n_used, tok_of_row, hext.reshape(t, dx), wg_all, wu_all, wd_all)


def _final_body(x_ref, y_ref, mod_ref, g_ref, o_ref):
    x = x_ref[0] + mod_ref[0, 5:6, :] * y_ref[...]
    ms = jnp.mean(x * x, axis=-1, keepdims=True)
    o_ref[0] = x * lax.rsqrt(ms + EPS) * g_ref[...]


def _final(x, y, mod, g, tm=512):
    b, s, d = x.shape
    nt = s // tm
    return pl.pallas_call(
        _final_body,
        grid=(b, nt),
        in_specs=[pl.BlockSpec((1, tm, d), lambda bi, i: (bi, i, 0)),
                  pl.BlockSpec((tm, d), lambda bi, i: (bi * nt + i, 0)),
                  pl.BlockSpec((1, N_MOD, d), lambda bi, i: (bi, 0, 0)),
                  pl.BlockSpec((1, d), lambda bi, i: (0, 0))],
        out_specs=pl.BlockSpec((1, tm, d), lambda bi, i: (bi, i, 0)),
        out_shape=jax.ShapeDtypeStruct(x.shape, F32),
        compiler_params=_cparams(2),
        name="final_norm",
    )(x, y, mod, g.reshape(1, d))


def kernel(x, c, w_in, w_out, attn_norm, ffn_norm, w_ada, b_ada, lambda_q1, lambda_k1, lambda_q2, lambda_k2,
           diff_subln, hgrn_lb, hgrn_norm, rel_bias, w_router, b_router, w_gate, w_up, w_down, final_norm):
    b, s, d = x.shape
    depth = w_in.shape[0]

    w_in_cur = w_in[0].astype(BF16)
    wr =jnp.pad(w_router.astype(F32), ((0, 0), (0, LANES - N_EXPERTS)))
    wr_hi = wr.astype(BF16)
    wr_lo = (wr - wr_hi.astype(F32)).astype(BF16)
    br = jnp.pad(b_router.astype(F32), (0, LANES - N_EXPERTS)).reshape(1, LANES)
    lb_soft = jax.nn.softmax(hgrn_lb.astype(F32), axis=0)
    lower_bounds = jnp.maximum(jnp.cumsum(lb_soft, axis=0) - lb_soft[0:1], 0.0).reshape(depth, 1, -1)
    attn_norm3 = attn_norm.astype(F32).reshape(depth, 1, d)
    ffn_norm3 = ffn_norm.astype(F32).reshape(depth, 1, d)
    subln3 = diff_subln.astype(F32).reshape(depth, 1, HEAD)
    hnorm3 = hgrn_norm.astype(F32).reshape(depth, 1, HEAD)

    c8 = jnp.pad(c.astype(F32), ((0, 8 - b), (0, 0)))
    mods = _ada(c8, w_ada, b_ada)
    bias = _bias_tiles(rel_bias, ATTN_T)

    y = None
    mod_prev = None
    for layer in range(depth):
        mod = mods[layer, :b].reshape(b, N_MOD, d)
        lambda_init = 0.8 - 0.6 * math.exp(-0.3 * layer)
        lam = (jnp.exp(jnp.sum(lambda_q1[layer].astype(F32) * lambda_k1[layer].astype(F32)))
               - jnp.exp(jnp.sum(lambda_q2[layer].astype(F32) * lambda_k2[layer].astype(F32)))
               + lambda_init)
        scalars = jnp.stack([lam, jnp.asarray(1.0 - lambda_init, F32)]).astype(F32)

        x, proj = _inproj(x, y, mod_prev, mod, attn_norm3, w_in_cur, layer)
        casts = [(w_out, layer), (w_gate, layer), (w_up, layer), (w_down, layer)]
        if layer + 1 < depth:
            casts.append((w_in, layer + 1))
        attn_o, converted = _attention(proj, bias, scalars, subln3, layer, casts)
        w_out_b, w_gate_b, w_up_b, w_down_b = converted[:4]
        if layer + 1 < depth:
            w_in_cur = converted[4]
        hgrn_o = _hgrn(proj, lower_bounds, hnorm3, layer)
        x, hext = _outproj_router(attn_o, hgrn_o, w_out_b, x, mod, ffn_norm3, wr_hi, wr_lo, br, layer)
        y = _moe(hext, w_gate_b, w_up_b, w_down_b)
        mod_prev = mod
    return _final(x, y, mod_prev, final_norm.astype(F32))
```
